```python
import math
import jax, jax.numpy as jnp
from jax import lax
import numpy as np

D_MODEL = 1024
BATCH = 8
SEQ = 4096
DEPTH = 1

CHUNK = 64
Q_BLOCK = 128
EPS = 1e-6

MLA_HEADS = 8
MLA_NOPE = 64
MLA_ROPE = 32
MLA_V = 64
MLA_QK = MLA_NOPE + MLA_ROPE
Q_LORA = 768
KV_LORA = 256
ROPE_THETA = 10000.0
MLA_WIDTH = MLA_HEADS * MLA_V

HG_HEADS = 8
HG_DK = 64
HG_DV = 64
HG_KEY_WIDTH = HG_HEADS * HG_DK
HG_WIDTH = HG_HEADS * HG_DV
HG_BLOCK = 32

N_BRANCH = 2
IN_SPLITS = (Q_LORA, KV_LORA, MLA_ROPE, MLA_WIDTH,
             HG_KEY_WIDTH, HG_KEY_WIDTH, HG_WIDTH, HG_WIDTH,
             N_BRANCH * D_MODEL)
D_IN = sum(IN_SPLITS)

kernel_name = "hybrid_mla_hgrn2_gated_merge"


def rmsnorm(x, g):
    xf = x.astype(jnp.float32)
    y = xf * lax.rsqrt(jnp.mean(xf * xf, axis=-1, keepdims=True) + EPS)
    return (y * g.astype(jnp.float32)).astype(x.dtype)


def rope_tables(seq):
    inv = ROPE_THETA ** (-jnp.arange(0, MLA_ROPE, 2, dtype=jnp.float32) / MLA_ROPE)
    ang = jnp.arange(seq, dtype=jnp.float32)[:, None] * inv[None, :]
    return jnp.cos(ang), jnp.sin(ang)


def apply_rope(x, cos, sin):
    half = MLA_ROPE // 2
    xf = x.astype(jnp.float32)
    x1, x2 = xf[..., :half], xf[..., half:]
    c = cos[None, :, None, :]
    s = sin[None, :, None, :]
    return jnp.concatenate([x1 * c - x2 * s, x1 * s + x2 * c], axis=-1).astype(x.dtype)


def chunk_causal_attention(q, k, v):
    b, h, s, dq = q.shape
    nqb = s // Q_BLOCK
    qb = q.reshape(b, h, nqb, Q_BLOCK, dq).transpose(2, 0, 1, 3, 4)
    key_chunk = jnp.arange(s) // CHUNK
    q_chunk = (jnp.arange(s) // CHUNK).reshape(nqb, Q_BLOCK)
    scale = 1.0 / math.sqrt(dq)

    def one_block(args):
        qi, qc = args
        sc = jnp.einsum('bhqd,bhkd->bhqk', qi, k).astype(jnp.float32) * scale
        mask = key_chunk[None, :] <= qc[:, None]
        sc = jnp.where(mask, sc, -jnp.inf)
        p = jax.nn.softmax(sc, axis=-1).astype(v.dtype)
        return jnp.einsum('bhqk,bhkd->bhqd', p, v)

    out = lax.map(one_block, (qb, q_chunk))
    return out.transpose(1, 2, 0, 3, 4).reshape(b, h, s, -1)


def hgrn2_chunkwise(q, k, v, log_f):
    b, s, h, dk = q.shape
    dv = v.shape[-1]
    n = s // HG_BLOCK

    def blocks(t):
        return t.astype(jnp.float32).reshape(b, n, HG_BLOCK, h, -1).transpose(0, 3, 1, 2, 4)

    qc, kc, vc, gc = blocks(q), blocks(k), blocks(v), blocks(log_f)
    cum = jnp.cumsum(gc, axis=3)
    last = cum[..., -1:, :]
    q_dec = qc * jnp.exp(cum)
    k_inv = kc * jnp.exp(-cum)
    k_end = kc * jnp.exp(last - cum)
    causal = jnp.tril(jnp.ones((HG_BLOCK, HG_BLOCK), dtype=bool))
    a = jnp.where(causal, jnp.einsum('bhntk,bhnsk->bhnts', q_dec, k_inv), 0.0)
    o_intra = jnp.einsum('bhnts,bhnsv->bhntv', a, vc)
    upd = jnp.einsum('bhnsk,bhnsv->bhnkv', k_end, vc)
    decay = jnp.exp(last[..., 0, :])

    def step(state, xs):
        d, u = xs
        return d[..., None] * state + u, state

    init = jnp.zeros((b, h, dk, dv), jnp.float32)
    _, s_prev = lax.scan(step, init, (jnp.moveaxis(decay, 2, 0), jnp.moveaxis(upd, 2, 0)))
    s_prev = jnp.moveaxis(s_prev, 0, 2)
    o_inter = jnp.einsum('bhntk,bhnkv->bhntv', q_dec, s_prev)
    return (o_intra + o_inter).transpose(0, 2, 3, 1, 4).reshape(b, s, h, dv)


def setup_inputs(seed: int = 0) -> dict:
    key = jax.random.key(seed)
    ks = jax.random.split(key, 14)

    def nrm(k, shape, fan_in):
        return jax.random.normal(k, shape, jnp.float32) * (fan_in ** -0.5)

    def gain(k, shape):
        return 1.0 + 0.02 * jax.random.normal(k, shape, jnp.float32)

    return {
        "x": jax.random.normal(ks[0], (BATCH, SEQ, D_MODEL), jnp.float32),
        "g_pre": gain(ks[1], (DEPTH, D_MODEL)),
        "w_in": nrm(ks[2], (DEPTH, D_MODEL, D_IN), D_MODEL),
        "b_gate": 0.01 * jax.random.normal(ks[3], (DEPTH, N_BRANCH * D_MODEL), jnp.float32),
        "g_q": gain(ks[4], (DEPTH, Q_LORA)),
        "w_uq": nrm(ks[5], (DEPTH, Q_LORA, MLA_HEADS * MLA_QK), Q_LORA),
        "g_kv": gain(ks[6], (DEPTH, KV_LORA)),
        "w_ukv": nrm(ks[7], (DEPTH, KV_LORA, MLA_HEADS * (MLA_NOPE + MLA_V)), KV_LORA),
        "lb_logits": 0.1 * jax.random.normal(ks[8], (DEPTH + 1, HG_KEY_WIDTH), jnp.float32),
        "g_hgrn": gain(ks[9], (DEPTH, HG_DV)),
        "w_branch_a": nrm(ks[10], (DEPTH, MLA_WIDTH, D_MODEL), MLA_WIDTH),
        "w_branch_b": nrm(ks[11], (DEPTH, HG_WIDTH, D_MODEL), HG_WIDTH),
        "w_out": nrm(ks[12], (DEPTH, D_MODEL, D_MODEL), D_MODEL),
        "g_post": gain(ks[13], (DEPTH, D_MODEL)),
    }


def reference(x, g_pre, w_in, b_gate, g_q, w_uq, g_kv, w_ukv, lb_logits,
              g_hgrn, w_branch_a, w_branch_b, w_out, g_post):
    b, s, _ = x.shape
    cos, sin = rope_tables(s)
    split_at = [int(o) for o in np.cumsum(IN_SPLITS)[:-1]]
    lower_bounds = jnp.cumsum(jax.nn.softmax(lb_logits.astype(jnp.float32), axis=0), axis=0)

    for l in range(DEPTH):
        h = rmsnorm(x, g_pre[l])
        proj = h @ w_in[l]
        (c_q, c_kv, k_pe, gate_a, hq, hf, hi, gate_b, merge_logits) = jnp.split(proj, split_at, axis=-1)

        q = (rmsnorm(c_q, g_q[l]) @ w_uq[l]).reshape(b, s, MLA_HEADS, MLA_QK)
        q_nope, q_pe = q[..., :MLA_NOPE], apply_rope(q[..., MLA_NOPE:], cos, sin)
        kv = (rmsnorm(c_kv, g_kv[l]) @ w_ukv[l]).reshape(b, s, MLA_HEADS, MLA_NOPE + MLA_V)
        k_nope, v = kv[..., :MLA_NOPE], kv[..., MLA_NOPE:]
        k_pe = jnp.broadcast_to(apply_rope(k_pe[:, :, None, :], cos, sin), (b, s, MLA_HEADS, MLA_ROPE))
        qf = jnp.concatenate([q_nope, q_pe], axis=-1).transpose(0, 2, 1, 3)
        kf = jnp.concatenate([k_nope, k_pe], axis=-1).transpose(0, 2, 1, 3)
        vf = v.transpose(0, 2, 1, 3)
        attn = chunk_causal_attention(qf, kf, vf).transpose(0, 2, 1, 3).reshape(b, s, MLA_WIDTH)
        y_a = (attn * jax.nn.silu(gate_a)) @ w_branch_a[l]

        lb = lower_bounds[l]
        f = lb + (1.0 - lb) * jax.nn.sigmoid(hf.astype(jnp.float32))
        log_f = jnp.log(f).reshape(b, s, HG_HEADS, HG_DK)
        k_in = (1.0 - f).reshape(b, s, HG_HEADS, HG_DK)
        o = hgrn2_chunkwise(hq.reshape(b, s, HG_HEADS, HG_DK), k_in,
                            hi.reshape(b, s, HG_HEADS, HG_DV), log_f)
        o = rmsnorm(o, g_hgrn[l]).astype(x.dtype).reshape(b, s, HG_WIDTH)
        y_b = (o * jax.nn.silu(gate_b)) @ w_branch_b[l]

        gates = jax.nn.sigmoid((merge_logits + b_gate[l]).astype(jnp.float32)).astype(x.dtype)
        m = gates[..., :D_MODEL] * y_a + gates[..., D_MODEL:] * y_b
        y = m @ w_out[l]
        x = x + rmsnorm(y, g_post[l])
    return x
```

```python
import functools
import math

import jax
import jax.numpy as jnp
import numpy as np
from jax import lax
from jax.experimental import pallas as pl
from jax.experimental.pallas import tpu as pltpu

F32 = jnp.float32
BF16 = jnp.bfloat16

D_MODEL = 1024
CHUNK = 64
EPS = 1e-6

MLA_HEADS = 8
MLA_NOPE = 64
MLA_ROPE = 32
MLA_V = 64
MLA_QK = MLA_NOPE + MLA_ROPE
Q_LORA = 768
KV_LORA = 256
ROPE_THETA = 10000.0
MLA_WIDTH = MLA_HEADS * MLA_V

HG_HEADS = 8
HG_DK = 64
HG_DV = 64
HG_WIDTH = HG_HEADS * HG_DV

LANES = 128
HEAD_PAD = LANES
HEADS_PER_GROUP = LANES // MLA_V
N_GROUPS = MLA_HEADS // HEADS_PER_GROUP

COL_CQ = 0
COL_CKV = COL_CQ + Q_LORA
COL_KPE = COL_CKV + KV_LORA
COL_GA = COL_KPE + HEAD_PAD
COL_HQ = COL_GA + MLA_WIDTH
COL_HF = COL_HQ + HG_WIDTH
COL_HI = COL_HF + HG_WIDTH
COL_GB = COL_HI + HG_WIDTH
COL_MG = COL_GB + HG_WIDTH
D_IN_PAD = COL_MG + 2 * D_MODEL

VMEM_LIMIT = 56 * 1024 * 1024

TM_PROJ = 512
TQ_ATTN = 256
TC_HGRN = 256
C_HGRN = 64
TM_OUT = 512


def _sigmoid(z):
    return 1.0 / (1.0 + jnp.exp(-z))


def _rms(t):
    return t * lax.rsqrt(jnp.mean(t * t, axis=-1, keepdims=True) + EPS)


def _dot(a, b):
    return jnp.dot(a, b, preferred_element_type=F32)


def _dot_nt(a, b):
    return lax.dot_general(a, b, (((1,), (1,)), ((), ())), preferred_element_type=F32)


def _dot_split(a_bf16, b_f32, terms):
    acc = None
    rem = b_f32
    for _ in range(terms):
        piece = rem.astype(BF16)
        rem = rem - piece.astype(F32)
        part = _dot(a_bf16, piece)
        acc = part if acc is None else acc + part
    return acc


def _proj_body(x_ref, gpre_ref, win_ref, gq_ref, wuq_ref, gkv_ref, wukv_ref, ra_ref, rs_ref, bg_ref,
               q_ref, k_ref, v_ref, ga_ref, hq_ref, hf_ref, hi_ref, gb_ref, gt_ref):
    h = (_rms(x_ref[0]) * gpre_ref[...]).astype(BF16)

    def proj(lo, width):
        return _dot(h, win_ref[:, lo:lo + width])

    ra = ra_ref[...]
    rs = rs_ref[...]
    lane = lax.broadcasted_iota(jnp.int32, ra.shape, 1)
    takes_upper = lane < MLA_NOPE + MLA_ROPE // 2

    def rope(t):
        partner = jnp.where(takes_upper, pltpu.roll(t, LANES - MLA_ROPE // 2, 1), pltpu.roll(t, MLA_ROPE // 2, 1))
        return t * ra + partner * rs

    scale = 1.0 / math.sqrt(MLA_QK)
    cqn = (_rms(proj(COL_CQ, Q_LORA)) * gq_ref[...]).astype(BF16)
    qu = _dot(cqn, wuq_ref[...])
    for hh in range(MLA_HEADS):
        q_ref[0, hh] = (rope(qu[:, hh * HEAD_PAD:(hh + 1) * HEAD_PAD]) * scale).astype(BF16)

    ckvn = (_rms(proj(COL_CKV, KV_LORA)) * gkv_ref[...]).astype(BF16)
    kvu = _dot(ckvn, wukv_ref[...])
    kpe = rope(proj(COL_KPE, HEAD_PAD))
    for hh in range(MLA_HEADS):
        k_ref[0, hh] = (kvu[:, hh * HEAD_PAD:(hh + 1) * HEAD_PAD] + kpe).astype(BF16)
    v_ref[0] = kvu[:, MLA_HEADS * HEAD_PAD:].astype(BF16)

    ga = proj(COL_GA, MLA_WIDTH)
    ga_ref[0] = (ga * _sigmoid(ga)).astype(BF16)
    hq_ref[0] = proj(COL_HQ, HG_WIDTH).astype(BF16)
    hf_ref[0] = proj(COL_HF, HG_WIDTH)
    hi_ref[0] = proj(COL_HI, HG_WIDTH).astype(BF16)
    gb = proj(COL_GB, HG_WIDTH)
    gb_ref[0] = (gb * _sigmoid(gb)).astype(BF16)
    step = 512
    for c in range(0, 2 * D_MODEL, step):
        z = proj(COL_MG + c, step) + bg_ref[:, c:c + step]
        gt_ref[0, :, c:c + step] = _sigmoid(z).astype(BF16)


def _proj_call(x, gpre, win, gq, wuq, gkv, wukv, ra, rs, bg):
    b, s, _ = x.shape
    tm = min(TM_PROJ, s)
    const = lambda bi, si: (0, 0)
    tok = lambda bi, si: (bi, si, 0)

    def full(a):
        return pl.BlockSpec(a.shape, const, pipeline_mode=pl.Buffered(1))

    def tok_out(width, dtype):
        return jax.ShapeDtypeStruct((b, s, width), dtype), pl.BlockSpec((1, tm, width), tok)

    head_shape = jax.ShapeDtypeStruct((b, MLA_HEADS, s, HEAD_PAD), BF16)
    head_spec = pl.BlockSpec((1, MLA_HEADS, tm, HEAD_PAD), lambda bi, si: (bi, 0, si, 0))
    outs = [(head_shape, head_spec), (head_shape, head_spec),
            tok_out(MLA_WIDTH, BF16), tok_out(MLA_WIDTH, BF16),
            tok_out(HG_WIDTH, BF16), tok_out(HG_WIDTH, F32), tok_out(HG_WIDTH, BF16), tok_out(HG_WIDTH, BF16),
            tok_out(2 * D_MODEL, BF16)]
    rope_spec = pl.BlockSpec((tm, HEAD_PAD), lambda bi, si: (si, 0))
    return pl.pallas_call(
        _proj_body,
        grid=(b, s // tm),
        in_specs=[pl.BlockSpec((1, tm, D_MODEL), tok), full(gpre), full(win), full(gq), full(wuq), full(gkv),
                  full(wukv), rope_spec, rope_spec, full(bg)],
        out_specs=[o[1] for o in outs],
        out_shape=[o[0] for o in outs],
        compiler_params=pltpu.CompilerParams(dimension_semantics=("parallel", "parallel"),
                                             vmem_limit_bytes=VMEM_LIMIT),
        name="proj",
    )(x, gpre, win, gq, wuq, gkv, wukv, ra, rs, bg)


def _attn_body(q_ref, k_ref, v_ref, o_ref):
    tq = q_ref.shape[2]
    qi = pl.program_id(2)
    row_chunk = lax.broadcasted_iota(jnp.int32, (tq, tq), 0) // CHUNK
    col_chunk = lax.broadcasted_iota(jnp.int32, (tq, tq), 1) // CHUNK
    visible = col_chunk <= row_chunk
    lane = lax.broadcasted_iota(jnp.int32, (tq, LANES), 1)

    outs = []
    for hh in range(HEADS_PER_GROUP):
        q = q_ref[0, hh]

        def block(j, carry, on_diagonal, hh=hh, q=q):
            m, l, acc = carry
            off = pl.multiple_of(j * tq, tq)
            sc = _dot_nt(q, k_ref[0, hh, pl.ds(off, tq), :])
            if on_diagonal:
                sc = jnp.where(visible, sc, -jnp.inf)
            m_new = jnp.maximum(m, jnp.max(sc, axis=-1, keepdims=True))
            alpha = jnp.exp(m - m_new)
            p = jnp.exp(sc - m_new)
            l = alpha * l + jnp.sum(p, axis=-1, keepdims=True)
            acc = alpha * acc + _dot(p.astype(BF16), v_ref[0, pl.ds(off, tq), :])
            return m_new, l, acc

        init = (jnp.full((tq, 1), -jnp.inf, F32), jnp.zeros((tq, 1), F32), jnp.zeros((tq, LANES), F32))
        carry = lax.fori_loop(0, qi, functools.partial(block, on_diagonal=False), init)
        _, l, acc = block(qi, carry, True)
        outs.append(acc / l)
    o_ref[0] = jnp.where(lane < MLA_V, outs[0], outs[1]).astype(BF16)


def _attn_call(q, k, v):
    b, _, s, _ = q.shape
    tq = min(TQ_ATTN, s)
    return pl.pallas_call(
        _attn_body,
        grid=(b, N_GROUPS, s // tq),
        in_specs=[pl.BlockSpec((1, HEADS_PER_GROUP, tq, HEAD_PAD), lambda bi, g, qi: (bi, g, qi, 0)),
                  pl.BlockSpec((1, HEADS_PER_GROUP, s, HEAD_PAD), lambda bi, g, qi: (bi, g, 0, 0)),
                  pl.BlockSpec((1, s, LANES), lambda bi, g, qi: (bi, 0, g))],
        out_specs=pl.BlockSpec((1, tq, LANES), lambda bi, g, qi: (bi, qi, g)),
        out_shape=jax.ShapeDtypeStruct((b, s, MLA_WIDTH), BF16),
        compiler_params=pltpu.CompilerParams(dimension_semantics=("parallel", "parallel", "arbitrary"),
                                             vmem_limit_bytes=VMEM_LIMIT),
        name="attn",
    )(q, k, v)


def _hgrn_body(hq_ref, hf_ref, hi_ref, lb_ref, ghg_ref, ltri_ref, hsum_ref, o_ref, st_ref, o_scr):
    tc = hq_ref.shape[1]
    c_len = C_HGRN
    half = c_len // 2

    @pl.when(pl.program_id(1) == 0)
    def _():
        st_ref[...] = jnp.zeros_like(st_ref)

    lb = lb_ref[...]
    f = lb + (1.0 - lb) * _sigmoid(hf_ref[0])
    k_in = 1.0 - f
    cum = _dot_split(ltri_ref[...], jnp.log(f), 3)
    hq = hq_ref[0].astype(F32)

    def per_chunk_row(r):
        return jnp.concatenate(
            [jnp.broadcast_to(cum[c * c_len + r:c * c_len + r + 1, :], (c_len, HG_WIDTH))
             for c in range(tc // c_len)], axis=0)

    mid = per_chunk_row(half - 1)
    tot = per_chunk_row(c_len - 1)
    q_mid = (hq * jnp.exp(cum - mid)).astype(BF16)
    k_mid = (k_in * jnp.exp(mid - cum)).astype(BF16)
    q_dec = (hq * jnp.exp(cum)).astype(BF16)
    k_end = (k_in * jnp.exp(tot - cum)).astype(BF16)
    decay = jnp.exp(tot)
    v = hi_ref[0]

    lane = lax.broadcasted_iota(jnp.int32, (c_len, LANES), 1)
    low = lane < HG_DK
    causal = (lax.broadcasted_iota(jnp.int32, (2 * c_len, c_len), 0) % c_len
              >= lax.broadcasted_iota(jnp.int32, (2 * c_len, c_len), 1))
    same_head = ((lax.broadcasted_iota(jnp.int32, (LANES, LANES), 0) < HG_DV)
                 == (lax.broadcasted_iota(jnp.int32, (LANES, LANES), 1) < HG_DK))

    for g in range(HG_HEADS // 2):
        cols = slice(g * LANES, (g + 1) * LANES)
        st = st_ref[g]
        for c in range(tc // c_len):
            rows = slice(c * c_len, (c + 1) * c_len)
            qm = q_mid[rows, cols]
            zero = jnp.zeros_like(qm)
            stacked = jnp.concatenate([jnp.where(low, qm, zero), jnp.where(low, zero, qm)], axis=0)
            a = _dot_nt(stacked, k_mid[rows, cols])
            a = jnp.where(causal, a, 0.0).astype(BF16)
            vv = v[rows, cols]
            av = _dot(a, vv)
            o_intra = jnp.where(low, av[:c_len], av[c_len:])
            o_inter = _dot_nt(q_dec[rows, cols], st.astype(BF16))
            o_scr[rows, cols] = o_intra + o_inter
            upd = _dot(vv.astype(F32).T.astype(BF16), k_end[rows, cols])
            st = st * decay[c * c_len:c * c_len + 1, cols] + jnp.where(same_head, upd, 0.0)
        st_ref[g] = st

    o = o_scr[...]
    msq = _dot_split_right(o * o, hsum_ref[...], 2) * (1.0 / HG_DV)
    o_ref[0] = (o * lax.rsqrt(msq + EPS) * ghg_ref[...]).astype(BF16)


def _dot_split_right(a_f32, b_bf16, terms):
    acc = None
    rem = a_f32
    for _ in range(terms):
        piece = rem.astype(BF16)
        rem = rem - piece.astype(F32)
        part = _dot(piece, b_bf16)
        acc = part if acc is None else acc + part
    return acc


def _hgrn_call(hq, hf, hi, lb, ghg):
    b, s, _ = hq.shape
    tc = min(TC_HGRN, s)
    r = np.arange(tc)
    ltri = ((r[:, None] // C_HGRN == r[None, :] // C_HGRN) & (r[:, None] >= r[None, :])).astype(np.float32)
    hcol = np.arange(HG_WIDTH) // HG_DV
    hsum = (hcol[:, None] == hcol[None, :]).astype(np.float32)
    tok = lambda bi, si: (bi, si, 0)
    const = lambda bi, si: (0, 0)
    spec = pl.BlockSpec((1, tc, HG_WIDTH), tok)
    return pl.pallas_call(
        _hgrn_body,
        grid=(b, s // tc),
        in_specs=[spec, spec, spec, pl.BlockSpec((1, HG_WIDTH), const), pl.BlockSpec((1, HG_WIDTH), const),
                  pl.BlockSpec((tc, tc), const), pl.BlockSpec((HG_WIDTH, HG_WIDTH), const)],
        out_specs=spec,
        out_shape=jax.ShapeDtypeStruct((b, s, HG_WIDTH), BF16),
        scratch_shapes=[pltpu.VMEM((HG_HEADS // 2, LANES, LANES), F32), pltpu.VMEM((tc, HG_WIDTH), F32)],
        compiler_params=pltpu.CompilerParams(dimension_semantics=("parallel", "arbitrary"),
                                             vmem_limit_bytes=VMEM_LIMIT),
        name="hgrn",
    )(hq, hf, hi, lb, ghg, jnp.asarray(ltri, BF16), jnp.asarray(hsum, BF16))


def _out_body(x_ref, attn_ref, ga_ref, o_ref, gb_ref, gt_ref, wa_ref, wb_ref, wo_ref, gpost_ref, out_ref):
    ya = _dot(attn_ref[0] * ga_ref[0], wa_ref[...])
    yb = _dot(o_ref[0] * gb_ref[0], wb_ref[...])
    m = gt_ref[0, :, :D_MODEL].astype(F32) * ya + gt_ref[0, :, D_MODEL:].astype(F32) * yb
    y = _dot(m.astype(BF16), wo_ref[...])
    out_ref[0] = x_ref[0] + _rms(y) * gpost_ref[...]


def _out_call(x, attn, ga, o, gb, gt, wa, wb, wo, gpost):
    b, s, _ = x.shape
    tm = min(TM_OUT, s)
    tok = lambda bi, si: (bi, si, 0)
    const = lambda bi, si: (0, 0)

    def tspec(a):
        return pl.BlockSpec((1, tm, a.shape[-1]), tok)

    def full(a):
        return pl.BlockSpec(a.shape, const)

    return pl.pallas_call(
        _out_body,
        grid=(b, s // tm),
        in_specs=[tspec(x), tspec(attn), tspec(ga), tspec(o), tspec(gb), tspec(gt),
                  full(wa), full(wb), full(wo), full(gpost)],
        out_specs=tspec(x),
        out_shape=jax.ShapeDtypeStruct(x.shape, x.dtype),
        compiler_params=pltpu.CompilerParams(dimension_semantics=("parallel", "parallel"),
                                             vmem_limit_bytes=VMEM_LIMIT),
        name="merge_out",
    )(x, attn, ga, o, gb, gt, wa, wb, wo, gpost)


def _rope_tables(s):
    inv = ROPE_THETA ** (-jnp.arange(0, MLA_ROPE, 2, dtype=F32) / MLA_ROPE)
    ang = jnp.arange(s, dtype=F32)[:, None] * inv[None, :]
    cos, sin = jnp.cos(ang), jnp.sin(ang)
    pad = jnp.zeros((s, HEAD_PAD - MLA_QK), F32)
    ra = jnp.concatenate([jnp.ones((s, MLA_NOPE), F32), cos, cos, pad], axis=1)
    rs = jnp.concatenate([jnp.zeros((s, MLA_NOPE), F32), -sin, sin, pad], axis=1)
    return ra, rs


def _pad_weights(w_in, w_uq, w_ukv):
    kpe_lo = Q_LORA + KV_LORA
    kpe = jnp.pad(w_in[:, kpe_lo:kpe_lo + MLA_ROPE], ((0, 0), (MLA_NOPE, HEAD_PAD - MLA_QK)))
    win = jnp.concatenate([w_in[:, :kpe_lo], kpe, w_in[:, kpe_lo + MLA_ROPE:]], axis=1)
    wuq = jnp.pad(w_uq.reshape(Q_LORA, MLA_HEADS, MLA_QK), ((0, 0), (0, 0), (0, HEAD_PAD - MLA_QK)))
    wkv = w_ukv.reshape(KV_LORA, MLA_HEADS, MLA_NOPE + MLA_V)
    wk = jnp.pad(wkv[..., :MLA_NOPE], ((0, 0), (0, 0), (0, HEAD_PAD - MLA_NOPE)))
    wukv = jnp.concatenate([wk.reshape(KV_LORA, MLA_HEADS * HEAD_PAD),
                            wkv[..., MLA_NOPE:].reshape(KV_LORA, MLA_WIDTH)], axis=1)
    return win.astype(BF16), wuq.reshape(Q_LORA, MLA_HEADS * HEAD_PAD).astype(BF16), wukv.astype(BF16)


def kernel(x, g_pre, w_in, b_gate, g_q, w_uq, g_kv, w_ukv, lb_logits, g_hgrn, w_branch_a, w_branch_b, w_out,
           g_post):
    assert g_pre.shape[0] == 1, "single-layer block"
    s = x.shape[1]
    win, wuq, wukv = _pad_weights(w_in[0], w_uq[0], w_ukv[0])
    ra, rs = _rope_tables(s)
    lower_bound = jax.nn.softmax(lb_logits.astype(F32), axis=0)[0:1]
    ghg = jnp.tile(g_hgrn[0], HG_HEADS)[None, :]

    q, k, v, ga, hq, hf, hi, gb, gt = _proj_call(x, g_pre, win, g_q, wuq, g_kv, wukv, ra, rs, b_gate)
    attn = _attn_call(q, k, v)
    o = _hgrn_call(hq, hf, hi, lower_bound, ghg)
    return _out_call(x, attn, ga, o, gb, gt, w_branch_a[0].astype(BF16), w_branch_b[0].astype(BF16),
                     w_out[0].astype(BF16), g_post)
```

```python
import functools
import math

import jax
import jax.numpy as jnp
import numpy as np
from jax import lax
from jax.experimental import pallas as pl
from jax.experimental.pallas import tpu as pltpu

F32 = jnp.float32
BF16 = jnp.bfloat16

D_MODEL = 1024
CHUNK = 64
EPS = 1e-6

MLA_HEADS = 8
MLA_NOPE = 64
MLA_ROPE = 32
MLA_V = 64
MLA_QK = MLA_NOPE + MLA_ROPE
Q_LORA = 768
KV_LORA = 256
ROPE_THETA = 10000.0
MLA_WIDTH = MLA_HEADS * MLA_V

HG_HEADS = 8
HG_DK = 64
HG_DV = 64
HG_WIDTH = HG_HEADS * HG_DV

LANES = 128
HEAD_PAD = LANES
HEADS_PER_GROUP = LANES // MLA_V
N_GROUPS = MLA_HEADS // HEADS_PER_GROUP

COL_CQ = 0
COL_CKV = COL_CQ + Q_LORA
COL_KPE = COL_CKV + KV_LORA
COL_GA = COL_KPE + HEAD_PAD
COL_HQ = COL_GA + MLA_WIDTH
COL_HF = COL_HQ + HG_WIDTH
COL_HI = COL_HF + HG_WIDTH
COL_GB = COL_HI + HG_WIDTH
COL_MG = COL_GB + HG_WIDTH
D_IN_PAD = COL_MG + 2 * D_MODEL

VMEM_LIMIT = 56 * 1024 * 1024

TM_PROJ = 512
TQ_ATTN = 512
TC_HGRN = 256
C_HGRN = 64
TM_OUT = 512


def _sigmoid(z):
    return 1.0 / (1.0 + jnp.exp(-z))


def _rms(t):
    return t * lax.rsqrt(jnp.mean(t * t, axis=-1, keepdims=True) + EPS)


def _dot(a, b):
    return jnp.dot(a, b, preferred_element_type=F32)


def _dot_nt(a, b):
    return lax.dot_general(a, b, (((1,), (1,)), ((), ())), preferred_element_type=F32)


def _dot_split(a_bf16, b_f32, terms):
    acc = None
    rem = b_f32
    for _ in range(terms):
        piece = rem.astype(BF16)
        rem = rem - piece.astype(F32)
        part = _dot(a_bf16, piece)
        acc = part if acc is None else acc + part
    return acc


def _proj_body(x_ref, gpre_ref, win_ref, gq_ref, wuq_ref, gkv_ref, wukv_ref, ra_ref, rs_ref, bg_ref,
               q_ref, k_ref, v_ref, ga_ref, hq_ref, hf_ref, hi_ref, gb_ref, gt_ref):
    h = (_rms(x_ref[0]) * gpre_ref[...]).astype(BF16)

    def proj(lo, width):
        return _dot(h, win_ref[:, lo:lo + width])

    ra = ra_ref[...]
    rs = rs_ref[...]
    lane = lax.broadcasted_iota(jnp.int32, ra.shape, 1)
    takes_upper = lane < MLA_NOPE + MLA_ROPE // 2

    def rope(t):
        partner = jnp.where(takes_upper, pltpu.roll(t, LANES - MLA_ROPE // 2, 1), pltpu.roll(t, MLA_ROPE // 2, 1))
        return t * ra + partner * rs

    scale = math.log2(math.e) / math.sqrt(MLA_QK)
    cqn = (_rms(proj(COL_CQ, Q_LORA)) * gq_ref[...]).astype(BF16)
    qu = _dot(cqn, wuq_ref[...])
    for hh in range(MLA_HEADS):
        q_ref[0, hh] = (rope(qu[:, hh * HEAD_PAD:(hh + 1) * HEAD_PAD]) * scale).astype(BF16)

    ckvn = (_rms(proj(COL_CKV, KV_LORA)) * gkv_ref[...]).astype(BF16)
    kvu = _dot(ckvn, wukv_ref[...])
    kpe = rope(proj(COL_KPE, HEAD_PAD))
    for hh in range(MLA_HEADS):
        k_ref[0, hh] = (kvu[:, hh * HEAD_PAD:(hh + 1) * HEAD_PAD] + kpe).astype(BF16)
    v_ref[0] = kvu[:, MLA_HEADS * HEAD_PAD:].T.astype(BF16)

    ga = proj(COL_GA, MLA_WIDTH)
    ga_ref[0] = (ga * _sigmoid(ga)).astype(BF16)
    hq_ref[0] = proj(COL_HQ, HG_WIDTH).astype(BF16)
    hf_ref[0] = proj(COL_HF, HG_WIDTH)
    hi_ref[0] = proj(COL_HI, HG_WIDTH).astype(BF16)
    gb = proj(COL_GB, HG_WIDTH)
    gb_ref[0] = (gb * _sigmoid(gb)).astype(BF16)
    step = 512
    for c in range(0, 2 * D_MODEL, step):
        z = proj(COL_MG + c, step) + bg_ref[:, c:c + step]
        gt_ref[0, :, c:c + step] = _sigmoid(z).astype(BF16)


def _proj_call(x, gpre, win, gq, wuq, gkv, wukv, ra, rs, bg):
    b, s, _ = x.shape
    tm = min(TM_PROJ, s)
    const = lambda bi, si: (0, 0)
    tok = lambda bi, si: (bi, si, 0)

    def full(a):
        return pl.BlockSpec(a.shape, const, pipeline_mode=pl.Buffered(1))

    def tok_out(width, dtype):
        return jax.ShapeDtypeStruct((b, s, width), dtype), pl.BlockSpec((1, tm, width), tok)

    head_shape = jax.ShapeDtypeStruct((b, MLA_HEADS, s, HEAD_PAD), BF16)
    head_spec = pl.BlockSpec((1, MLA_HEADS, tm, HEAD_PAD), lambda bi, si: (bi, 0, si, 0))
    outs = [(head_shape, head_spec), (head_shape, head_spec),
            (jax.ShapeDtypeStruct((b, MLA_WIDTH, s), BF16),
             pl.BlockSpec((1, MLA_WIDTH, tm), lambda bi, si: (bi, 0, si))),
            tok_out(MLA_WIDTH, BF16),
            tok_out(HG_WIDTH, BF16), tok_out(HG_WIDTH, F32), tok_out(HG_WIDTH, BF16), tok_out(HG_WIDTH, BF16),
            tok_out(2 * D_MODEL, BF16)]
    rope_spec = pl.BlockSpec((tm, HEAD_PAD), lambda bi, si: (si, 0))
    return pl.pallas_call(
        _proj_body,
        grid=(b, s // tm),
        in_specs=[pl.BlockSpec((1, tm, D_MODEL), tok), full(gpre), full(win), full(gq), full(wuq), full(gkv),
                  full(wukv), rope_spec, rope_spec, full(bg)],
        out_specs=[o[1] for o in outs],
        out_shape=[o[0] for o in outs],
        compiler_params=pltpu.CompilerParams(dimension_semantics=("parallel", "parallel"),
                                             vmem_limit_bytes=VMEM_LIMIT),
        name="proj",
    )(x, gpre, win, gq, wuq, gkv, wukv, ra, rs, bg)


def _attn_body(q_ref, k_ref, vt_ref, o_ref):
    tq = q_ref.shape[2]
    qi = pl.program_id(2)
    key_chunk = lax.broadcasted_iota(jnp.int32, (tq, tq), 0) // CHUNK
    query_chunk = lax.broadcasted_iota(jnp.int32, (tq, tq), 1) // CHUNK
    visible = key_chunk <= query_chunk

    def block(j, carry, on_diagonal):
        off = pl.multiple_of(j * tq, tq)
        heads = range(HEADS_PER_GROUP)
        scores = [_dot_nt(k_ref[0, hh, pl.ds(off, tq), :], q_ref[0, hh]) for hh in heads]
        stats = []
        for hh in heads:
            m, l, _ = carry[hh]
            sc = jnp.where(visible, scores[hh], -jnp.inf) if on_diagonal else scores[hh]
            m_new = jnp.maximum(m, jnp.max(sc, axis=0, keepdims=True))
            alpha = jnp.exp2(m - m_new)
            p = jnp.exp2(sc - m_new)
            l = alpha * l + jnp.sum(p, axis=0, keepdims=True)
            stats.append((m_new, l, alpha, p.astype(BF16)))
        out = []
        for hh in heads:
            m_new, l, alpha, p = stats[hh]
            vt = vt_ref[0, hh * MLA_V:(hh + 1) * MLA_V, pl.ds(off, tq)]
            out.append((m_new, l, alpha * carry[hh][2] + _dot(vt, p)))
        return tuple(out)

    init = tuple((jnp.full((1, tq), -jnp.inf, F32), jnp.zeros((1, tq), F32), jnp.zeros((MLA_V, tq), F32))
                 for _ in range(HEADS_PER_GROUP))
    carry = lax.fori_loop(0, qi, functools.partial(block, on_diagonal=False), init)
    carry = block(qi, carry, True)
    out_t = jnp.concatenate([acc / l for _, l, acc in carry], axis=0)
    o_ref[0] = out_t.T.astype(BF16)


def _attn_call(q, k, vt):
    b, _, s, _ = q.shape
    tq = min(TQ_ATTN, s)
    return pl.pallas_call(
        _attn_body,
        grid=(b, N_GROUPS, s // tq),
        in_specs=[pl.BlockSpec((1, HEADS_PER_GROUP, tq, HEAD_PAD), lambda bi, g, qi: (bi, g, qi, 0)),
                  pl.BlockSpec((1, HEADS_PER_GROUP, s, HEAD_PAD), lambda bi, g, qi: (bi, g, 0, 0)),
                  pl.BlockSpec((1, LANES, s), lambda bi, g, qi: (bi, g, 0))],
        out_specs=pl.BlockSpec((1, tq, LANES), lambda bi, g, qi: (bi, qi, g)),
        out_shape=jax.ShapeDtypeStruct((b, s, MLA_WIDTH), BF16),
        compiler_params=pltpu.CompilerParams(dimension_semantics=("parallel", "parallel", "arbitrary"),
                                             vmem_limit_bytes=VMEM_LIMIT),
        name="attn",
    )(q, k, vt)


def _hgrn_body(hq_ref, hf_ref, hi_ref, lb_ref, ghg_ref, ltri_ref, hsum_ref, o_ref, st_ref, o_scr):
    tc = hq_ref.shape[1]
    c_len = C_HGRN
    half = c_len // 2

    @pl.when(pl.program_id(1) == 0)
    def _():
        st_ref[...] = jnp.zeros_like(st_ref)

    lb = lb_ref[...]
    f = lb + (1.0 - lb) * _sigmoid(hf_ref[0])
    k_in = 1.0 - f
    cum = _dot_split(ltri_ref[...], jnp.log(f), 3)
    hq = hq_ref[0].astype(F32)

    def per_chunk_row(r):
        return jnp.concatenate(
            [jnp.broadcast_to(cum[c * c_len + r:c * c_len + r + 1, :], (c_len, HG_WIDTH))
             for c in range(tc // c_len)], axis=0)

    mid = per_chunk_row(half - 1)
    tot = per_chunk_row(c_len - 1)
    q_mid = (hq * jnp.exp(cum - mid)).astype(BF16)
    k_mid = (k_in * jnp.exp(mid - cum)).astype(BF16)
    q_dec = (hq * jnp.exp(cum)).astype(BF16)
    k_end = (k_in * jnp.exp(tot - cum)).astype(BF16)
    decay = jnp.exp(tot)
    v = hi_ref[0]

    lane = lax.broadcasted_iota(jnp.int32, (c_len, LANES), 1)
    low = lane < HG_DK
    causal = (lax.broadcasted_iota(jnp.int32, (2 * c_len, c_len), 0) % c_len
              >= lax.broadcasted_iota(jnp.int32, (2 * c_len, c_len), 1))
    same_head = ((lax.broadcasted_iota(jnp.int32, (LANES, LANES), 0) < HG_DV)
                 == (lax.broadcasted_iota(jnp.int32, (LANES, LANES), 1) < HG_DK))

    for g in range(HG_HEADS // 2):
        cols = slice(g * LANES, (g + 1) * LANES)
        st = st_ref[g]
        for c in range(tc // c_len):
            rows = slice(c * c_len, (c + 1) * c_len)
            qm = q_mid[rows, cols]
            zero = jnp.zeros_like(qm)
            stacked = jnp.concatenate([jnp.where(low, qm, zero), jnp.where(low, zero, qm)], axis=0)
            a = _dot_nt(stacked, k_mid[rows, cols])
            a = jnp.where(causal, a, 0.0).astype(BF16)
            vv = v[rows, cols]
            av = _dot(a, vv)
            o_intra = jnp.where(low, av[:c_len], av[c_len:])
            o_inter = _dot_nt(q_dec[rows, cols], st.astype(BF16))
            o_scr[rows, cols] = o_intra + o_inter
            upd = _dot(vv.astype(F32).T.astype(BF16), k_end[rows, cols])
            st = st * decay[c * c_len:c * c_len + 1, cols] + jnp.where(same_head, upd, 0.0)
        st_ref[g] = st

    o = o_scr[...]
    msq = _dot_split_right(o * o, hsum_ref[...], 2) * (1.0 / HG_DV)
    o_ref[0] = (o * lax.rsqrt(msq + EPS) * ghg_ref[...]).astype(BF16)


def _dot_split_right(a_f32, b_bf16, terms):
    acc = None
    rem = a_f32
    for _ in range(terms):
        piece = rem.astype(BF16)
        rem = rem - piece.astype(F32)
        part = _dot(piece, b_bf16)
        acc = part if acc is None else acc + part
    return acc


def _hgrn_call(hq, hf, hi, lb, ghg):
    b, s, _ = hq.shape
    tc = min(TC_HGRN, s)
    r = np.arange(tc)
    ltri = ((r[:, None] // C_HGRN == r[None, :] // C_HGRN) & (r[:, None] >= r[None, :])).astype(np.float32)
    hcol = np.arange(HG_WIDTH) // HG_DV
    hsum = (hcol[:, None] == hcol[None, :]).astype(np.float32)
    tok = lambda bi, si: (bi, si, 0)
    const = lambda bi, si: (0, 0)
    spec = pl.BlockSpec((1, tc, HG_WIDTH), tok)
    return pl.pallas_call(
        _hgrn_body,
        grid=(b, s // tc),
        in_specs=[spec, spec, spec, pl.BlockSpec((1, HG_WIDTH), const), pl.BlockSpec((1, HG_WIDTH), const),
                  pl.BlockSpec((tc, tc), const), pl.BlockSpec((HG_WIDTH, HG_WIDTH), const)],
        out_specs=spec,
        out_shape=jax.ShapeDtypeStruct((b, s, HG_WIDTH), BF16),
        scratch_shapes=[pltpu.VMEM((HG_HEADS // 2, LANES, LANES), F32), pltpu.VMEM((tc, HG_WIDTH), F32)],
        compiler_params=pltpu.CompilerParams(dimension_semantics=("parallel", "arbitrary"),
                                             vmem_limit_bytes=VMEM_LIMIT),
        name="hgrn",
    )(hq, hf, hi, lb, ghg, jnp.asarray(ltri, BF16), jnp.asarray(hsum, BF16))


def _out_body(x_ref, attn_ref, ga_ref, o_ref, gb_ref, gt_ref, wa_ref, wb_ref, wo_ref, gpost_ref, out_ref):
    ya = _dot(attn_ref[0] * ga_ref[0], wa_ref[...])
    yb = _dot(o_ref[0] * gb_ref[0], wb_ref[...])
    m = gt_ref[0, :, :D_MODEL].astype(F32) * ya + gt_ref[0, :, D_MODEL:].astype(F32) * yb
    y = _dot(m.astype(BF16), wo_ref[...])
    out_ref[0] = x_ref[0] + _rms(y) * gpost_ref[...]


def _out_call(x, attn, ga, o, gb, gt, wa, wb, wo, gpost):
    b, s, _ = x.shape
    tm = min(TM_OUT, s)
    tok = lambda bi, si: (bi, si, 0)
    const = lambda bi, si: (0, 0)

    def tspec(a):
        return pl.BlockSpec((1, tm, a.shape[-1]), tok)

    def full(a):
        return pl.BlockSpec(a.shape, const)

    return pl.pallas_call(
        _out_body,
        grid=(b, s // tm),
        in_specs=[tspec(x), tspec(attn), tspec(ga), tspec(o), tspec(gb), tspec(gt),
                  full(wa), full(wb), full(wo), full(gpost)],
        out_specs=tspec(x),
        out_shape=jax.ShapeDtypeStruct(x.shape, x.dtype),
        compiler_params=pltpu.CompilerParams(dimension_semantics=("parallel", "parallel"),
                                             vmem_limit_bytes=VMEM_LIMIT),
        name="merge_out",
    )(x, attn, ga, o, gb, gt, wa, wb, wo, gpost)


def _rope_tables(s):
    inv = ROPE_THETA ** (-jnp.arange(0, MLA_ROPE, 2, dtype=F32) / MLA_ROPE)
    ang = jnp.arange(s, dtype=F32)[:, None] * inv[None, :]
    cos, sin = jnp.cos(ang), jnp.sin(ang)
    pad = jnp.zeros((s, HEAD_PAD - MLA_QK), F32)
    ra = jnp.concatenate([jnp.ones((s, MLA_NOPE), F32), cos, cos, pad], axis=1)
    rs = jnp.concatenate([jnp.zeros((s, MLA_NOPE), F32), -sin, sin, pad], axis=1)
    return ra, rs


def _pad_weights(w_in, w_uq, w_ukv):
    kpe_lo = Q_LORA + KV_LORA
    kpe = jnp.pad(w_in[:, kpe_lo:kpe_lo + MLA_ROPE], ((0, 0), (MLA_NOPE, HEAD_PAD - MLA_QK)))
    win = jnp.concatenate([w_in[:, :kpe_lo], kpe, w_in[:, kpe_lo + MLA_ROPE:]], axis=1)
    wuq = jnp.pad(w_uq.reshape(Q_LORA, MLA_HEADS, MLA_QK), ((0, 0), (0, 0), (0, HEAD_PAD - MLA_QK)))
    wkv = w_ukv.reshape(KV_LORA, MLA_HEADS, MLA_NOPE + MLA_V)
    wk = jnp.pad(wkv[..., :MLA_NOPE], ((0, 0), (0, 0), (0, HEAD_PAD - MLA_NOPE)))
    wukv = jnp.concatenate([wk.reshape(KV_LORA, MLA_HEADS * HEAD_PAD),
                            wkv[..., MLA_NOPE:].reshape(KV_LORA, MLA_WIDTH)], axis=1)
    return win.astype(BF16), wuq.reshape(Q_LORA, MLA_HEADS * HEAD_PAD).astype(BF16), wukv.astype(BF16)


def kernel(x, g_pre, w_in, b_gate, g_q, w_uq, g_kv, w_ukv, lb_logits, g_hgrn, w_branch_a, w_branch_b, w_out,
           g_post):
    assert g_pre.shape[0] == 1, "single-layer block"
    s = x.shape[1]
    win, wuq, wukv = _pad_weights(w_in[0], w_uq[0], w_ukv[0])
    ra, rs = _rope_tables(s)
    lower_bound = jax.nn.softmax(lb_logits.astype(F32), axis=0)[0:1]
    ghg = jnp.tile(g_hgrn[0], HG_HEADS)[None, :]

    q, k, vt, ga, hq, hf, hi, gb, gt = _proj_call(x, g_pre, win, g_q, wuq, g_kv, wukv, ra, rs, b_gate)
    attn = _attn_call(q, k, vt)
    o = _hgrn_call(hq, hf, hi, lower_bound, ghg)
    return _out_call(x, attn, ga, o, gb, gt, w_branch_a[0].astype(BF16), w_branch_b[0].astype(BF16),
                     w_out[0].astype(BF16), g_post)
```

```python
import functools
import math

import jax
import jax.numpy as jnp
import numpy as np
from jax import lax
from jax.experimental import pallas as pl
from jax.experimental.pallas import tpu as pltpu

F32 = jnp.float32
BF16 = jnp.bfloat16

D_MODEL = 1024
CHUNK = 64
EPS = 1e-6

MLA_HEADS = 8
MLA_NOPE = 64
MLA_ROPE = 32
MLA_V = 64
MLA_QK = MLA_NOPE + MLA_ROPE
Q_LORA = 768
KV_LORA = 256
ROPE_THETA = 10000.0
MLA_WIDTH = MLA_HEADS * MLA_V

HG_HEADS = 8
HG_DK = 64
HG_DV = 64
HG_WIDTH = HG_HEADS * HG_DV

LANES = 128
HEAD_PAD = LANES
HEADS_PER_GROUP = LANES // MLA_V
N_GROUPS = MLA_HEADS // HEADS_PER_GROUP

COL_CQ = 0
COL_CKV = COL_CQ + Q_LORA
COL_KPE = COL_CKV + KV_LORA
COL_GA = COL_KPE + HEAD_PAD
COL_HQ = COL_GA + MLA_WIDTH
COL_HF = COL_HQ + HG_WIDTH
COL_HI = COL_HF + HG_WIDTH
COL_GB = COL_HI + HG_WIDTH
COL_MG = COL_GB + HG_WIDTH
D_IN_PAD = COL_MG + 2 * D_MODEL

VMEM_LIMIT = 56 * 1024 * 1024

TM_PROJ = 512
TQ_ATTN = 512
SUM_ROWS = 16
TC_HGRN = 256
C_HGRN = 64
TM_OUT = 512


def _sigmoid(z):
    return 1.0 / (1.0 + jnp.exp(-z))


def _rms(t):
    return t * lax.rsqrt(jnp.mean(t * t, axis=-1, keepdims=True) + EPS)


def _dot(a, b):
    return jnp.dot(a, b, preferred_element_type=F32)


def _dot_nt(a, b):
    return lax.dot_general(a, b, (((1,), (1,)), ((), ())), preferred_element_type=F32)


def _dot_split(a_bf16, b_f32, terms):
    acc = None
    rem = b_f32
    for _ in range(terms):
        piece = rem.astype(BF16)
        rem = rem - piece.astype(F32)
        part = _dot(a_bf16, piece)
        acc = part if acc is None else acc + part
    return acc


def _proj_body(x_ref, gpre_ref, win_ref, gq_ref, wuq_ref, gkv_ref, wukv_ref, ra_ref, rs_ref, bg_ref,
               q_ref, k_ref, v_ref, ga_ref, hq_ref, hf_ref, hi_ref, gb_ref, gt_ref):
    h = (_rms(x_ref[0]) * gpre_ref[...]).astype(BF16)

    def proj(lo, width):
        return _dot(h, win_ref[:, lo:lo + width])

    ra = ra_ref[...]
    rs = rs_ref[...]
    lane = lax.broadcasted_iota(jnp.int32, ra.shape, 1)
    takes_upper = lane < MLA_NOPE + MLA_ROPE // 2

    def rope(t):
        partner = jnp.where(takes_upper, pltpu.roll(t, LANES - MLA_ROPE // 2, 1), pltpu.roll(t, MLA_ROPE // 2, 1))
        return t * ra + partner * rs

    scale = math.log2(math.e) / math.sqrt(MLA_QK)
    cqn = (_rms(proj(COL_CQ, Q_LORA)) * gq_ref[...]).astype(BF16)
    qu = _dot(cqn, wuq_ref[...])
    for hh in range(MLA_HEADS):
        q_ref[0, hh] = (rope(qu[:, hh * HEAD_PAD:(hh + 1) * HEAD_PAD]) * scale).astype(BF16)

    ckvn = (_rms(proj(COL_CKV, KV_LORA)) * gkv_ref[...]).astype(BF16)
    kvu = _dot(ckvn, wukv_ref[...])
    kpe = rope(proj(COL_KPE, HEAD_PAD))
    for hh in range(MLA_HEADS):
        k_ref[0, hh] = (kvu[:, hh * HEAD_PAD:(hh + 1) * HEAD_PAD] + kpe).astype(BF16)
    v_ref[0] = kvu[:, MLA_HEADS * HEAD_PAD:].T.astype(BF16)

    ga = proj(COL_GA, MLA_WIDTH)
    ga_ref[0] = (ga * _sigmoid(ga)).astype(BF16)
    hq_ref[0] = proj(COL_HQ, HG_WIDTH).astype(BF16)
    hf_ref[0] = proj(COL_HF, HG_WIDTH)
    hi_ref[0] = proj(COL_HI, HG_WIDTH).astype(BF16)
    gb = proj(COL_GB, HG_WIDTH)
    gb_ref[0] = (gb * _sigmoid(gb)).astype(BF16)
    step = 512
    for c in range(0, 2 * D_MODEL, step):
        z = proj(COL_MG + c, step) + bg_ref[:, c:c + step]
        gt_ref[0, :, c:c + step] = _sigmoid(z).astype(BF16)


def _proj_call(x, gpre, win, gq, wuq, gkv, wukv, ra, rs, bg):
    b, s, _ = x.shape
    tm = min(TM_PROJ, s)
    const = lambda bi, si: (0, 0)
    tok = lambda bi, si: (bi, si, 0)

    def full(a):
        return pl.BlockSpec(a.shape, const, pipeline_mode=pl.Buffered(1))

    def tok_out(width, dtype):
        return jax.ShapeDtypeStruct((b, s, width), dtype), pl.BlockSpec((1, tm, width), tok)

    head_shape = jax.ShapeDtypeStruct((b, MLA_HEADS, s, HEAD_PAD), BF16)
    head_spec = pl.BlockSpec((1, MLA_HEADS, tm, HEAD_PAD), lambda bi, si: (bi, 0, si, 0))
    outs = [(head_shape, head_spec), (head_shape, head_spec),
            (jax.ShapeDtypeStruct((b, MLA_WIDTH, s), BF16),
             pl.BlockSpec((1, MLA_WIDTH, tm), lambda bi, si: (bi, 0, si))),
            tok_out(MLA_WIDTH, BF16),
            tok_out(HG_WIDTH, BF16), tok_out(HG_WIDTH, F32), tok_out(HG_WIDTH, BF16), tok_out(HG_WIDTH, BF16),
            tok_out(2 * D_MODEL, BF16)]
    rope_spec = pl.BlockSpec((tm, HEAD_PAD), lambda bi, si: (si, 0))
    return pl.pallas_call(
        _proj_body,
        grid=(b, s // tm),
        in_specs=[pl.BlockSpec((1, tm, D_MODEL), tok), full(gpre), full(win), full(gq), full(wuq), full(gkv),
                  full(wukv), rope_spec, rope_spec, full(bg)],
        out_specs=[o[1] for o in outs],
        out_shape=[o[0] for o in outs],
        compiler_params=pltpu.CompilerParams(dimension_semantics=("parallel", "parallel"),
                                             vmem_limit_bytes=VMEM_LIMIT),
        name="proj",
    )(x, gpre, win, gq, wuq, gkv, wukv, ra, rs, bg)


def _attn_body(q_ref, k_ref, vt_ref, o_ref, s_ref):
    tq = q_ref.shape[2]
    qi = pl.program_id(2)
    key_chunk = lax.broadcasted_iota(jnp.int32, (tq, tq), 0) // CHUNK
    query_chunk = lax.broadcasted_iota(jnp.int32, (tq, tq), 1) // CHUNK
    visible = key_chunk <= query_chunk

    heads = range(HEADS_PER_GROUP)
    ones = jnp.ones((SUM_ROWS, tq), BF16)

    def score(slot, j, on_diagonal=False):
        off = pl.multiple_of(j * tq, tq)
        block_max = []
        for hh in heads:
            sc = _dot_nt(k_ref[0, hh, pl.ds(off, tq), :], q_ref[0, hh])
            if on_diagonal:
                sc = jnp.where(visible, sc, -jnp.inf)
            s_ref[slot, hh] = sc
            block_max.append(jnp.max(sc, axis=0, keepdims=True))
        return tuple(block_max)

    def absorb(slot, j, block_max, stats):
        off = pl.multiple_of(j * tq, tq)
        probs = []
        for hh in heads:
            m_new = jnp.maximum(stats[hh][0], block_max[hh])
            probs.append((m_new, jnp.exp2(stats[hh][0] - m_new), jnp.exp2(s_ref[slot, hh] - m_new).astype(BF16)))
        out = []
        for hh in heads:
            m_new, alpha, p = probs[hh]
            vt = vt_ref[0, hh * MLA_V:(hh + 1) * MLA_V, pl.ds(off, tq)]
            vt_ones = jnp.concatenate([vt, ones], axis=0)
            out.append((m_new, alpha * stats[hh][1] + _dot(vt_ones, p)))
        return tuple(out)

    def block_at(i):
        return jnp.where(i == 0, qi, i - 1)

    def two_positions(pair, carry):
        max0, stats = carry
        i = 2 * pair
        max1 = score(1, i)
        stats = absorb(0, block_at(i), max0, stats)
        max0 = score(0, i + 1)
        return max0, absorb(1, i, max1, stats)

    def one_left(carry):
        max0, stats = carry
        return absorb(0, block_at(qi), max0, stats)

    def two_left(carry):
        max0, stats = carry
        max1 = score(1, qi - 1)
        stats = absorb(0, block_at(qi - 1), max0, stats)
        return absorb(1, qi - 1, max1, stats)

    init = tuple((jnp.full((1, tq), -jnp.inf, F32), jnp.zeros((MLA_V + SUM_ROWS, tq), F32)) for _ in heads)
    carry = lax.fori_loop(0, qi // 2, two_positions, (score(0, qi, on_diagonal=True), init))
    stats = lax.cond(qi % 2 == 0, one_left, two_left, carry)
    out_t = jnp.concatenate([acc[:MLA_V] / acc[MLA_V:MLA_V + 1] for _, acc in stats], axis=0)
    o_ref[0] = out_t.T.astype(BF16)


def _attn_call(q, k, vt):
    b, _, s, _ = q.shape
    tq = min(TQ_ATTN, s)
    return pl.pallas_call(
        _attn_body,
        grid=(b, N_GROUPS, s // tq),
        in_specs=[pl.BlockSpec((1, HEADS_PER_GROUP, tq, HEAD_PAD), lambda bi, g, qi: (bi, g, qi, 0)),
                  pl.BlockSpec((1, HEADS_PER_GROUP, s, HEAD_PAD), lambda bi, g, qi: (bi, g, 0, 0)),
                  pl.BlockSpec((1, LANES, s), lambda bi, g, qi: (bi, g, 0))],
        out_specs=pl.BlockSpec((1, tq, LANES), lambda bi, g, qi: (bi, qi, g)),
        out_shape=jax.ShapeDtypeStruct((b, s, MLA_WIDTH), BF16),
        scratch_shapes=[pltpu.VMEM((2, HEADS_PER_GROUP, tq, tq), F32)],
        compiler_params=pltpu.CompilerParams(dimension_semantics=("parallel", "parallel", "arbitrary"),
                                             vmem_limit_bytes=VMEM_LIMIT),
        name="attn",
    )(q, k, vt)


def _hgrn_body(hq_ref, hf_ref, hi_ref, lb_ref, ghg_ref, ltri_ref, hsum_ref, o_ref, st_ref, o_scr):
    tc = hq_ref.shape[1]
    c_len = C_HGRN
    half = c_len // 2

    @pl.when(pl.program_id(1) == 0)
    def _():
        st_ref[...] = jnp.zeros_like(st_ref)

    lb = lb_ref[...]
    f = lb + (1.0 - lb) * _sigmoid(hf_ref[0])
    k_in = 1.0 - f
    cum = _dot_split(ltri_ref[...], jnp.log(f), 3)
    hq = hq_ref[0].astype(F32)

    def per_chunk_row(r):
        return jnp.concatenate(
            [jnp.broadcast_to(cum[c * c_len + r:c * c_len + r + 1, :], (c_len, HG_WIDTH))
             for c in range(tc // c_len)], axis=0)

    mid = per_chunk_row(half - 1)
    tot = per_chunk_row(c_len - 1)
    q_mid = (hq * jnp.exp(cum - mid)).astype(BF16)
    k_mid = (k_in * jnp.exp(mid - cum)).astype(BF16)
    q_dec = (hq * jnp.exp(cum)).astype(BF16)
    k_end = (k_in * jnp.exp(tot - cum)).astype(BF16)
    decay = jnp.exp(tot)
    v = hi_ref[0]

    lane = lax.broadcasted_iota(jnp.int32, (c_len, LANES), 1)
    low = lane < HG_DK
    causal = (lax.broadcasted_iota(jnp.int32, (2 * c_len, c_len), 0) % c_len
              >= lax.broadcasted_iota(jnp.int32, (2 * c_len, c_len), 1))
    same_head = ((lax.broadcasted_iota(jnp.int32, (LANES, LANES), 0) < HG_DV)
                 == (lax.broadcasted_iota(jnp.int32, (LANES, LANES), 1) < HG_DK))

    for g in range(HG_HEADS // 2):
        cols = slice(g * LANES, (g + 1) * LANES)
        st = st_ref[g]
        for c in range(tc // c_len):
            rows = slice(c * c_len, (c + 1) * c_len)
            qm = q_mid[rows, cols]
            zero = jnp.zeros_like(qm)
            stacked = jnp.concatenate([jnp.where(low, qm, zero), jnp.where(low, zero, qm)], axis=0)
            a = _dot_nt(stacked, k_mid[rows, cols])
            a = jnp.where(causal, a, 0.0).astype(BF16)
            vv = v[rows, cols]
            av = _dot(a, vv)
            o_intra = jnp.where(low, av[:c_len], av[c_len:])
            o_inter = _dot_nt(q_dec[rows, cols], st.astype(BF16))
            o_scr[rows, cols] = o_intra + o_inter
            upd = _dot(vv.astype(F32).T.astype(BF16), k_end[rows, cols])
            st = st * decay[c * c_len:c * c_len + 1, cols] + jnp.where(same_head, upd, 0.0)
        st_ref[g] = st

    o = o_scr[...]
    msq = _dot_split_right(o * o, hsum_ref[...], 2) * (1.0 / HG_DV)
    o_ref[0] = (o * lax.rsqrt(msq + EPS) * ghg_ref[...]).astype(BF16)


def _dot_split_right(a_f32, b_bf16, terms):
    acc = None
    rem = a_f32
    for _ in range(terms):
        piece = rem.astype(BF16)
        rem = rem - piece.astype(F32)
        part = _dot(piece, b_bf16)
        acc = part if acc is None else acc + part
    return acc


def _hgrn_call(hq, hf, hi, lb, ghg):
    b, s, _ = hq.shape
    tc = min(TC_HGRN, s)
    r = np.arange(tc)
    ltri = ((r[:, None] // C_HGRN == r[None, :] // C_HGRN) & (r[:, None] >= r[None, :])).astype(np.float32)
    hcol = np.arange(HG_WIDTH) // HG_DV
    hsum = (hcol[:, None] == hcol[None, :]).astype(np.float32)
    tok = lambda bi, si: (bi, si, 0)
    const = lambda bi, si: (0, 0)
    spec = pl.BlockSpec((1, tc, HG_WIDTH), tok)
    return pl.pallas_call(
        _hgrn_body,
        grid=(b, s // tc),
        in_specs=[spec, spec, spec, pl.BlockSpec((1, HG_WIDTH), const), pl.BlockSpec((1, HG_WIDTH), const),
                  pl.BlockSpec((tc, tc), const), pl.BlockSpec((HG_WIDTH, HG_WIDTH), const)],
        out_specs=spec,
        out_shape=jax.ShapeDtypeStruct((b, s, HG_WIDTH), BF16),
        scratch_shapes=[pltpu.VMEM((HG_HEADS // 2, LANES, LANES), F32), pltpu.VMEM((tc, HG_WIDTH), F32)],
        compiler_params=pltpu.CompilerParams(dimension_semantics=("parallel", "arbitrary"),
                                             vmem_limit_bytes=VMEM_LIMIT),
        name="hgrn",
    )(hq, hf, hi, lb, ghg, jnp.asarray(ltri, BF16), jnp.asarray(hsum, BF16))


def _out_body(x_ref, attn_ref, ga_ref, o_ref, gb_ref, gt_ref, wa_ref, wb_ref, wo_ref, gpost_ref, out_ref):
    ya = _dot(attn_ref[0] * ga_ref[0], wa_ref[...])
    yb = _dot(o_ref[0] * gb_ref[0], wb_ref[...])
    m = gt_ref[0, :, :D_MODEL].astype(F32) * ya + gt_ref[0, :, D_MODEL:].astype(F32) * yb
    y = _dot(m.astype(BF16), wo_ref[...])
    out_ref[0] = x_ref[0] + _rms(y) * gpost_ref[...]


def _out_call(x, attn, ga, o, gb, gt, wa, wb, wo, gpost):
    b, s, _ = x.shape
    tm = min(TM_OUT, s)
    tok = lambda bi, si: (bi, si, 0)
    const = lambda bi, si: (0, 0)

    def tspec(a):
        return pl.BlockSpec((1, tm, a.shape[-1]), tok)

    def full(a):
        return pl.BlockSpec(a.shape, const)

    return pl.pallas_call(
        _out_body,
        grid=(b, s // tm),
        in_specs=[tspec(x), tspec(attn), tspec(ga), tspec(o), tspec(gb), tspec(gt),
                  full(wa), full(wb), full(wo), full(gpost)],
        out_specs=tspec(x),
        out_shape=jax.ShapeDtypeStruct(x.shape, x.dtype),
        compiler_params=pltpu.CompilerParams(dimension_semantics=("parallel", "parallel"),
                                             vmem_limit_bytes=VMEM_LIMIT),
        name="merge_out",
    )(x, attn, ga, o, gb, gt, wa, wb, wo, gpost)


def _rope_tables(s):
    inv = ROPE_THETA ** (-jnp.arange(0, MLA_ROPE, 2, dtype=F32) / MLA_ROPE)
    ang = jnp.arange(s, dtype=F32)[:, None] * inv[None, :]
    cos, sin = jnp.cos(ang), jnp.sin(ang)
    pad = jnp.zeros((s, HEAD_PAD - MLA_QK), F32)
    ra = jnp.concatenate([jnp.ones((s, MLA_NOPE), F32), cos, cos, pad], axis=1)
    rs = jnp.concatenate([jnp.zeros((s, MLA_NOPE), F32), -sin, sin, pad], axis=1)
    return ra, rs


def _pad_weights(w_in, w_uq, w_ukv):
    kpe_lo = Q_LORA + KV_LORA
    kpe = jnp.pad(w_in[:, kpe_lo:kpe_lo + MLA_ROPE], ((0, 0), (MLA_NOPE, HEAD_PAD - MLA_QK)))
    win = jnp.concatenate([w_in[:, :kpe_lo], kpe, w_in[:, kpe_lo + MLA_ROPE:]], axis=1)
    wuq = jnp.pad(w_uq.reshape(Q_LORA, MLA_HEADS, MLA_QK), ((0, 0), (0, 0), (0, HEAD_PAD - MLA_QK)))
    wkv = w_ukv.reshape(KV_LORA, MLA_HEADS, MLA_NOPE + MLA_V)
    wk = jnp.pad(wkv[..., :MLA_NOPE], ((0, 0), (0, 0), (0, HEAD_PAD - MLA_NOPE)))
    wukv = jnp.concatenate([wk.reshape(KV_LORA, MLA_HEADS * HEAD_PAD),
                            wkv[..., MLA_NOPE:].reshape(KV_LORA, MLA_WIDTH)], axis=1)
    return win.astype(BF16), wuq.reshape(Q_LORA, MLA_HEADS * HEAD_PAD).astype(BF16), wukv.astype(BF16)


def kernel(x, g_pre, w_in, b_gate, g_q, w_uq, g_kv, w_ukv, lb_logits, g_hgrn, w_branch_a, w_branch_b, w_out,
           g_post):
    assert g_pre.shape[0] == 1, "single-layer block"
    s = x.shape[1]
    win, wuq, wukv = _pad_weights(w_in[0], w_uq[0], w_ukv[0])
    ra, rs = _rope_tables(s)
    lower_bound = jax.nn.softmax(lb_logits.astype(F32), axis=0)[0:1]
    ghg = jnp.tile(g_hgrn[0], HG_HEADS)[None, :]

    q, k, vt, ga, hq, hf, hi, gb, gt = _proj_call(x, g_pre, win, g_q, wuq, g_kv, wukv, ra, rs, b_gate)
    attn = _attn_call(q, k, vt)
    o = _hgrn_call(hq, hf, hi, lower_bound, ghg)
    return _out_call(x, attn, ga, o, gb, gt, w_branch_a[0].astype(BF16), w_branch_b[0].astype(BF16),
                     w_out[0].astype(BF16), g_post)
```

```python
import functools
import math

import jax
import jax.numpy as jnp
import numpy as np
from jax import lax
from jax.experimental import pallas as pl
from jax.experimental.pallas import tpu as pltpu

F32 = jnp.float32
BF16 = jnp.bfloat16

D_MODEL = 1024
CHUNK = 64
EPS = 1e-6

MLA_HEADS = 8
MLA_NOPE = 64
MLA_ROPE = 32
MLA_V = 64
MLA_QK = MLA_NOPE + MLA_ROPE
Q_LORA = 768
KV_LORA = 256
ROPE_THETA = 10000.0
MLA_WIDTH = MLA_HEADS * MLA_V

HG_HEADS = 8
HG_DK = 64
HG_DV = 64
HG_WIDTH = HG_HEADS * HG_DV

LANES = 128
HEAD_PAD = LANES
HEADS_PER_GROUP = LANES // MLA_V
N_GROUPS = MLA_HEADS // HEADS_PER_GROUP

COL_CQ = 0
COL_CKV = COL_CQ + Q_LORA
COL_KPE = COL_CKV + KV_LORA
COL_GA = COL_KPE + HEAD_PAD
COL_HQ = COL_GA + MLA_WIDTH
COL_HF = COL_HQ + HG_WIDTH
COL_HI = COL_HF + HG_WIDTH
COL_GB = COL_HI + HG_WIDTH
COL_MG = COL_GB + HG_WIDTH
D_IN_PAD = COL_MG + 2 * D_MODEL

VMEM_LIMIT = 56 * 1024 * 1024

TM_PROJ = 512
TQ_ATTN = 512
SUM_ROWS = 16
TC_HGRN = 512
C_HGRN = 64
TM_OUT = 512


def _sigmoid(z):
    return 1.0 / (1.0 + jnp.exp(-z))


def _rms(t):
    return t * lax.rsqrt(jnp.mean(t * t, axis=-1, keepdims=True) + EPS)


def _dot(a, b):
    return jnp.dot(a, b, preferred_element_type=F32)


def _dot_nt(a, b):
    return lax.dot_general(a, b, (((1,), (1,)), ((), ())), preferred_element_type=F32)


def _dot_split(a_bf16, b_f32, terms):
    acc = None
    rem = b_f32
    for _ in range(terms):
        piece = rem.astype(BF16)
        rem = rem - piece.astype(F32)
        part = _dot(a_bf16, piece)
        acc = part if acc is None else acc + part
    return acc


def _proj_body(x_ref, gpre_ref, win_ref, gq_ref, wuq_ref, gkv_ref, wukv_ref, ra_ref, rs_ref, bg_ref,
               q_ref, k_ref, v_ref, ga_ref, hq_ref, hf_ref, hi_ref, gb_ref, gt_ref):
    h = (_rms(x_ref[0]) * gpre_ref[...]).astype(BF16)

    def proj(lo, width):
        return _dot(h, win_ref[:, lo:lo + width])

    ra = ra_ref[...]
    rs = rs_ref[...]
    lane = lax.broadcasted_iota(jnp.int32, ra.shape, 1)
    takes_upper = lane < MLA_NOPE + MLA_ROPE // 2

    def rope(t):
        partner = jnp.where(takes_upper, pltpu.roll(t, LANES - MLA_ROPE // 2, 1), pltpu.roll(t, MLA_ROPE // 2, 1))
        return t * ra + partner * rs

    scale = math.log2(math.e) / math.sqrt(MLA_QK)
    cqn = (_rms(proj(COL_CQ, Q_LORA)) * gq_ref[...]).astype(BF16)
    qu = _dot(cqn, wuq_ref[...])
    for hh in range(MLA_HEADS):
        q_ref[0, hh] = (rope(qu[:, hh * HEAD_PAD:(hh + 1) * HEAD_PAD]) * scale).astype(BF16)

    ckvn = (_rms(proj(COL_CKV, KV_LORA)) * gkv_ref[...]).astype(BF16)
    kvu = _dot(ckvn, wukv_ref[...])
    kpe = rope(proj(COL_KPE, HEAD_PAD))
    for hh in range(MLA_HEADS):
        k_ref[0, hh] = (kvu[:, hh * HEAD_PAD:(hh + 1) * HEAD_PAD] + kpe).astype(BF16)
    v_ref[0] = kvu[:, MLA_HEADS * HEAD_PAD:].T.astype(BF16)

    ga = proj(COL_GA, MLA_WIDTH)
    ga_ref[0] = (ga * _sigmoid(ga)).astype(BF16)
    hq_ref[0] = proj(COL_HQ, HG_WIDTH).astype(BF16)
    hf_ref[0] = proj(COL_HF, HG_WIDTH)
    hi_ref[0] = proj(COL_HI, HG_WIDTH).astype(BF16)
    gb = proj(COL_GB, HG_WIDTH)
    gb_ref[0] = (gb * _sigmoid(gb)).astype(BF16)
    step = 512
    for c in range(0, 2 * D_MODEL, step):
        z = proj(COL_MG + c, step) + bg_ref[:, c:c + step]
        gt_ref[0, :, c:c + step] = _sigmoid(z).astype(BF16)


def _proj_call(x, gpre, win, gq, wuq, gkv, wukv, ra, rs, bg):
    b, s, _ = x.shape
    tm = min(TM_PROJ, s)
    const = lambda bi, si: (0, 0)
    tok = lambda bi, si: (bi, si, 0)

    def full(a):
        return pl.BlockSpec(a.shape, const, pipeline_mode=pl.Buffered(1))

    def tok_out(width, dtype):
        return jax.ShapeDtypeStruct((b, s, width), dtype), pl.BlockSpec((1, tm, width), tok)

    head_shape = jax.ShapeDtypeStruct((b, MLA_HEADS, s, HEAD_PAD), BF16)
    head_spec = pl.BlockSpec((1, MLA_HEADS, tm, HEAD_PAD), lambda bi, si: (bi, 0, si, 0))
    outs = [(head_shape, head_spec), (head_shape, head_spec),
            (jax.ShapeDtypeStruct((b, MLA_WIDTH, s), BF16),
             pl.BlockSpec((1, MLA_WIDTH, tm), lambda bi, si: (bi, 0, si))),
            tok_out(MLA_WIDTH, BF16),
            tok_out(HG_WIDTH, BF16), tok_out(HG_WIDTH, F32), tok_out(HG_WIDTH, BF16), tok_out(HG_WIDTH, BF16),
            tok_out(2 * D_MODEL, BF16)]
    rope_spec = pl.BlockSpec((tm, HEAD_PAD), lambda bi, si: (si, 0))
    return pl.pallas_call(
        _proj_body,
        grid=(b, s // tm),
        in_specs=[pl.BlockSpec((1, tm, D_MODEL), tok), full(gpre), full(win), full(gq), full(wuq), full(gkv),
                  full(wukv), rope_spec, rope_spec, full(bg)],
        out_specs=[o[1] for o in outs],
        out_shape=[o[0] for o in outs],
        compiler_params=pltpu.CompilerParams(dimension_semantics=("parallel", "parallel"),
                                             vmem_limit_bytes=VMEM_LIMIT),
        name="proj",
    )(x, gpre, win, gq, wuq, gkv, wukv, ra, rs, bg)


def _attn_body(q_ref, k_ref, vt_ref, o_ref, s_ref):
    tq = q_ref.shape[2]
    qi = pl.program_id(2)
    key_chunk = lax.broadcasted_iota(jnp.int32, (tq, tq), 0) // CHUNK
    query_chunk = lax.broadcasted_iota(jnp.int32, (tq, tq), 1) // CHUNK
    visible = key_chunk <= query_chunk

    heads = range(HEADS_PER_GROUP)
    ones = jnp.ones((SUM_ROWS, tq), BF16)

    def score(slot, j, on_diagonal=False):
        off = pl.multiple_of(j * tq, tq)
        block_max = []
        for hh in heads:
            sc = _dot_nt(k_ref[0, hh, pl.ds(off, tq), :], q_ref[0, hh])
            if on_diagonal:
                sc = jnp.where(visible, sc, -jnp.inf)
            s_ref[slot, hh] = sc
            block_max.append(jnp.max(sc, axis=0, keepdims=True))
        return tuple(block_max)

    def absorb(slot, j, block_max, stats):
        off = pl.multiple_of(j * tq, tq)
        probs = []
        for hh in heads:
            m_new = jnp.maximum(stats[hh][0], block_max[hh])
            probs.append((m_new, jnp.exp2(stats[hh][0] - m_new), jnp.exp2(s_ref[slot, hh] - m_new).astype(BF16)))
        out = []
        for hh in heads:
            m_new, alpha, p = probs[hh]
            vt = vt_ref[0, hh * MLA_V:(hh + 1) * MLA_V, pl.ds(off, tq)]
            vt_ones = jnp.concatenate([vt, ones], axis=0)
            out.append((m_new, alpha * stats[hh][1] + _dot(vt_ones, p)))
        return tuple(out)

    def block_at(i):
        return jnp.where(i == 0, qi, i - 1)

    def two_positions(pair, carry):
        max0, stats = carry
        i = 2 * pair
        max1 = score(1, i)
        stats = absorb(0, block_at(i), max0, stats)
        max0 = score(0, i + 1)
        return max0, absorb(1, i, max1, stats)

    def one_left(carry):
        max0, stats = carry
        return absorb(0, block_at(qi), max0, stats)

    def two_left(carry):
        max0, stats = carry
        max1 = score(1, qi - 1)
        stats = absorb(0, block_at(qi - 1), max0, stats)
        return absorb(1, qi - 1, max1, stats)

    init = tuple((jnp.full((1, tq), -jnp.inf, F32), jnp.zeros((MLA_V + SUM_ROWS, tq), F32)) for _ in heads)
    carry = lax.fori_loop(0, qi // 2, two_positions, (score(0, qi, on_diagonal=True), init))
    stats = lax.cond(qi % 2 == 0, one_left, two_left, carry)
    out_t = jnp.concatenate([acc[:MLA_V] / acc[MLA_V:MLA_V + 1] for _, acc in stats], axis=0)
    o_ref[0] = out_t.T.astype(BF16)


def _attn_call(q, k, vt):
    b, _, s, _ = q.shape
    tq = min(TQ_ATTN, s)
    return pl.pallas_call(
        _attn_body,
        grid=(b, N_GROUPS, s // tq),
        in_specs=[pl.BlockSpec((1, HEADS_PER_GROUP, tq, HEAD_PAD), lambda bi, g, qi: (bi, g, qi, 0)),
                  pl.BlockSpec((1, HEADS_PER_GROUP, s, HEAD_PAD), lambda bi, g, qi: (bi, g, 0, 0)),
                  pl.BlockSpec((1, LANES, s), lambda bi, g, qi: (bi, g, 0))],
        out_specs=pl.BlockSpec((1, tq, LANES), lambda bi, g, qi: (bi, qi, g)),
        out_shape=jax.ShapeDtypeStruct((b, s, MLA_WIDTH), BF16),
        scratch_shapes=[pltpu.VMEM((2, HEADS_PER_GROUP, tq, tq), F32)],
        compiler_params=pltpu.CompilerParams(dimension_semantics=("parallel", "parallel", "arbitrary"),
                                             vmem_limit_bytes=VMEM_LIMIT),
        name="attn",
    )(q, k, vt)


def _hgrn_body(hq_ref, hf_ref, hi_ref, lb_ref, ltri_ref, o_ref, st_ref):
    tc = hq_ref.shape[1]
    c_len = C_HGRN
    n_chunks = tc // c_len
    groups = range(HG_HEADS // 2)

    @pl.when(pl.program_id(1) == 0)
    def _():
        st_ref[...] = jnp.zeros_like(st_ref)

    lb = lb_ref[...]
    ltri = ltri_ref[...]
    low = lax.broadcasted_iota(jnp.int32, (c_len, LANES), 1) < HG_DK
    causal = (lax.broadcasted_iota(jnp.int32, (2 * c_len, c_len), 0) % c_len
              >= lax.broadcasted_iota(jnp.int32, (2 * c_len, c_len), 1))
    same_head = ((lax.broadcasted_iota(jnp.int32, (LANES, LANES), 0) < HG_DV)
                 == (lax.broadcasted_iota(jnp.int32, (LANES, LANES), 1) < HG_DK))

    def stack_heads(t):
        zero = jnp.zeros_like(t)
        return jnp.concatenate([jnp.where(low, t, zero), jnp.where(low, zero, t)], axis=0)

    def first_stage(c):
        rows = slice(c * c_len, (c + 1) * c_len)
        f = lb + (1.0 - lb) * _sigmoid(hf_ref[0, rows, :])
        k_in = 1.0 - f
        cum = _dot_split(ltri, jnp.log2(f), 3)
        mid = cum[c_len // 2 - 1:c_len // 2, :]
        tot = cum[c_len - 1:c_len, :]
        q_mid = hq_ref[0, rows, :].astype(F32) * jnp.exp2(cum - mid)
        k_mid = k_in * jnp.exp2(mid - cum)
        q_dec = (q_mid * jnp.exp2(mid)).astype(BF16)
        k_end = (k_mid * jnp.exp2(tot - mid)).astype(BF16)
        q_mid = q_mid.astype(BF16)
        k_mid = k_mid.astype(BF16)
        v = hi_ref[0, rows, :]
        per_group = []
        for g in groups:
            cols = slice(g * LANES, (g + 1) * LANES)
            a = _dot_nt(stack_heads(q_mid[:, cols]), k_mid[:, cols])
            v_t = v[:, cols].astype(F32).T.astype(BF16)
            upd = _dot(v_t, k_end[:, cols])
            per_group.append((a, v_t, upd, stack_heads(q_dec[:, cols])))
        return per_group, jnp.exp2(tot)

    def second_stage(c, per_group, decay, states):
        rows = slice(c * c_len, (c + 1) * c_len)
        new_states = []
        for g in groups:
            cols = slice(g * LANES, (g + 1) * LANES)
            a, v_t, upd, q_dec = per_group[g]
            a = jnp.where(causal, a, 0.0).astype(BF16)
            lhs = jnp.concatenate([q_dec, a], axis=1)
            rhs_t = jnp.concatenate([states[g].astype(BF16), v_t], axis=1)
            res = _dot_nt(lhs, rhs_t)
            o_ref[0, rows, cols] = jnp.where(low, res[:c_len], res[c_len:]).astype(BF16)
            new_states.append(states[g] * decay[:, cols] + jnp.where(same_head, upd, 0.0))
        return new_states

    states = [st_ref[g] for g in groups]
    staged = first_stage(0)
    for c in range(n_chunks):
        nxt = first_stage(c + 1) if c + 1 < n_chunks else None
        states = second_stage(c, staged[0], staged[1], states)
        staged = nxt
    for g in groups:
        st_ref[g] = states[g]


def _hgrn_call(hq, hf, hi, lb):
    b, s, _ = hq.shape
    tc = min(TC_HGRN, s)
    r = np.arange(C_HGRN)
    ltri = (r[:, None] >= r[None, :]).astype(np.float32)
    tok = lambda bi, si: (bi, si, 0)
    const = lambda bi, si: (0, 0)
    spec = pl.BlockSpec((1, tc, HG_WIDTH), tok)
    return pl.pallas_call(
        _hgrn_body,
        grid=(b, s // tc),
        in_specs=[spec, spec, spec, pl.BlockSpec((1, HG_WIDTH), const), pl.BlockSpec((C_HGRN, C_HGRN), const)],
        out_specs=spec,
        out_shape=jax.ShapeDtypeStruct((b, s, HG_WIDTH), BF16),
        scratch_shapes=[pltpu.VMEM((HG_HEADS // 2, LANES, LANES), F32)],
        compiler_params=pltpu.CompilerParams(dimension_semantics=("parallel", "arbitrary"),
                                             vmem_limit_bytes=VMEM_LIMIT),
        name="hgrn",
    )(hq, hf, hi, lb, jnp.asarray(ltri, BF16))


def _out_body(x_ref, attn_ref, ga_ref, o_ref, gb_ref, gt_ref, wa_ref, wb_ref, wo_ref, gpost_ref, ghg_ref, hsum_ref,
              out_ref):
    ya = _dot(attn_ref[0] * ga_ref[0], wa_ref[...])
    o = o_ref[0].astype(F32)
    msq = _dot((o * o).astype(BF16), hsum_ref[...]) * (1.0 / HG_DV)
    o = o * lax.rsqrt(msq + EPS) * ghg_ref[...]
    yb = _dot((o * gb_ref[0].astype(F32)).astype(BF16), wb_ref[...])
    m = gt_ref[0, :, :D_MODEL].astype(F32) * ya + gt_ref[0, :, D_MODEL:].astype(F32) * yb
    y = _dot(m.astype(BF16), wo_ref[...])
    out_ref[0] = x_ref[0] + _rms(y) * gpost_ref[...]


def _out_call(x, attn, ga, o, gb, gt, wa, wb, wo, gpost, ghg):
    b, s, _ = x.shape
    hcol = np.arange(HG_WIDTH) // HG_DV
    hsum = jnp.asarray(hcol[:, None] == hcol[None, :], BF16)
    tm = min(TM_OUT, s)
    tok = lambda bi, si: (bi, si, 0)
    const = lambda bi, si: (0, 0)

    def tspec(a):
        return pl.BlockSpec((1, tm, a.shape[-1]), tok)

    def full(a):
        return pl.BlockSpec(a.shape, const)

    return pl.pallas_call(
        _out_body,
        grid=(b, s // tm),
        in_specs=[tspec(x), tspec(attn), tspec(ga), tspec(o), tspec(gb), tspec(gt),
                  full(wa), full(wb), full(wo), full(gpost), full(ghg), full(hsum)],
        out_specs=tspec(x),
        out_shape=jax.ShapeDtypeStruct(x.shape, x.dtype),
        compiler_params=pltpu.CompilerParams(dimension_semantics=("parallel", "parallel"),
                                             vmem_limit_bytes=VMEM_LIMIT),
        name="merge_out",
    )(x, attn, ga, o, gb, gt, wa, wb, wo, gpost, ghg, hsum)


def _rope_tables(s):
    inv = ROPE_THETA ** (-jnp.arange(0, MLA_ROPE, 2, dtype=F32) / MLA_ROPE)
    ang = jnp.arange(s, dtype=F32)[:, None] * inv[None, :]
    cos, sin = jnp.cos(ang), jnp.sin(ang)
    pad = jnp.zeros((s, HEAD_PAD - MLA_QK), F32)
    ra = jnp.concatenate([jnp.ones((s, MLA_NOPE), F32), cos, cos, pad], axis=1)
    rs = jnp.concatenate([jnp.zeros((s, MLA_NOPE), F32), -sin, sin, pad], axis=1)
    return ra, rs


def _pad_weights(w_in, w_uq, w_ukv):
    kpe_lo = Q_LORA + KV_LORA
    kpe = jnp.pad(w_in[:, kpe_lo:kpe_lo + MLA_ROPE], ((0, 0), (MLA_NOPE, HEAD_PAD - MLA_QK)))
    win = jnp.concatenate([w_in[:, :kpe_lo], kpe, w_in[:, kpe_lo + MLA_ROPE:]], axis=1)
    wuq = jnp.pad(w_uq.reshape(Q_LORA, MLA_HEADS, MLA_QK), ((0, 0), (0, 0), (0, HEAD_PAD - MLA_QK)))
    wkv = w_ukv.reshape(KV_LORA, MLA_HEADS, MLA_NOPE + MLA_V)
    wk = jnp.pad(wkv[..., :MLA_NOPE], ((0, 0), (0, 0), (0, HEAD_PAD - MLA_NOPE)))
    wukv = jnp.concatenate([wk.reshape(KV_LORA, MLA_HEADS * HEAD_PAD),
                            wkv[..., MLA_NOPE:].reshape(KV_LORA, MLA_WIDTH)], axis=1)
    return win.astype(BF16), wuq.reshape(Q_LORA, MLA_HEADS * HEAD_PAD).astype(BF16), wukv.astype(BF16)


def kernel(x, g_pre, w_in, b_gate, g_q, w_uq, g_kv, w_ukv, lb_logits, g_hgrn, w_branch_a, w_branch_b, w_out,
           g_post):
    assert g_pre.shape[0] == 1, "single-layer block"
    s = x.shape[1]
    win, wuq, wukv = _pad_weights(w_in[0], w_uq[0], w_ukv[0])
    ra, rs = _rope_tables(s)
    lower_bound = jax.nn.softmax(lb_logits.astype(F32), axis=0)[0:1]
    ghg = jnp.tile(g_hgrn[0], HG_HEADS)[None, :]

    q, k, vt, ga, hq, hf, hi, gb, gt = _proj_call(x, g_pre, win, g_q, wuq, g_kv, wukv, ra, rs, b_gate)
    attn = _attn_call(q, k, vt)
    o = _hgrn_call(hq, hf, hi, lower_bound)
    return _out_call(x, attn, ga, o, gb, gt, w_branch_a[0].astype(BF16), w_branch_b[0].astype(BF16),
                     w_out[0].astype(BF16), g_post, ghg)
```

```python
import functools
import math

import jax
import jax.numpy as jnp
import numpy as np
from jax import lax
from jax.experimental import pallas as pl
from jax.experimental.pallas import tpu as pltpu

F32 = jnp.float32
BF16 = jnp.bfloat16

D_MODEL = 1024
CHUNK = 64
EPS = 1e-6

MLA_HEADS = 8
MLA_NOPE = 64
MLA_ROPE = 32
MLA_V = 64
MLA_QK = MLA_NOPE + MLA_ROPE
Q_LORA = 768
KV_LORA = 256
ROPE_THETA = 10000.0
MLA_WIDTH = MLA_HEADS * MLA_V

HG_HEADS = 8
HG_DK = 64
HG_DV = 64
HG_WIDTH = HG_HEADS * HG_DV

LANES = 128
HEAD_PAD = LANES
HEADS_PER_GROUP = LANES // MLA_V
N_GROUPS = MLA_HEADS // HEADS_PER_GROUP

COL_CQ = 0
COL_CKV = COL_CQ + Q_LORA
COL_KPE = COL_CKV + KV_LORA
COL_GA = COL_KPE + HEAD_PAD
COL_HQ = COL_GA + MLA_WIDTH
COL_HF = COL_HQ + HG_WIDTH
COL_HI = COL_HF + HG_WIDTH
COL_GB = COL_HI + HG_WIDTH
COL_MG = COL_GB + HG_WIDTH
D_IN_PAD = COL_MG + 2 * D_MODEL

VMEM_LIMIT = 56 * 1024 * 1024

TM_PROJ = 512
TQ_ATTN = 512
SUM_ROWS = 16
TC_HGRN = 512
C_HGRN = 64
TM_OUT = 512


def _sigmoid(z):
    return 1.0 / (1.0 + jnp.exp(-z))


def _rms(t):
    return t * lax.rsqrt(jnp.mean(t * t, axis=-1, keepdims=True) + EPS)


def _dot(a, b):
    return jnp.dot(a, b, preferred_element_type=F32)


def _dot_nt(a, b):
    return lax.dot_general(a, b, (((1,), (1,)), ((), ())), preferred_element_type=F32)


def _dot_split(a_bf16, b_f32, terms):
    acc = None
    rem = b_f32
    for _ in range(terms):
        piece = rem.astype(BF16)
        rem = rem - piece.astype(F32)
        part = _dot(a_bf16, piece)
        acc = part if acc is None else acc + part
    return acc


def _proj_body(x_ref, gpre_ref, win_ref, gq_ref, wuq_ref, gkv_ref, wukv_ref, ra_ref, rs_ref, bg_ref,
               q_ref, k_ref, v_ref, ga_ref, hq_ref, hf_ref, hi_ref, gb_ref, gt_ref):
    h = (_rms(x_ref[0]) * gpre_ref[...]).astype(BF16)

    def proj(lo, width):
        return _dot(h, win_ref[:, lo:lo + width])

    ra = ra_ref[...]
    rs = rs_ref[...]
    lane = lax.broadcasted_iota(jnp.int32, ra.shape, 1)
    takes_upper = lane < MLA_NOPE + MLA_ROPE // 2

    def rope(t):
        partner = jnp.where(takes_upper, pltpu.roll(t, LANES - MLA_ROPE // 2, 1), pltpu.roll(t, MLA_ROPE // 2, 1))
        return t * ra + partner * rs

    scale = math.log2(math.e) / math.sqrt(MLA_QK)
    cqn = (_rms(proj(COL_CQ, Q_LORA)) * gq_ref[...]).astype(BF16)
    qu = _dot(cqn, wuq_ref[...])
    for hh in range(MLA_HEADS):
        q_ref[0, hh] = (rope(qu[:, hh * HEAD_PAD:(hh + 1) * HEAD_PAD]) * scale).astype(BF16)

    ckvn = (_rms(proj(COL_CKV, KV_LORA)) * gkv_ref[...]).astype(BF16)
    kvu = _dot(ckvn, wukv_ref[...])
    kpe = rope(proj(COL_KPE, HEAD_PAD))
    for hh in range(MLA_HEADS):
        k_ref[0, hh] = (kvu[:, hh * HEAD_PAD:(hh + 1) * HEAD_PAD] + kpe).astype(BF16)
    v_ref[0] = kvu[:, MLA_HEADS * HEAD_PAD:].T.astype(BF16)

    ga = proj(COL_GA, MLA_WIDTH)
    ga_ref[0] = (ga * _sigmoid(ga)).astype(BF16)
    hq_ref[0] = proj(COL_HQ, HG_WIDTH).astype(BF16)
    hf_ref[0] = proj(COL_HF, HG_WIDTH)
    hi_ref[0] = proj(COL_HI, HG_WIDTH).astype(BF16)
    gb = proj(COL_GB, HG_WIDTH)
    gb_ref[0] = (gb * _sigmoid(gb)).astype(BF16)
    step = 512
    for c in range(0, 2 * D_MODEL, step):
        z = proj(COL_MG + c, step) + bg_ref[:, c:c + step]
        gt_ref[0, :, c:c + step] = _sigmoid(z).astype(BF16)


def _proj_call(x, gpre, win, gq, wuq, gkv, wukv, ra, rs, bg):
    b, s, _ = x.shape
    tm = min(TM_PROJ, s)
    const = lambda bi, si: (0, 0)
    tok = lambda bi, si: (bi, si, 0)

    def full(a):
        return pl.BlockSpec(a.shape, const, pipeline_mode=pl.Buffered(1))

    def tok_out(width, dtype):
        return jax.ShapeDtypeStruct((b, s, width), dtype), pl.BlockSpec((1, tm, width), tok)

    head_shape = jax.ShapeDtypeStruct((b, MLA_HEADS, s, HEAD_PAD), BF16)
    head_spec = pl.BlockSpec((1, MLA_HEADS, tm, HEAD_PAD), lambda bi, si: (bi, 0, si, 0))
    outs = [(head_shape, head_spec), (head_shape, head_spec),
            (jax.ShapeDtypeStruct((b, MLA_WIDTH, s), BF16),
             pl.BlockSpec((1, MLA_WIDTH, tm), lambda bi, si: (bi, 0, si))),
            tok_out(MLA_WIDTH, BF16),
            tok_out(HG_WIDTH, BF16), tok_out(HG_WIDTH, F32), tok_out(HG_WIDTH, BF16), tok_out(HG_WIDTH, BF16),
            tok_out(2 * D_MODEL, BF16)]
    rope_spec = pl.BlockSpec((tm, HEAD_PAD), lambda bi, si: (si, 0))
    return pl.pallas_call(
        _proj_body,
        grid=(b, s // tm),
        in_specs=[pl.BlockSpec((1, tm, D_MODEL), tok), full(gpre), full(win), full(gq), full(wuq), full(gkv),
                  full(wukv), rope_spec, rope_spec, full(bg)],
        out_specs=[o[1] for o in outs],
        out_shape=[o[0] for o in outs],
        compiler_params=pltpu.CompilerParams(dimension_semantics=("parallel", "parallel"),
                                             vmem_limit_bytes=VMEM_LIMIT),
        name="proj",
    )(x, gpre, win, gq, wuq, gkv, wukv, ra, rs, bg)


def _attn_body(qtab_ref, ktab_ref, q_ref, k_ref, vt_ref, o_ref, s_ref, m_ref, acc_ref):
    tq = s_ref.shape[-1]
    n_q = q_ref.shape[2] // tq
    n_off = n_q * (n_q - 1) // 2
    heads = range(HEADS_PER_GROUP)
    key_chunk = lax.broadcasted_iota(jnp.int32, (tq, tq), 0) // CHUNK
    query_chunk = lax.broadcasted_iota(jnp.int32, (tq, tq), 1) // CHUNK
    visible = key_chunk <= query_chunk
    ones = jnp.ones((SUM_ROWS, tq), BF16)

    m_ref[...] = jnp.full(m_ref.shape, -jnp.inf, F32)
    acc_ref[...] = jnp.zeros(acc_ref.shape, F32)

    def score(slot, pos, on_diagonal):
        q_off = pl.multiple_of(qtab_ref[pos] * tq, tq)
        k_off = pl.multiple_of(ktab_ref[pos] * tq, tq)
        block_max = []
        for hh in heads:
            sc = _dot_nt(k_ref[0, hh, pl.ds(k_off, tq), :], q_ref[0, hh, pl.ds(q_off, tq), :])
            if on_diagonal:
                sc = jnp.where(visible, sc, -jnp.inf)
            s_ref[slot, hh] = sc
            block_max.append(jnp.max(sc, axis=0, keepdims=True))
        return tuple(block_max)

    def absorb(slot, pos, block_max):
        qi = qtab_ref[pos]
        k_off = pl.multiple_of(ktab_ref[pos] * tq, tq)
        probs = []
        for hh in heads:
            m_old = m_ref[qi, hh]
            m_new = jnp.maximum(m_old, block_max[hh])
            m_ref[qi, hh] = m_new
            probs.append((jnp.exp2(m_old - m_new), jnp.exp2(s_ref[slot, hh] - m_new).astype(BF16)))
        for hh in heads:
            alpha, p = probs[hh]
            vt = vt_ref[0, hh * MLA_V:(hh + 1) * MLA_V, pl.ds(k_off, tq)]
            vt_ones = jnp.concatenate([vt, ones], axis=0)
            acc_ref[qi, hh] = alpha * acc_ref[qi, hh] + _dot(vt_ones, p)

    def pipeline(first, count, on_diagonal):
        if count == 0:
            return
        n_loop = (count - 1) // 2

        def two_positions(t, max0):
            pos = first + 2 * t
            max1 = score(1, pos + 1, on_diagonal)
            absorb(0, pos, max0)
            max0 = score(0, pos + 2, on_diagonal)
            absorb(1, pos + 1, max1)
            return max0

        max0 = lax.fori_loop(0, n_loop, two_positions, score(0, first, on_diagonal))
        pos = first + 2 * n_loop
        if count - 2 * n_loop == 1:
            absorb(0, pos, max0)
        else:
            max1 = score(1, pos + 1, on_diagonal)
            absorb(0, pos, max0)
            absorb(1, pos + 1, max1)

    pipeline(0, n_q, True)
    pipeline(n_q, n_off, False)

    def finish(qi, carry):
        out_t = jnp.concatenate([acc_ref[qi, hh, :MLA_V, :] / acc_ref[qi, hh, MLA_V:MLA_V + 1, :] for hh in heads],
                                axis=0)
        o_ref[0, pl.ds(pl.multiple_of(qi * tq, tq), tq), :] = out_t.T.astype(BF16)
        return carry

    lax.fori_loop(0, n_q, finish, 0)


def _attn_call(q, k, vt):
    b, _, s, _ = q.shape
    tq = min(TQ_ATTN, s)
    n_q = s // tq
    off = [(qi, kj) for kj in range(n_q) for qi in range(kj + 1, n_q)]
    order = [(qi, qi) for qi in range(n_q)] + off
    qtab = jnp.asarray([p[0] for p in order], jnp.int32)
    ktab = jnp.asarray([p[1] for p in order], jnp.int32)
    grid_spec = pltpu.PrefetchScalarGridSpec(
        num_scalar_prefetch=2,
        grid=(b, N_GROUPS),
        in_specs=[pl.BlockSpec((1, HEADS_PER_GROUP, s, HEAD_PAD), lambda bi, g, qt, kt: (bi, g, 0, 0)),
                  pl.BlockSpec((1, HEADS_PER_GROUP, s, HEAD_PAD), lambda bi, g, qt, kt: (bi, g, 0, 0)),
                  pl.BlockSpec((1, LANES, s), lambda bi, g, qt, kt: (bi, g, 0))],
        out_specs=pl.BlockSpec((1, s, LANES), lambda bi, g, qt, kt: (bi, 0, g)),
        scratch_shapes=[pltpu.VMEM((2, HEADS_PER_GROUP, tq, tq), F32),
                        pltpu.VMEM((n_q, HEADS_PER_GROUP, 1, tq), F32),
                        pltpu.VMEM((n_q, HEADS_PER_GROUP, MLA_V + SUM_ROWS, tq), F32)])
    return pl.pallas_call(
        _attn_body,
        grid_spec=grid_spec,
        out_shape=jax.ShapeDtypeStruct((b, s, MLA_WIDTH), BF16),
        compiler_params=pltpu.CompilerParams(dimension_semantics=("parallel", "parallel"),
                                             vmem_limit_bytes=VMEM_LIMIT),
        name="attn",
    )(qtab, ktab, q, k, vt)


def _hgrn_body(hq_ref, hf_ref, hi_ref, lb_ref, ltri_ref, o_ref, st_ref):
    tc = hq_ref.shape[1]
    c_len = C_HGRN
    n_chunks = tc // c_len
    groups = range(HG_HEADS // 2)

    @pl.when(pl.program_id(1) == 0)
    def _():
        st_ref[...] = jnp.zeros_like(st_ref)

    lb = lb_ref[...]
    ltri = ltri_ref[...]
    low = lax.broadcasted_iota(jnp.int32, (c_len, LANES), 1) < HG_DK
    causal = (lax.broadcasted_iota(jnp.int32, (2 * c_len, c_len), 0) % c_len
              >= lax.broadcasted_iota(jnp.int32, (2 * c_len, c_len), 1))
    same_head = ((lax.broadcasted_iota(jnp.int32, (LANES, LANES), 0) < HG_DV)
                 == (lax.broadcasted_iota(jnp.int32, (LANES, LANES), 1) < HG_DK))

    def stack_heads(t):
        zero = jnp.zeros_like(t)
        return jnp.concatenate([jnp.where(low, t, zero), jnp.where(low, zero, t)], axis=0)

    def first_stage(c):
        rows = slice(c * c_len, (c + 1) * c_len)
        f = lb + (1.0 - lb) * _sigmoid(hf_ref[0, rows, :])
        k_in = 1.0 - f
        cum = _dot_split(ltri, jnp.log2(f), 3)
        mid = cum[c_len // 2 - 1:c_len // 2, :]
        tot = cum[c_len - 1:c_len, :]
        q_mid = hq_ref[0, rows, :].astype(F32) * jnp.exp2(cum - mid)
        k_mid = k_in * jnp.exp2(mid - cum)
        q_dec = (q_mid * jnp.exp2(mid)).astype(BF16)
        k_end = (k_mid * jnp.exp2(tot - mid)).astype(BF16)
        q_mid = q_mid.astype(BF16)
        k_mid = k_mid.astype(BF16)
        v = hi_ref[0, rows, :]
        per_group = []
        for g in groups:
            cols = slice(g * LANES, (g + 1) * LANES)
            a = _dot_nt(stack_heads(q_mid[:, cols]), k_mid[:, cols])
            v_t = v[:, cols].astype(F32).T.astype(BF16)
            upd = _dot(v_t, k_end[:, cols])
            per_group.append((a, v_t, upd, stack_heads(q_dec[:, cols])))
        return per_group, jnp.exp2(tot)

    def second_stage(c, per_group, decay, states):
        rows = slice(c * c_len, (c + 1) * c_len)
        new_states = []
        for g in groups:
            cols = slice(g * LANES, (g + 1) * LANES)
            a, v_t, upd, q_dec = per_group[g]
            a = jnp.where(causal, a, 0.0).astype(BF16)
            lhs = jnp.concatenate([q_dec, a], axis=1)
            rhs_t = jnp.concatenate([states[g].astype(BF16), v_t], axis=1)
            res = _dot_nt(lhs, rhs_t)
            o_ref[0, rows, cols] = jnp.where(low, res[:c_len], res[c_len:]).astype(BF16)
            new_states.append(states[g] * decay[:, cols] + jnp.where(same_head, upd, 0.0))
        return new_states

    states = [st_ref[g] for g in groups]
    staged = first_stage(0)
    for c in range(n_chunks):
        nxt = first_stage(c + 1) if c + 1 < n_chunks else None
        states = second_stage(c, staged[0], staged[1], states)
        staged = nxt
    for g in groups:
        st_ref[g] = states[g]


def _hgrn_call(hq, hf, hi, lb):
    b, s, _ = hq.shape
    tc = min(TC_HGRN, s)
    r = np.arange(C_HGRN)
    ltri = (r[:, None] >= r[None, :]).astype(np.float32)
    tok = lambda bi, si: (bi, si, 0)
    const = lambda bi, si: (0, 0)
    spec = pl.BlockSpec((1, tc, HG_WIDTH), tok)
    return pl.pallas_call(
        _hgrn_body,
        grid=(b, s // tc),
        in_specs=[spec, spec, spec, pl.BlockSpec((1, HG_WIDTH), const), pl.BlockSpec((C_HGRN, C_HGRN), const)],
        out_specs=spec,
        out_shape=jax.ShapeDtypeStruct((b, s, HG_WIDTH), BF16),
        scratch_shapes=[pltpu.VMEM((HG_HEADS // 2, LANES, LANES), F32)],
        compiler_params=pltpu.CompilerParams(dimension_semantics=("parallel", "arbitrary"),
                                             vmem_limit_bytes=VMEM_LIMIT),
        name="hgrn",
    )(hq, hf, hi, lb, jnp.asarray(ltri, BF16))


def _out_body(x_ref, attn_ref, ga_ref, o_ref, gb_ref, gt_ref, wa_ref, wb_ref, wo_ref, gpost_ref, ghg_ref, hsum_ref,
              out_ref):
    ya = _dot(attn_ref[0] * ga_ref[0], wa_ref[...])
    o = o_ref[0].astype(F32)
    msq = _dot((o * o).astype(BF16), hsum_ref[...]) * (1.0 / HG_DV)
    o = o * lax.rsqrt(msq + EPS) * ghg_ref[...]
    yb = _dot((o * gb_ref[0].astype(F32)).astype(BF16), wb_ref[...])
    m = gt_ref[0, :, :D_MODEL].astype(F32) * ya + gt_ref[0, :, D_MODEL:].astype(F32) * yb
    y = _dot(m.astype(BF16), wo_ref[...])
    out_ref[0] = x_ref[0] + _rms(y) * gpost_ref[...]


def _out_call(x, attn, ga, o, gb, gt, wa, wb, wo, gpost, ghg):
    b, s, _ = x.shape
    hcol = np.arange(HG_WIDTH) // HG_DV
    hsum = jnp.asarray(hcol[:, None] == hcol[None, :], BF16)
    tm = min(TM_OUT, s)
    tok = lambda bi, si: (bi, si, 0)
    const = lambda bi, si: (0, 0)

    def tspec(a):
        return pl.BlockSpec((1, tm, a.shape[-1]), tok)

    def full(a):
        return pl.BlockSpec(a.shape, const)

    return pl.pallas_call(
        _out_body,
        grid=(b, s // tm),
        in_specs=[tspec(x), tspec(attn), tspec(ga), tspec(o), tspec(gb), tspec(gt),
                  full(wa), full(wb), full(wo), full(gpost), full(ghg), full(hsum)],
        out_specs=tspec(x),
        out_shape=jax.ShapeDtypeStruct(x.shape, x.dtype),
        compiler_params=pltpu.CompilerParams(dimension_semantics=("parallel", "parallel"),
                                             vmem_limit_bytes=VMEM_LIMIT),
        name="merge_out",
    )(x, attn, ga, o, gb, gt, wa, wb, wo, gpost, ghg, hsum)


def _rope_tables(s):
    inv = ROPE_THETA ** (-jnp.arange(0, MLA_ROPE, 2, dtype=F32) / MLA_ROPE)
    ang = jnp.arange(s, dtype=F32)[:, None] * inv[None, :]
    cos, sin = jnp.cos(ang), jnp.sin(ang)
    pad = jnp.zeros((s, HEAD_PAD - MLA_QK), F32)
    ra = jnp.concatenate([jnp.ones((s, MLA_NOPE), F32), cos, cos, pad], axis=1)
    rs = jnp.concatenate([jnp.zeros((s, MLA_NOPE), F32), -sin, sin, pad], axis=1)
    return ra, rs


def _pad_weights(w_in, w_uq, w_ukv):
    kpe_lo = Q_LORA + KV_LORA
    kpe = jnp.pad(w_in[:, kpe_lo:kpe_lo + MLA_ROPE], ((0, 0), (MLA_NOPE, HEAD_PAD - MLA_QK)))
    win = jnp.concatenate([w_in[:, :kpe_lo], kpe, w_in[:, kpe_lo + MLA_ROPE:]], axis=1)
    wuq = jnp.pad(w_uq.reshape(Q_LORA, MLA_HEADS, MLA_QK), ((0, 0), (0, 0), (0, HEAD_PAD - MLA_QK)))
    wkv = w_ukv.reshape(KV_LORA, MLA_HEADS, MLA_NOPE + MLA_V)
    wk = jnp.pad(wkv[..., :MLA_NOPE], ((0, 0), (0, 0), (0, HEAD_PAD - MLA_NOPE)))
    wukv = jnp.concatenate([wk.reshape(KV_LORA, MLA_HEADS * HEAD_PAD),
                            wkv[..., MLA_NOPE:].reshape(KV_LORA, MLA_WIDTH)], axis=1)
    return win.astype(BF16), wuq.reshape(Q_LORA, MLA_HEADS * HEAD_PAD).astype(BF16), wukv.astype(BF16)


def kernel(x, g_pre, w_in, b_gate, g_q, w_uq, g_kv, w_ukv, lb_logits, g_hgrn, w_branch_a, w_branch_b, w_out,
           g_post):
    assert g_pre.shape[0] == 1, "single-layer block"
    s = x.shape[1]
    win, wuq, wukv = _pad_weights(w_in[0], w_uq[0], w_ukv[0])
    ra, rs = _rope_tables(s)
    lower_bound = jax.nn.softmax(lb_logits.astype(F32), axis=0)[0:1]
    ghg = jnp.tile(g_hgrn[0], HG_HEADS)[None, :]

    q, k, vt, ga, hq, hf, hi, gb, gt = _proj_call(x, g_pre, win, g_q, wuq, g_kv, wukv, ra, rs, b_gate)
    attn = _attn_call(q, k, vt)
    o = _hgrn_call(hq, hf, hi, lower_bound)
    return _out_call(x, attn, ga, o, gb, gt, w_branch_a[0].astype(BF16), w_branch_b[0].astype(BF16),
                     w_out[0].astype(BF16), g_post, ghg)
```

```python
import functools
import math

import jax
import jax.numpy as jnp
import numpy as np
from jax import lax
from jax.experimental import pallas as pl
from jax.experimental.pallas import tpu as pltpu

F32 = jnp.float32
BF16 = jnp.bfloat16

D_MODEL = 1024
CHUNK = 64
EPS = 1e-6

MLA_HEADS = 8
MLA_NOPE = 64
MLA_ROPE = 32
MLA_V = 64
MLA_QK = MLA_NOPE + MLA_ROPE
Q_LORA = 768
KV_LORA = 256
ROPE_THETA = 10000.0
MLA_WIDTH = MLA_HEADS * MLA_V

HG_HEADS = 8
HG_DK = 64
HG_DV = 64
HG_WIDTH = HG_HEADS * HG_DV

LANES = 128
HEAD_PAD = LANES
HEADS_PER_GROUP = LANES // MLA_V
N_GROUPS = MLA_HEADS // HEADS_PER_GROUP

COL_CQ = 0
COL_CKV = COL_CQ + Q_LORA
COL_KPE = COL_CKV + KV_LORA
COL_GA = COL_KPE + HEAD_PAD
COL_HQ = COL_GA + MLA_WIDTH
COL_HF = COL_HQ + HG_WIDTH
COL_HI = COL_HF + HG_WIDTH
COL_GB = COL_HI + HG_WIDTH
COL_MG = COL_GB + HG_WIDTH
D_IN_PAD = COL_MG + 2 * D_MODEL

VMEM_LIMIT = 56 * 1024 * 1024

TM_PROJ = 512
TQ_ATTN = 512
SUM_ROWS = 16
C_HGRN = 64
TM_OUT = 512


def _sigmoid(z):
    return 1.0 / (1.0 + jnp.exp(-z))


def _rms(t):
    return t * lax.rsqrt(jnp.mean(t * t, axis=-1, keepdims=True) + EPS)


def _dot(a, b):
    return jnp.dot(a, b, preferred_element_type=F32)


def _dot_nt(a, b):
    return lax.dot_general(a, b, (((1,), (1,)), ((), ())), preferred_element_type=F32)


def _dot_split(a_bf16, b_f32, terms):
    acc = None
    rem = b_f32
    for _ in range(terms):
        piece = rem.astype(BF16)
        rem = rem - piece.astype(F32)
        part = _dot(a_bf16, piece)
        acc = part if acc is None else acc + part
    return acc


def _recurrence_stages(proj, lb, ltri, o_ref):
    c_len = C_HGRN
    groups = range(HG_HEADS // 2)
    low = lax.broadcasted_iota(jnp.int32, (c_len, LANES), 1) < HG_DK
    causal = (lax.broadcasted_iota(jnp.int32, (2 * c_len, c_len), 0) % c_len
              >= lax.broadcasted_iota(jnp.int32, (2 * c_len, c_len), 1))
    same_head = ((lax.broadcasted_iota(jnp.int32, (LANES, LANES), 0) < HG_DV)
                 == (lax.broadcasted_iota(jnp.int32, (LANES, LANES), 1) < HG_DK))

    def stack_heads(t):
        zero = jnp.zeros_like(t)
        return jnp.concatenate([jnp.where(low, t, zero), jnp.where(low, zero, t)], axis=0)

    def gates(c):
        f = lb + (1.0 - lb) * _sigmoid(proj["hf"][c * c_len:(c + 1) * c_len])
        return 1.0 - f, _dot_split(ltri, jnp.log2(f), 3)

    def scores(c, gated):
        k_in, cum = gated
        rows = slice(c * c_len, (c + 1) * c_len)
        mid = cum[c_len // 2 - 1:c_len // 2, :]
        tot = cum[c_len - 1:c_len, :]
        q_mid = proj["hq"][rows] * jnp.exp2(cum - mid)
        k_mid = k_in * jnp.exp2(mid - cum)
        q_dec = (q_mid * jnp.exp2(mid)).astype(BF16)
        k_end = (k_mid * jnp.exp2(tot - mid)).astype(BF16)
        q_mid = q_mid.astype(BF16)
        k_mid = k_mid.astype(BF16)
        v = proj["hi"][rows]
        per_group = []
        for g in groups:
            cols = slice(g * LANES, (g + 1) * LANES)
            a = _dot_nt(stack_heads(q_mid[:, cols]), k_mid[:, cols])
            v_t = v[:, cols].T.astype(BF16)
            upd = _dot(v_t, k_end[:, cols])
            per_group.append((a, v_t, upd, stack_heads(q_dec[:, cols])))
        return per_group, jnp.exp2(tot)

    def output(c, scored, states):
        per_group, decay = scored
        rows = slice(c * c_len, (c + 1) * c_len)
        new_states = []
        for g in groups:
            cols = slice(g * LANES, (g + 1) * LANES)
            a, v_t, upd, q_dec = per_group[g]
            a = jnp.where(causal, a, 0.0).astype(BF16)
            lhs = jnp.concatenate([q_dec, a], axis=1)
            rhs_t = jnp.concatenate([states[g].astype(BF16), v_t], axis=1)
            res = _dot_nt(lhs, rhs_t)
            o_ref[0, rows, cols] = jnp.where(low, res[:c_len], res[c_len:]).astype(BF16)
            new_states.append(states[g] * decay[:, cols] + jnp.where(same_head, upd, 0.0))
        return new_states

    return gates, scores, output


def _proj_body(x_ref, gpre_ref, win_ref, gq_ref, wuq_ref, gkv_ref, wukv_ref, ra_ref, rs_ref, bg_ref, lb_ref, ltri_ref,
               q_ref, k_ref, v_ref, ga_ref, o_ref, gb_ref, gt_ref, st_ref):
    tm = x_ref.shape[1]
    groups = range(HG_HEADS // 2)

    @pl.when(pl.program_id(1) == 0)
    def _():
        st_ref[...] = jnp.zeros_like(st_ref)

    h = (_rms(x_ref[0]) * gpre_ref[...]).astype(BF16)

    def proj(lo, width):
        return _dot(h, win_ref[:, lo:lo + width])

    ra = ra_ref[...]
    rs = rs_ref[...]
    lane = lax.broadcasted_iota(jnp.int32, ra.shape, 1)
    takes_upper = lane < MLA_NOPE + MLA_ROPE // 2

    def rope(t):
        partner = jnp.where(takes_upper, pltpu.roll(t, LANES - MLA_ROPE // 2, 1), pltpu.roll(t, MLA_ROPE // 2, 1))
        return t * ra + partner * rs

    held = {}

    def q_down():
        held["cqn"] = (_rms(proj(COL_CQ, Q_LORA)) * gq_ref[...]).astype(BF16)

    def q_up():
        scale = math.log2(math.e) / math.sqrt(MLA_QK)
        qu = _dot(held["cqn"], wuq_ref[...])
        for hh in range(MLA_HEADS):
            q_ref[0, hh] = (rope(qu[:, hh * HEAD_PAD:(hh + 1) * HEAD_PAD]) * scale).astype(BF16)

    def kv_down():
        held["ckvn"] = (_rms(proj(COL_CKV, KV_LORA)) * gkv_ref[...]).astype(BF16)
        held["kpe"] = rope(proj(COL_KPE, HEAD_PAD))

    def kv_up():
        kvu = _dot(held["ckvn"], wukv_ref[...])
        for hh in range(MLA_HEADS):
            k_ref[0, hh] = (kvu[:, hh * HEAD_PAD:(hh + 1) * HEAD_PAD] + held["kpe"]).astype(BF16)
        v_ref[0] = kvu[:, MLA_HEADS * HEAD_PAD:].T.astype(BF16)

    def silu_gate(lo, out_ref):
        z = proj(lo, out_ref.shape[-1])
        out_ref[0] = (z * _sigmoid(z)).astype(BF16)

    def merge_gates(c, width=2 * LANES):
        z = proj(COL_MG + c * width, width) + bg_ref[:, c * width:(c + 1) * width]
        gt_ref[0, :, c * width:(c + 1) * width] = _sigmoid(z).astype(BF16)

    pieces = [functools.partial(merge_gates, c) for c in range(D_MODEL // LANES)]
    pieces[1:1] = [q_down]
    pieces[3:3] = [kv_down]
    pieces[5:5] = [q_up]
    pieces[7:7] = [kv_up]
    pieces += [functools.partial(silu_gate, COL_GA, ga_ref), functools.partial(silu_gate, COL_GB, gb_ref)]

    def next_piece():
        if pieces:
            pieces.pop(0)()

    rec = {}
    gates, scores, output = _recurrence_stages(rec, lb_ref[...], ltri_ref[...], o_ref)
    n_chunks = tm // C_HGRN
    rec["hf"] = proj(COL_HF, HG_WIDTH)
    gated = {0: gates(0)}
    rec["hq"] = proj(COL_HQ, HG_WIDTH)
    if n_chunks > 1:
        gated[1] = gates(1)
    rec["hi"] = proj(COL_HI, HG_WIDTH)
    scored = {0: scores(0, gated.pop(0))}
    states = [st_ref[g] for g in groups]
    for c in range(n_chunks):
        next_piece()
        if c + 2 < n_chunks:
            gated[c + 2] = gates(c + 2)
        next_piece()
        if c + 1 < n_chunks:
            scored[c + 1] = scores(c + 1, gated.pop(c + 1))
        states = output(c, scored.pop(c), states)
    for g in groups:
        st_ref[g] = states[g]
    while pieces:
        next_piece()


def _proj_call(x, gpre, win, gq, wuq, gkv, wukv, ra, rs, bg, lb):
    b, s, _ = x.shape
    tm = min(TM_PROJ, s)
    r = np.arange(C_HGRN)
    ltri = jnp.asarray(r[:, None] >= r[None, :], BF16)
    const = lambda bi, si: (0, 0)
    tok = lambda bi, si: (bi, si, 0)

    def full(a):
        return pl.BlockSpec(a.shape, const, pipeline_mode=pl.Buffered(1))

    def tok_out(width, dtype):
        return jax.ShapeDtypeStruct((b, s, width), dtype), pl.BlockSpec((1, tm, width), tok)

    head_shape = jax.ShapeDtypeStruct((b, MLA_HEADS, s, HEAD_PAD), BF16)
    head_spec = pl.BlockSpec((1, MLA_HEADS, tm, HEAD_PAD), lambda bi, si: (bi, 0, si, 0))
    outs = [(head_shape, head_spec), (head_shape, head_spec),
            (jax.ShapeDtypeStruct((b, MLA_WIDTH, s), BF16),
             pl.BlockSpec((1, MLA_WIDTH, tm), lambda bi, si: (bi, 0, si))),
            tok_out(MLA_WIDTH, BF16), tok_out(HG_WIDTH, BF16), tok_out(HG_WIDTH, BF16),
            tok_out(2 * D_MODEL, BF16)]
    rope_spec = pl.BlockSpec((tm, HEAD_PAD), lambda bi, si: (si, 0))
    return pl.pallas_call(
        _proj_body,
        grid=(b, s // tm),
        in_specs=[pl.BlockSpec((1, tm, D_MODEL), tok), full(gpre), full(win), full(gq), full(wuq), full(gkv),
                  full(wukv), rope_spec, rope_spec, full(bg), full(lb), full(ltri)],
        out_specs=[o[1] for o in outs],
        out_shape=[o[0] for o in outs],
        scratch_shapes=[pltpu.VMEM((HG_HEADS // 2, LANES, LANES), F32)],
        compiler_params=pltpu.CompilerParams(dimension_semantics=("parallel", "arbitrary"),
                                             vmem_limit_bytes=VMEM_LIMIT),
        name="proj_hgrn",
    )(x, gpre, win, gq, wuq, gkv, wukv, ra, rs, bg, lb, ltri)


def _attn_body(qtab_ref, ktab_ref, q_ref, k_ref, vt_ref, o_ref, s_ref, m_ref, acc_ref):
    tq = s_ref.shape[-1]
    n_q = q_ref.shape[2] // tq
    n_off = n_q * (n_q - 1) // 2
    heads = range(HEADS_PER_GROUP)
    key_chunk = lax.broadcasted_iota(jnp.int32, (tq, tq), 0) // CHUNK
    query_chunk = lax.broadcasted_iota(jnp.int32, (tq, tq), 1) // CHUNK
    visible = key_chunk <= query_chunk
    ones = jnp.ones((SUM_ROWS, tq), BF16)

    m_ref[...] = jnp.full(m_ref.shape, -jnp.inf, F32)
    acc_ref[...] = jnp.zeros(acc_ref.shape, F32)

    def score(slot, pos, on_diagonal):
        q_off = pl.multiple_of(qtab_ref[pos] * tq, tq)
        k_off = pl.multiple_of(ktab_ref[pos] * tq, tq)
        block_max = []
        for hh in heads:
            sc = _dot_nt(k_ref[0, hh, pl.ds(k_off, tq), :], q_ref[0, hh, pl.ds(q_off, tq), :])
            if on_diagonal:
                sc = jnp.where(visible, sc, -jnp.inf)
            s_ref[slot, hh] = sc
            block_max.append(jnp.max(sc, axis=0, keepdims=True))
        return tuple(block_max)

    def absorb(slot, pos, block_max):
        qi = qtab_ref[pos]
        k_off = pl.multiple_of(ktab_ref[pos] * tq, tq)
        probs = []
        for hh in heads:
            m_old = m_ref[qi, hh]
            m_new = jnp.maximum(m_old, block_max[hh])
            m_ref[qi, hh] = m_new
            probs.append((jnp.exp2(m_old - m_new), jnp.exp2(s_ref[slot, hh] - m_new).astype(BF16)))
        for hh in heads:
            alpha, p = probs[hh]
            vt = vt_ref[0, hh * MLA_V:(hh + 1) * MLA_V, pl.ds(k_off, tq)]
            vt_ones = jnp.concatenate([vt, ones], axis=0)
            acc_ref[qi, hh] = alpha * acc_ref[qi, hh] + _dot(vt_ones, p)

    def pipeline(first, count, on_diagonal):
        if count == 0:
            return
        n_loop = (count - 1) // 2

        def two_positions(t, max0):
            pos = first + 2 * t
            max1 = score(1, pos + 1, on_diagonal)
            absorb(0, pos, max0)
            max0 = score(0, pos + 2, on_diagonal)
            absorb(1, pos + 1, max1)
            return max0

        max0 = lax.fori_loop(0, n_loop, two_positions, score(0, first, on_diagonal))
        pos = first + 2 * n_loop
        if count - 2 * n_loop == 1:
            absorb(0, pos, max0)
        else:
            max1 = score(1, pos + 1, on_diagonal)
            absorb(0, pos, max0)
            absorb(1, pos + 1, max1)

    pipeline(0, n_q, True)
    pipeline(n_q, n_off, False)

    def finish(qi, carry):
        out_t = jnp.concatenate([acc_ref[qi, hh, :MLA_V, :] / acc_ref[qi, hh, MLA_V:MLA_V + 1, :] for hh in heads],
                                axis=0)
        o_ref[0, pl.ds(pl.multiple_of(qi * tq, tq), tq), :] = out_t.T.astype(BF16)
        return carry

    lax.fori_loop(0, n_q, finish, 0)


def _attn_call(q, k, vt):
    b, _, s, _ = q.shape
    tq = min(TQ_ATTN, s)
    n_q = s // tq
    off = [(qi, kj) for kj in range(n_q) for qi in range(kj + 1, n_q)]
    order = [(qi, qi) for qi in range(n_q)] + off
    qtab = jnp.asarray([p[0] for p in order], jnp.int32)
    ktab = jnp.asarray([p[1] for p in order], jnp.int32)
    grid_spec = pltpu.PrefetchScalarGridSpec(
        num_scalar_prefetch=2,
        grid=(b, N_GROUPS),
        in_specs=[pl.BlockSpec((1, HEADS_PER_GROUP, s, HEAD_PAD), lambda bi, g, qt, kt: (bi, g, 0, 0)),
                  pl.BlockSpec((1, HEADS_PER_GROUP, s, HEAD_PAD), lambda bi, g, qt, kt: (bi, g, 0, 0)),
                  pl.BlockSpec((1, LANES, s), lambda bi, g, qt, kt: (bi, g, 0))],
        out_specs=pl.BlockSpec((1, s, LANES), lambda bi, g, qt, kt: (bi, 0, g)),
        scratch_shapes=[pltpu.VMEM((2, HEADS_PER_GROUP, tq, tq), F32),
                        pltpu.VMEM((n_q, HEADS_PER_GROUP, 1, tq), F32),
                        pltpu.VMEM((n_q, HEADS_PER_GROUP, MLA_V + SUM_ROWS, tq), F32)])
    return pl.pallas_call(
        _attn_body,
        grid_spec=grid_spec,
        out_shape=jax.ShapeDtypeStruct((b, s, MLA_WIDTH), BF16),
        compiler_params=pltpu.CompilerParams(dimension_semantics=("parallel", "parallel"),
                                             vmem_limit_bytes=VMEM_LIMIT),
        name="attn",
    )(qtab, ktab, q, k, vt)


def _out_body(x_ref, attn_ref, ga_ref, o_ref, gb_ref, gt_ref, wa_ref, wb_ref, wo_ref, gpost_ref, ghg_ref, hsum_ref,
              out_ref):
    ya = _dot(attn_ref[0] * ga_ref[0], wa_ref[...])
    o = o_ref[0].astype(F32)
    msq = _dot((o * o).astype(BF16), hsum_ref[...]) * (1.0 / HG_DV)
    o = o * lax.rsqrt(msq + EPS) * ghg_ref[...]
    yb = _dot((o * gb_ref[0].astype(F32)).astype(BF16), wb_ref[...])
    m = gt_ref[0, :, :D_MODEL].astype(F32) * ya + gt_ref[0, :, D_MODEL:].astype(F32) * yb
    y = _dot(m.astype(BF16), wo_ref[...])
    out_ref[0] = x_ref[0] + _rms(y) * gpost_ref[...]


def _out_call(x, attn, ga, o, gb, gt, wa, wb, wo, gpost, ghg):
    b, s, _ = x.shape
    hcol = np.arange(HG_WIDTH) // HG_DV
    hsum = jnp.asarray(hcol[:, None] == hcol[None, :], BF16)
    tm = min(TM_OUT, s)
    tok = lambda bi, si: (bi, si, 0)
    const = lambda bi, si: (0, 0)

    def tspec(a):
        return pl.BlockSpec((1, tm, a.shape[-1]), tok)

    def full(a):
        return pl.BlockSpec(a.shape, const)

    return pl.pallas_call(
        _out_body,
        grid=(b, s // tm),
        in_specs=[tspec(x), tspec(attn), tspec(ga), tspec(o), tspec(gb), tspec(gt),
                  full(wa), full(wb), full(wo), full(gpost), full(ghg), full(hsum)],
        out_specs=tspec(x),
        out_shape=jax.ShapeDtypeStruct(x.shape, x.dtype),
        compiler_params=pltpu.CompilerParams(dimension_semantics=("parallel", "parallel"),
                                             vmem_limit_bytes=VMEM_LIMIT),
        name="merge_out",
    )(x, attn, ga, o, gb, gt, wa, wb, wo, gpost, ghg, hsum)


def _rope_tables(s):
    inv = ROPE_THETA ** (-jnp.arange(0, MLA_ROPE, 2, dtype=F32) / MLA_ROPE)
    ang = jnp.arange(s, dtype=F32)[:, None] * inv[None, :]
    cos, sin = jnp.cos(ang), jnp.sin(ang)
    pad = jnp.zeros((s, HEAD_PAD - MLA_QK), F32)
    ra = jnp.concatenate([jnp.ones((s, MLA_NOPE), F32), cos, cos, pad], axis=1)
    rs = jnp.concatenate([jnp.zeros((s, MLA_NOPE), F32), -sin, sin, pad], axis=1)
    return ra, rs


def _pad_weights(w_in, w_uq, w_ukv):
    kpe_lo = Q_LORA + KV_LORA
    kpe = jnp.pad(w_in[:, kpe_lo:kpe_lo + MLA_ROPE], ((0, 0), (MLA_NOPE, HEAD_PAD - MLA_QK)))
    win = jnp.concatenate([w_in[:, :kpe_lo], kpe, w_in[:, kpe_lo + MLA_ROPE:]], axis=1)
    wuq = jnp.pad(w_uq.reshape(Q_LORA, MLA_HEADS, MLA_QK), ((0, 0), (0, 0), (0, HEAD_PAD - MLA_QK)))
    wkv = w_ukv.reshape(KV_LORA, MLA_HEADS, MLA_NOPE + MLA_V)
    wk = jnp.pad(wkv[..., :MLA_NOPE], ((0, 0), (0, 0), (0, HEAD_PAD - MLA_NOPE)))
    wukv = jnp.concatenate([wk.reshape(KV_LORA, MLA_HEADS * HEAD_PAD),
                            wkv[..., MLA_NOPE:].reshape(KV_LORA, MLA_WIDTH)], axis=1)
    return win.astype(BF16), wuq.reshape(Q_LORA, MLA_HEADS * HEAD_PAD).astype(BF16), wukv.astype(BF16)


def kernel(x, g_pre, w_in, b_gate, g_q, w_uq, g_kv, w_ukv, lb_logits, g_hgrn, w_branch_a, w_branch_b, w_out,
           g_post):
    assert g_pre.shape[0] == 1, "single-layer block"
    s = x.shape[1]
    win, wuq, wukv = _pad_weights(w_in[0], w_uq[0], w_ukv[0])
    ra, rs = _rope_tables(s)
    lower_bound = jax.nn.softmax(lb_logits.astype(F32), axis=0)[0:1]
    ghg = jnp.tile(g_hgrn[0], HG_HEADS)[None, :]

    q, k, vt, ga, o, gb, gt = _proj_call(x, g_pre, win, g_q, wuq, g_kv, wukv, ra, rs, b_gate, lower_bound)
    attn = _attn_call(q, k, vt)
    return _out_call(x, attn, ga, o, gb, gt, w_branch_a[0].astype(BF16), w_branch_b[0].astype(BF16),
                     w_out[0].astype(BF16), g_post, ghg)
```

```python
import functools
import math

import jax
import jax.numpy as jnp
import numpy as np
from jax import lax
from jax.experimental import pallas as pl
from jax.experimental.pallas import tpu as pltpu

F32 = jnp.float32
BF16 = jnp.bfloat16

D_MODEL = 1024
CHUNK = 64
EPS = 1e-6

MLA_HEADS = 8
MLA_NOPE = 64
MLA_ROPE = 32
MLA_V = 64
MLA_QK = MLA_NOPE + MLA_ROPE
Q_LORA = 768
KV_LORA = 256
ROPE_THETA = 10000.0
MLA_WIDTH = MLA_HEADS * MLA_V

HG_HEADS = 8
HG_DK = 64
HG_DV = 64
HG_WIDTH = HG_HEADS * HG_DV

LANES = 128
HEAD_PAD = LANES
HEADS_PER_GROUP = LANES // MLA_V
N_GROUPS = MLA_HEADS // HEADS_PER_GROUP

COL_CQ = 0
COL_CKV = COL_CQ + Q_LORA
COL_KPE = COL_CKV + KV_LORA
COL_GA = COL_KPE + HEAD_PAD
COL_HQ = COL_GA + MLA_WIDTH
COL_HF = COL_HQ + HG_WIDTH
COL_HI = COL_HF + HG_WIDTH
COL_GB = COL_HI + HG_WIDTH
COL_MG = COL_GB + HG_WIDTH
D_IN_PAD = COL_MG + 2 * D_MODEL

VMEM_LIMIT = 56 * 1024 * 1024

TM_PROJ = 512
TQ_ATTN = 512
SUM_ROWS = 16
C_HGRN = 64
TM_OUT = 1024
OUT_PARTS = 2


def _sigmoid(z):
    return 1.0 / (1.0 + jnp.exp(-z))


def _rms(t):
    return t * lax.rsqrt(jnp.mean(t * t, axis=-1, keepdims=True) + EPS)


def _dot(a, b):
    return jnp.dot(a, b, preferred_element_type=F32)


def _dot_nt(a, b):
    return lax.dot_general(a, b, (((1,), (1,)), ((), ())), preferred_element_type=F32)


def _dot_split(a_bf16, b_f32, terms):
    acc = None
    rem = b_f32
    for _ in range(terms):
        piece = rem.astype(BF16)
        rem = rem - piece.astype(F32)
        part = _dot(a_bf16, piece)
        acc = part if acc is None else acc + part
    return acc


def _recurrence_stages(proj, lb, ltri, o_ref):
    c_len = C_HGRN
    groups = range(HG_HEADS // 2)
    low = lax.broadcasted_iota(jnp.int32, (c_len, LANES), 1) < HG_DK
    causal = (lax.broadcasted_iota(jnp.int32, (2 * c_len, c_len), 0) % c_len
              >= lax.broadcasted_iota(jnp.int32, (2 * c_len, c_len), 1))
    same_head = ((lax.broadcasted_iota(jnp.int32, (LANES, LANES), 0) < HG_DV)
                 == (lax.broadcasted_iota(jnp.int32, (LANES, LANES), 1) < HG_DK))

    def stack_heads(t):
        zero = jnp.zeros_like(t)
        return jnp.concatenate([jnp.where(low, t, zero), jnp.where(low, zero, t)], axis=0)

    def gates(c):
        f = lb + (1.0 - lb) * _sigmoid(proj["hf"][c * c_len:(c + 1) * c_len])
        return 1.0 - f, _dot_split(ltri, jnp.log2(f), 3)

    def scores(c, gated):
        k_in, cum = gated
        rows = slice(c * c_len, (c + 1) * c_len)
        mid = cum[c_len // 2 - 1:c_len // 2, :]
        tot = cum[c_len - 1:c_len, :]
        q_mid = proj["hq"][rows] * jnp.exp2(cum - mid)
        k_mid = k_in * jnp.exp2(mid - cum)
        q_dec = (q_mid * jnp.exp2(mid)).astype(BF16)
        k_end = (k_mid * jnp.exp2(tot - mid)).astype(BF16)
        q_mid = q_mid.astype(BF16)
        k_mid = k_mid.astype(BF16)
        v = proj["hi"][rows]
        per_group = []
        for g in groups:
            cols = slice(g * LANES, (g + 1) * LANES)
            a = _dot_nt(stack_heads(q_mid[:, cols]), k_mid[:, cols])
            v_t = v[:, cols].T.astype(BF16)
            upd = _dot(v_t, k_end[:, cols])
            per_group.append((a, v_t, upd, stack_heads(q_dec[:, cols])))
        return per_group, jnp.exp2(tot)

    def output(c, scored, states):
        per_group, decay = scored
        rows = slice(c * c_len, (c + 1) * c_len)
        new_states = []
        for g in groups:
            cols = slice(g * LANES, (g + 1) * LANES)
            a, v_t, upd, q_dec = per_group[g]
            a = jnp.where(causal, a, 0.0).astype(BF16)
            lhs = jnp.concatenate([q_dec, a], axis=1)
            rhs_t = jnp.concatenate([states[g].astype(BF16), v_t], axis=1)
            res = _dot_nt(lhs, rhs_t)
            o_ref[0, rows, cols] = jnp.where(low, res[:c_len], res[c_len:]).astype(BF16)
            new_states.append(states[g] * decay[:, cols] + jnp.where(same_head, upd, 0.0))
        return new_states

    return gates, scores, output


def _proj_body(x_ref, gpre_ref, wlat_ref, wkpe_ref, wwide_ref, gq_ref, wuq_ref, gkv_ref, wukv_ref, ra_ref, rs_ref,
               bg_ref, lb_ref, ltri_ref, q_ref, k_ref, v_ref, ga_ref, o_ref, gb_ref, gt_ref, st_ref):
    tm = x_ref.shape[1]
    groups = range(HG_HEADS // 2)

    @pl.when(pl.program_id(1) == 0)
    def _():
        st_ref[...] = jnp.zeros_like(st_ref)

    h = (_rms(x_ref[0]) * gpre_ref[...]).astype(BF16)

    def proj(lo, width):
        if lo + width <= COL_KPE:
            return _dot(h, wlat_ref[:, lo:lo + width])
        if lo == COL_KPE:
            return _dot(h, wkpe_ref[...])
        return _dot(h, wwide_ref[:, lo - COL_GA:lo - COL_GA + width])

    ra = ra_ref[...]
    rs = rs_ref[...]
    lane = lax.broadcasted_iota(jnp.int32, ra.shape, 1)
    takes_upper = lane < MLA_NOPE + MLA_ROPE // 2

    def rope(t):
        partner = jnp.where(takes_upper, pltpu.roll(t, LANES - MLA_ROPE // 2, 1), pltpu.roll(t, MLA_ROPE // 2, 1))
        return t * ra + partner * rs

    held = {}

    def q_down():
        held["cqn"] = (_rms(proj(COL_CQ, Q_LORA)) * gq_ref[...]).astype(BF16)

    def q_up():
        scale = math.log2(math.e) / math.sqrt(MLA_QK)
        qu = _dot(held["cqn"], wuq_ref[...])
        for hh in range(MLA_HEADS):
            q_ref[0, hh] = (rope(qu[:, hh * HEAD_PAD:(hh + 1) * HEAD_PAD]) * scale).astype(BF16)

    def kv_down():
        held["ckvn"] = (_rms(proj(COL_CKV, KV_LORA)) * gkv_ref[...]).astype(BF16)
        held["kpe"] = rope(proj(COL_KPE, HEAD_PAD))

    def kv_up():
        kvu = _dot(held["ckvn"], wukv_ref[...])
        for hh in range(MLA_HEADS):
            k_ref[0, hh] = (kvu[:, hh * HEAD_PAD:(hh + 1) * HEAD_PAD] + held["kpe"]).astype(BF16)
        v_ref[0] = kvu[:, MLA_HEADS * HEAD_PAD:].T.astype(BF16)

    def silu_gate(lo, out_ref):
        z = proj(lo, out_ref.shape[-1])
        out_ref[0] = (z * _sigmoid(z)).astype(BF16)

    def merge_gates(c, width=2 * LANES):
        z = proj(COL_MG + c * width, width) + bg_ref[:, c * width:(c + 1) * width]
        gt_ref[0, :, c * width:(c + 1) * width] = _sigmoid(z).astype(BF16)

    pieces = [functools.partial(merge_gates, c) for c in range(D_MODEL // LANES)]
    pieces[1:1] = [q_down]
    pieces[3:3] = [kv_down]
    pieces[5:5] = [q_up]
    pieces[7:7] = [kv_up]
    pieces += [functools.partial(silu_gate, COL_GA, ga_ref), functools.partial(silu_gate, COL_GB, gb_ref)]

    def next_piece():
        if pieces:
            pieces.pop(0)()

    rec = {}
    gates, scores, output = _recurrence_stages(rec, lb_ref[...], ltri_ref[...], o_ref)
    n_chunks = tm // C_HGRN
    rec["hf"] = proj(COL_HF, HG_WIDTH)
    gated = {0: gates(0)}
    rec["hq"] = proj(COL_HQ, HG_WIDTH)
    if n_chunks > 1:
        gated[1] = gates(1)
    rec["hi"] = proj(COL_HI, HG_WIDTH)
    scored = {0: scores(0, gated.pop(0))}
    states = [st_ref[g] for g in groups]
    for c in range(n_chunks):
        next_piece()
        if c + 2 < n_chunks:
            gated[c + 2] = gates(c + 2)
        next_piece()
        if c + 1 < n_chunks:
            scored[c + 1] = scores(c + 1, gated.pop(c + 1))
        states = output(c, scored.pop(c), states)
    for g in groups:
        st_ref[g] = states[g]
    while pieces:
        next_piece()


def _proj_call(x, gpre, win, gq, wuq, gkv, wukv, ra, rs, bg, lb):
    b, s, _ = x.shape
    tm = min(TM_PROJ, s)
    r = np.arange(C_HGRN)
    ltri = jnp.asarray(r[:, None] >= r[None, :], BF16)
    const = lambda bi, si: (0, 0)
    tok = lambda bi, si: (bi, si, 0)

    def full(a):
        return pl.BlockSpec(a.shape, const, pipeline_mode=pl.Buffered(1))

    def tok_out(width, dtype):
        return jax.ShapeDtypeStruct((b, s, width), dtype), pl.BlockSpec((1, tm, width), tok)

    head_shape = jax.ShapeDtypeStruct((b, MLA_HEADS, s, HEAD_PAD), BF16)
    head_spec = pl.BlockSpec((1, MLA_HEADS, tm, HEAD_PAD), lambda bi, si: (bi, 0, si, 0))
    outs = [(head_shape, head_spec), (head_shape, head_spec),
            (jax.ShapeDtypeStruct((b, MLA_WIDTH, s), BF16),
             pl.BlockSpec((1, MLA_WIDTH, tm), lambda bi, si: (bi, 0, si))),
            tok_out(MLA_WIDTH, BF16), tok_out(HG_WIDTH, BF16), tok_out(HG_WIDTH, BF16),
            tok_out(2 * D_MODEL, BF16)]
    rope_spec = pl.BlockSpec((tm, HEAD_PAD), lambda bi, si: (si, 0))
    return pl.pallas_call(
        _proj_body,
        grid=(b, s // tm),
        in_specs=[pl.BlockSpec((1, tm, D_MODEL), tok), full(gpre), *(full(w) for w in win), full(gq), full(wuq),
                  full(gkv), full(wukv), rope_spec, rope_spec, full(bg), full(lb), full(ltri)],
        out_specs=[o[1] for o in outs],
        out_shape=[o[0] for o in outs],
        scratch_shapes=[pltpu.VMEM((HG_HEADS // 2, LANES, LANES), F32)],
        compiler_params=pltpu.CompilerParams(dimension_semantics=("parallel", "arbitrary"),
                                             vmem_limit_bytes=VMEM_LIMIT),
        name="proj_hgrn",
    )(x, gpre, *win, gq, wuq, gkv, wukv, ra, rs, bg, lb, ltri)


def _attn_body(qtab_ref, ktab_ref, q_ref, k_ref, vt_ref, o_ref, s_ref, m_ref, acc_ref):
    tq = s_ref.shape[-1]
    n_q = q_ref.shape[2] // tq
    n_off = n_q * (n_q - 1) // 2
    heads = range(HEADS_PER_GROUP)
    key_chunk = lax.broadcasted_iota(jnp.int32, (tq, tq), 0) // CHUNK
    query_chunk = lax.broadcasted_iota(jnp.int32, (tq, tq), 1) // CHUNK
    visible = key_chunk <= query_chunk
    ones = jnp.ones((SUM_ROWS, tq), BF16)

    m_ref[...] = jnp.full(m_ref.shape, -jnp.inf, F32)
    acc_ref[...] = jnp.zeros(acc_ref.shape, F32)

    half = tq // 2
    assert half % CHUNK == 0 and half % LANES == 0
    early, late, everything = slice(0, half), slice(half, tq), slice(0, tq)

    def column_parts(on_diagonal):
        return [(early, early), (late, everything)] if on_diagonal else [(everything, everything)]

    def score(slot, pos, on_diagonal):
        q_off = qtab_ref[pos] * tq
        k_off = ktab_ref[pos] * tq
        block_max = []
        for hh in heads:
            part_max = []
            for queries, keys in column_parts(on_diagonal):
                k_rows = pl.ds(pl.multiple_of(k_off + keys.start, half), keys.stop - keys.start)
                q_rows = pl.ds(pl.multiple_of(q_off + queries.start, half), queries.stop - queries.start)
                sc = _dot_nt(k_ref[0, hh, k_rows, :], q_ref[0, hh, q_rows, :])
                if on_diagonal:
                    sc = jnp.where(visible[keys, queries], sc, -jnp.inf)
                s_ref[slot, hh, keys, queries] = sc
                part_max.append(jnp.max(sc, axis=0, keepdims=True))
            block_max.append(jnp.concatenate(part_max, axis=1))
        return tuple(block_max)

    def absorb(slot, pos, block_max, on_diagonal):
        qi = qtab_ref[pos]
        k_off = ktab_ref[pos] * tq
        probs = []
        for hh in heads:
            m_old = m_ref[qi, hh]
            m_new = jnp.maximum(m_old, block_max[hh])
            m_ref[qi, hh] = m_new
            p = [jnp.exp2(s_ref[slot, hh, keys, queries] - m_new[:, queries]).astype(BF16)
                 for queries, keys in column_parts(on_diagonal)]
            probs.append((jnp.exp2(m_old - m_new), p))
        for hh in heads:
            alpha, p = probs[hh]
            for part, (queries, keys) in zip(p, column_parts(on_diagonal)):
                k_cols = pl.ds(pl.multiple_of(k_off + keys.start, half), keys.stop - keys.start)
                vt = vt_ref[0, hh * MLA_V:(hh + 1) * MLA_V, k_cols]
                vt_ones = jnp.concatenate([vt, ones[:, keys]], axis=0)
                acc_ref[qi, hh, :, queries] = (alpha[:, queries] * acc_ref[qi, hh, :, queries]
                                               + _dot(vt_ones, part))

    def pipeline(first, count, on_diagonal, unroll):
        if count == 0:
            return
        assert unroll % 2 == 0
        n_loop = (count - 1) // unroll

        def several(pos, n, block_max, score_last):
            for u in range(n):
                nxt = score((u + 1) % 2, pos + u + 1, on_diagonal) if (u + 1 < n or score_last) else None
                absorb(u % 2, pos + u, block_max, on_diagonal)
                block_max = nxt
            return block_max

        block_max = lax.fori_loop(0, n_loop, lambda t, bm: several(first + unroll * t, unroll, bm, True),
                                  score(0, first, on_diagonal))
        several(first + unroll * n_loop, count - unroll * n_loop, block_max, False)

    pipeline(0, n_q, True, 2)
    pipeline(n_q, n_off, False, 4)

    def finish(qi, carry):
        out_t = jnp.concatenate([acc_ref[qi, hh, :MLA_V, :] / acc_ref[qi, hh, MLA_V:MLA_V + 1, :] for hh in heads],
                                axis=0)
        o_ref[0, pl.ds(pl.multiple_of(qi * tq, tq), tq), :] = out_t.T.astype(BF16)
        return carry

    lax.fori_loop(0, n_q, finish, 0)


def _attn_call(q, k, vt):
    b, _, s, _ = q.shape
    tq = min(TQ_ATTN, s)
    n_q = s // tq
    off = [(qi, kj) for kj in range(n_q) for qi in range(kj + 1, n_q)]
    order = [(qi, qi) for qi in range(n_q)] + off
    qtab = jnp.asarray([p[0] for p in order], jnp.int32)
    ktab = jnp.asarray([p[1] for p in order], jnp.int32)
    grid_spec = pltpu.PrefetchScalarGridSpec(
        num_scalar_prefetch=2,
        grid=(b, N_GROUPS),
        in_specs=[pl.BlockSpec((1, HEADS_PER_GROUP, s, HEAD_PAD), lambda bi, g, qt, kt: (bi, g, 0, 0)),
                  pl.BlockSpec((1, HEADS_PER_GROUP, s, HEAD_PAD), lambda bi, g, qt, kt: (bi, g, 0, 0)),
                  pl.BlockSpec((1, LANES, s), lambda bi, g, qt, kt: (bi, g, 0))],
        out_specs=pl.BlockSpec((1, s, LANES), lambda bi, g, qt, kt: (bi, 0, g)),
        scratch_shapes=[pltpu.VMEM((2, HEADS_PER_GROUP, tq, tq), F32),
                        pltpu.VMEM((n_q, HEADS_PER_GROUP, 1, tq), F32),
                        pltpu.VMEM((n_q, HEADS_PER_GROUP, MLA_V + SUM_ROWS, tq), F32)])
    return pl.pallas_call(
        _attn_body,
        grid_spec=grid_spec,
        out_shape=jax.ShapeDtypeStruct((b, s, MLA_WIDTH), BF16),
        compiler_params=pltpu.CompilerParams(dimension_semantics=("parallel", "parallel"),
                                             vmem_limit_bytes=VMEM_LIMIT),
        name="attn",
    )(qtab, ktab, q, k, vt)


def _out_body(x_ref, attn_ref, ga_ref, o_ref, gb_ref, gt_ref, wa_ref, wb_ref, wo_ref, gpost_ref, ghg_ref, hsum_ref,
              out_ref):
    tm = x_ref.shape[1]
    parts = [slice(i * tm // OUT_PARTS, (i + 1) * tm // OUT_PARTS) for i in range(OUT_PARTS)]

    def sq_sum(o):
        return _dot((o * o).astype(BF16), hsum_ref[...])

    def merged(r, o, sq):
        ya = _dot(attn_ref[0, r] * ga_ref[0, r], wa_ref[...])
        o = o * lax.rsqrt(sq * (1.0 / HG_DV) + EPS) * ghg_ref[...]
        yb = _dot((o * gb_ref[0, r].astype(F32)).astype(BF16), wb_ref[...])
        m = gt_ref[0, r, :D_MODEL].astype(F32) * ya + gt_ref[0, r, D_MODEL:].astype(F32) * yb
        return m.astype(BF16)

    os_ = [o_ref[0, r].astype(F32) for r in parts]
    sqs = [sq_sum(o) for o in os_]
    ms = [merged(r, o, sq) for r, o, sq in zip(parts, os_, sqs)]
    ys = [_dot(m, wo_ref[...]) for m in ms]
    for r, y in zip(parts, ys):
        out_ref[0, r] = x_ref[0, r] + _rms(y) * gpost_ref[...]


def _out_call(x, attn, ga, o, gb, gt, wa, wb, wo, gpost, ghg):
    b, s, _ = x.shape
    hcol = np.arange(HG_WIDTH) // HG_DV
    hsum = jnp.asarray(hcol[:, None] == hcol[None, :], BF16)
    tm = min(TM_OUT, s)
    tok = lambda bi, si: (bi, si, 0)
    const = lambda bi, si: (0, 0)

    def tspec(a):
        return pl.BlockSpec((1, tm, a.shape[-1]), tok)

    def full(a):
        return pl.BlockSpec(a.shape, const)

    return pl.pallas_call(
        _out_body,
        grid=(b, s // tm),
        in_specs=[tspec(x), tspec(attn), tspec(ga), tspec(o), tspec(gb), tspec(gt),
                  full(wa), full(wb), full(wo), full(gpost), full(ghg), full(hsum)],
        out_specs=tspec(x),
        out_shape=jax.ShapeDtypeStruct(x.shape, x.dtype),
        compiler_params=pltpu.CompilerParams(dimension_semantics=("parallel", "parallel"),
                                             vmem_limit_bytes=VMEM_LIMIT),
        name="merge_out",
    )(x, attn, ga, o, gb, gt, wa, wb, wo, gpost, ghg, hsum)


def _rope_tables(s):
    inv = ROPE_THETA ** (-jnp.arange(0, MLA_ROPE, 2, dtype=F32) / MLA_ROPE)
    ang = jnp.arange(s, dtype=F32)[:, None] * inv[None, :]
    cos, sin = jnp.cos(ang), jnp.sin(ang)
    pad = jnp.zeros((s, HEAD_PAD - MLA_QK), F32)
    ra = jnp.concatenate([jnp.ones((s, MLA_NOPE), F32), cos, cos, pad], axis=1)
    rs = jnp.concatenate([jnp.zeros((s, MLA_NOPE), F32), -sin, sin, pad], axis=1)
    return ra, rs


def _pad_weights(w_in, w_uq, w_ukv):
    kpe_lo = Q_LORA + KV_LORA
    w_lat = w_in[:, :kpe_lo].astype(BF16)
    w_kpe = jnp.pad(w_in[:, kpe_lo:kpe_lo + MLA_ROPE].astype(BF16), ((0, 0), (MLA_NOPE, HEAD_PAD - MLA_QK)))
    w_wide = w_in[:, kpe_lo + MLA_ROPE:].astype(BF16)
    wuq =jnp.pad(w_uq.reshape(Q_LORA, MLA_HEADS, MLA_QK), ((0, 0), (0, 0), (0, HEAD_PAD - MLA_QK)))
    wkv = w_ukv.reshape(KV_LORA, MLA_HEADS, MLA_NOPE + MLA_V)
    wk = jnp.pad(wkv[..., :MLA_NOPE], ((0, 0), (0, 0), (0, HEAD_PAD - MLA_NOPE)))
    wukv = jnp.concatenate([wk.reshape(KV_LORA, MLA_HEADS * HEAD_PAD),
                            wkv[..., MLA_NOPE:].reshape(KV_LORA, MLA_WIDTH)], axis=1)
    return ((w_lat, w_kpe, w_wide), wuq.reshape(Q_LORA, MLA_HEADS * HEAD_PAD).astype(BF16), wukv.astype(BF16))


def kernel(x, g_pre, w_in, b_gate, g_q, w_uq, g_kv, w_ukv, lb_logits, g_hgrn, w_branch_a, w_branch_b, w_out,
           g_post):
    assert g_pre.shape[0] == 1, "single-layer block"
    s = x.shape[1]
    win, wuq, wukv = _pad_weights(w_in[0], w_uq[0], w_ukv[0])
    ra, rs = _rope_tables(s)
    lower_bound = jax.nn.softmax(lb_logits.astype(F32), axis=0)[0:1]
    ghg = jnp.tile(g_hgrn[0], HG_HEADS)[None, :]

    q, k, vt, ga, o, gb, gt = _proj_call(x, g_pre, win, g_q, wuq, g_kv, wukv, ra, rs, b_gate, lower_bound)
    attn = _attn_call(q, k, vt)
    return _out_call(x, attn, ga, o, gb, gt, w_branch_a[0].astype(BF16), w_branch_b[0].astype(BF16),
                     w_out[0].astype(BF16), g_post, ghg)
```

```python
import functools
import math

import jax
import jax.numpy as jnp
import numpy as np
from jax import lax
from jax.experimental import pallas as pl
from jax.experimental.pallas import tpu as pltpu

F32 = jnp.float32
BF16 = jnp.bfloat16

D_MODEL = 1024
CHUNK = 64
EPS = 1e-6

MLA_HEADS = 8
MLA_NOPE = 64
MLA_ROPE = 32
MLA_V = 64
MLA_QK = MLA_NOPE + MLA_ROPE
Q_LORA = 768
KV_LORA = 256
ROPE_THETA = 10000.0
MLA_WIDTH = MLA_HEADS * MLA_V

HG_HEADS = 8
HG_DK = 64
HG_DV = 64
HG_WIDTH = HG_HEADS * HG_DV

LANES = 128
HEAD_PAD = LANES
HEADS_PER_GROUP = LANES // MLA_V
N_GROUPS = MLA_HEADS // HEADS_PER_GROUP

COL_CQ = 0
COL_CKV = COL_CQ + Q_LORA
COL_KPE = COL_CKV + KV_LORA
COL_GA = COL_KPE + HEAD_PAD
COL_HQ = COL_GA + MLA_WIDTH
COL_HF = COL_HQ + HG_WIDTH
COL_HI = COL_HF + HG_WIDTH
COL_GB = COL_HI + HG_WIDTH
COL_MG = COL_GB + HG_WIDTH
D_IN_PAD = COL_MG + 2 * D_MODEL

VMEM_LIMIT = 56 * 1024 * 1024

TM_PROJ = 512
TQ_ATTN = 512
SUM_ROWS = 16
C_HGRN = 64
TM_OUT = 1024
OUT_PARTS = 2
W_SPLIT_ROWS = 128


def _sigmoid(z):
    return 1.0 / (1.0 + jnp.exp(-z))


def _rms(t):
    return t * lax.rsqrt(jnp.mean(t * t, axis=-1, keepdims=True) + EPS)


def _dot(a, b):
    return jnp.dot(a, b, preferred_element_type=F32)


def _dot_nt(a, b):
    return lax.dot_general(a, b, (((1,), (1,)), ((), ())), preferred_element_type=F32)


def _dot_split(a_bf16, b_f32, terms):
    acc = None
    rem = b_f32
    for _ in range(terms):
        piece = rem.astype(BF16)
        rem = rem - piece.astype(F32)
        part = _dot(a_bf16, piece)
        acc = part if acc is None else acc + part
    return acc


def _recurrence_stages(proj, lb, ltri, o_ref):
    c_len = C_HGRN
    groups = range(HG_HEADS // 2)
    low = lax.broadcasted_iota(jnp.int32, (c_len, LANES), 1) < HG_DK
    causal = (lax.broadcasted_iota(jnp.int32, (2 * c_len, c_len), 0) % c_len
              >= lax.broadcasted_iota(jnp.int32, (2 * c_len, c_len), 1))
    same_head = ((lax.broadcasted_iota(jnp.int32, (LANES, LANES), 0) < HG_DV)
                 == (lax.broadcasted_iota(jnp.int32, (LANES, LANES), 1) < HG_DK))

    def stack_heads(t):
        zero = jnp.zeros_like(t)
        return jnp.concatenate([jnp.where(low, t, zero), jnp.where(low, zero, t)], axis=0)

    def gates(c):
        f = lb + (1.0 - lb) * _sigmoid(proj["hf"][c * c_len:(c + 1) * c_len])
        return 1.0 - f, _dot_split(ltri, jnp.log2(f), 3)

    def scores(c, gated):
        k_in, cum = gated
        rows = slice(c * c_len, (c + 1) * c_len)
        mid = cum[c_len // 2 - 1:c_len // 2, :]
        tot = cum[c_len - 1:c_len, :]
        q_mid = proj["hq"][rows] * jnp.exp2(cum - mid)
        k_mid = k_in * jnp.exp2(mid - cum)
        q_dec = (q_mid * jnp.exp2(mid)).astype(BF16)
        k_end = (k_mid * jnp.exp2(tot - mid)).astype(BF16)
        q_mid = q_mid.astype(BF16)
        k_mid = k_mid.astype(BF16)
        v = proj["hi"][rows]
        per_group = []
        for g in groups:
            cols = slice(g * LANES, (g + 1) * LANES)
            a = _dot_nt(stack_heads(q_mid[:, cols]), k_mid[:, cols])
            v_t = v[:, cols].T.astype(BF16)
            upd = _dot(v_t, k_end[:, cols])
            per_group.append((a, v_t, upd, stack_heads(q_dec[:, cols])))
        return per_group, jnp.exp2(tot)

    def output(c, scored, states):
        per_group, decay = scored
        rows = slice(c * c_len, (c + 1) * c_len)
        new_states = []
        for g in groups:
            cols = slice(g * LANES, (g + 1) * LANES)
            a, v_t, upd, q_dec = per_group[g]
            a = jnp.where(causal, a, 0.0).astype(BF16)
            lhs = jnp.concatenate([q_dec, a], axis=1)
            rhs_t = jnp.concatenate([states[g].astype(BF16), v_t], axis=1)
            res = _dot_nt(lhs, rhs_t)
            o_ref[0, rows, cols] = jnp.where(low, res[:c_len], res[c_len:]).astype(BF16)
            new_states.append(states[g] * decay[:, cols] + jnp.where(same_head, upd, 0.0))
        return new_states

    return gates, scores, output


def _proj_body(x_ref, gpre_ref, wlat_ref, wkpe_ref, wwide_ref, gq_ref, wuq_ref, gkv_ref, wukv_ref, ra_ref, rs_ref,
               bg_ref, lb_ref, ltri_ref, q_ref, k_ref, v_ref, ga_ref, o_ref, gb_ref, gt_ref, st_ref):
    tm = x_ref.shape[1]
    groups = range(HG_HEADS // 2)

    @pl.when(pl.program_id(1) == 0)
    def _():
        st_ref[...] = jnp.zeros_like(st_ref)

    h = (_rms(x_ref[0]) * gpre_ref[...]).astype(BF16)

    def proj(lo, width):
        if lo + width <= COL_KPE:
            return _dot(h, wlat_ref[:, lo:lo + width])
        if lo == COL_KPE:
            return _dot(h, wkpe_ref[...])
        return _dot(h, wwide_ref[:, lo - COL_GA:lo - COL_GA + width])

    ra = ra_ref[...]
    rs = rs_ref[...]
    lane = lax.broadcasted_iota(jnp.int32, ra.shape, 1)
    takes_upper = lane < MLA_NOPE + MLA_ROPE // 2

    def rope(t):
        partner = jnp.where(takes_upper, pltpu.roll(t, LANES - MLA_ROPE // 2, 1), pltpu.roll(t, MLA_ROPE // 2, 1))
        return t * ra + partner * rs

    held = {}

    def q_down():
        held["cqn"] = (_rms(proj(COL_CQ, Q_LORA)) * gq_ref[...]).astype(BF16)

    def q_up():
        scale = math.log2(math.e) / math.sqrt(MLA_QK)
        qu = _dot(held["cqn"], wuq_ref[...])
        for hh in range(MLA_HEADS):
            q_ref[0, hh] = (rope(qu[:, hh * HEAD_PAD:(hh + 1) * HEAD_PAD]) * scale).astype(BF16)

    def kv_down():
        held["ckvn"] = (_rms(proj(COL_CKV, KV_LORA)) * gkv_ref[...]).astype(BF16)
        held["kpe"] = rope(proj(COL_KPE, HEAD_PAD))

    def kv_up():
        kvu = _dot(held["ckvn"], wukv_ref[...])
        for hh in range(MLA_HEADS):
            k_ref[0, hh] = (kvu[:, hh * HEAD_PAD:(hh + 1) * HEAD_PAD] + held["kpe"]).astype(BF16)
        v_ref[0] = kvu[:, MLA_HEADS * HEAD_PAD:].T.astype(BF16)

    def silu_gate(lo, out_ref):
        z = proj(lo, out_ref.shape[-1])
        out_ref[0] = (z * _sigmoid(z)).astype(BF16)

    def merge_gates(c, width=2 * LANES):
        z = proj(COL_MG + c * width, width) + bg_ref[:, c * width:(c + 1) * width]
        gt_ref[0, :, c * width:(c + 1) * width] = _sigmoid(z).astype(BF16)

    pieces = [functools.partial(merge_gates, c) for c in range(D_MODEL // LANES)]
    pieces[1:1] = [q_down]
    pieces[3:3] = [kv_down]
    pieces[5:5] = [q_up]
    pieces[7:7] = [kv_up]
    pieces += [functools.partial(silu_gate, COL_GA, ga_ref), functools.partial(silu_gate, COL_GB, gb_ref)]

    def next_piece():
        if pieces:
            pieces.pop(0)()

    rec = {}
    gates, scores, output = _recurrence_stages(rec, lb_ref[...], ltri_ref[...], o_ref)
    n_chunks = tm // C_HGRN
    rec["hf"] = proj(COL_HF, HG_WIDTH)
    gated = {0: gates(0)}
    rec["hq"] = proj(COL_HQ, HG_WIDTH)
    if n_chunks > 1:
        gated[1] = gates(1)
    rec["hi"] = proj(COL_HI, HG_WIDTH)
    scored = {0: scores(0, gated.pop(0))}
    states = [st_ref[g] for g in groups]
    for c in range(n_chunks):
        next_piece()
        if c + 2 < n_chunks:
            gated[c + 2] = gates(c + 2)
        next_piece()
        if c + 1 < n_chunks:
            scored[c + 1] = scores(c + 1, gated.pop(c + 1))
        states = output(c, scored.pop(c), states)
    for g in groups:
        st_ref[g] = states[g]
    while pieces:
        next_piece()


def _proj_call(x, gpre, win, gq, wuq, gkv, wukv, ra, rs, bg, lb):
    b, s, _ = x.shape
    tm = min(TM_PROJ, s)
    r = np.arange(C_HGRN)
    ltri = jnp.asarray(r[:, None] >= r[None, :], BF16)
    const = lambda bi, si: (0, 0)
    tok = lambda bi, si: (bi, si, 0)

    def full(a):
        return pl.BlockSpec(a.shape, const, pipeline_mode=pl.Buffered(1))

    def tok_out(width, dtype):
        return jax.ShapeDtypeStruct((b, s, width), dtype), pl.BlockSpec((1, tm, width), tok)

    head_shape = jax.ShapeDtypeStruct((b, MLA_HEADS, s, HEAD_PAD), BF16)
    head_spec = pl.BlockSpec((1, MLA_HEADS, tm, HEAD_PAD), lambda bi, si: (bi, 0, si, 0))
    outs = [(head_shape, head_spec), (head_shape, head_spec),
            (jax.ShapeDtypeStruct((b, MLA_WIDTH, s), BF16),
             pl.BlockSpec((1, MLA_WIDTH, tm), lambda bi, si: (bi, 0, si))),
            tok_out(MLA_WIDTH, BF16), tok_out(HG_WIDTH, BF16), tok_out(HG_WIDTH, BF16),
            tok_out(2 * D_MODEL, BF16)]
    rope_spec = pl.BlockSpec((tm, HEAD_PAD), lambda bi, si: (si, 0))
    return pl.pallas_call(
        _proj_body,
        grid=(b, s // tm),
        in_specs=[pl.BlockSpec((1, tm, D_MODEL), tok), full(gpre), *(full(w) for w in win), full(gq), full(wuq),
                  full(gkv), full(wukv), rope_spec, rope_spec, full(bg), full(lb), full(ltri)],
        out_specs=[o[1] for o in outs],
        out_shape=[o[0] for o in outs],
        scratch_shapes=[pltpu.VMEM((HG_HEADS // 2, LANES, LANES), F32)],
        compiler_params=pltpu.CompilerParams(dimension_semantics=("parallel", "arbitrary"),
                                             vmem_limit_bytes=VMEM_LIMIT),
        name="proj_hgrn",
    )(x, gpre, *win, gq, wuq, gkv, wukv, ra, rs, bg, lb, ltri)


def _attn_body(qtab_ref, ktab_ref, q_ref, k_ref, vt_ref, o_ref, s_ref, m_ref, acc_ref):
    tq = s_ref.shape[-1]
    n_q = q_ref.shape[2] // tq
    n_off = n_q * (n_q - 1) // 2
    heads = range(HEADS_PER_GROUP)
    key_chunk = lax.broadcasted_iota(jnp.int32, (tq, tq), 0) // CHUNK
    query_chunk = lax.broadcasted_iota(jnp.int32, (tq, tq), 1) // CHUNK
    visible = key_chunk <= query_chunk
    ones = jnp.ones((SUM_ROWS, tq), BF16)

    m_ref[...] = jnp.full(m_ref.shape, -jnp.inf, F32)
    acc_ref[...] = jnp.zeros(acc_ref.shape, F32)

    half = tq // 2
    assert half % CHUNK == 0 and half % LANES == 0
    early, late, everything = slice(0, half), slice(half, tq), slice(0, tq)

    def column_parts(on_diagonal):
        return [(early, early), (late, everything)] if on_diagonal else [(everything, everything)]

    def score(slot, pos, on_diagonal):
        q_off = qtab_ref[pos] * tq
        k_off = ktab_ref[pos] * tq
        block_max = []
        for hh in heads:
            part_max = []
            for queries, keys in column_parts(on_diagonal):
                k_rows = pl.ds(pl.multiple_of(k_off + keys.start, half), keys.stop - keys.start)
                q_rows = pl.ds(pl.multiple_of(q_off + queries.start, half), queries.stop - queries.start)
                sc = _dot_nt(k_ref[0, hh, k_rows, :], q_ref[0, hh, q_rows, :])
                if on_diagonal:
                    sc = jnp.where(visible[keys, queries], sc, -jnp.inf)
                s_ref[slot, hh, keys, queries] = sc
                part_max.append(jnp.max(sc, axis=0, keepdims=True))
            block_max.append(jnp.concatenate(part_max, axis=1))
        return tuple(block_max)

    def absorb(slot, pos, block_max, on_diagonal):
        qi = qtab_ref[pos]
        k_off = ktab_ref[pos] * tq
        probs = []
        for hh in heads:
            m_old = m_ref[qi, hh]
            m_new = jnp.maximum(m_old, block_max[hh])
            m_ref[qi, hh] = m_new
            p = [jnp.exp2(s_ref[slot, hh, keys, queries] - m_new[:, queries]).astype(BF16)
                 for queries, keys in column_parts(on_diagonal)]
            probs.append((jnp.exp2(m_old - m_new), p))
        for hh in heads:
            alpha, p = probs[hh]
            for part, (queries, keys) in zip(p, column_parts(on_diagonal)):
                k_cols = pl.ds(pl.multiple_of(k_off + keys.start, half), keys.stop - keys.start)
                vt = vt_ref[0, hh * MLA_V:(hh + 1) * MLA_V, k_cols]
                vt_ones = jnp.concatenate([vt, ones[:, keys]], axis=0)
                acc_ref[qi, hh, :, queries] = (alpha[:, queries] * acc_ref[qi, hh, :, queries]
                                               + _dot(vt_ones, part))

    def pipeline(first, count, on_diagonal, unroll):
        if count == 0:
            return
        assert unroll % 2 == 0
        n_loop = (count - 1) // unroll

        def several(pos, n, block_max, score_last):
            for u in range(n):
                nxt = score((u + 1) % 2, pos + u + 1, on_diagonal) if (u + 1 < n or score_last) else None
                absorb(u % 2, pos + u, block_max, on_diagonal)
                block_max = nxt
            return block_max

        block_max = lax.fori_loop(0, n_loop, lambda t, bm: several(first + unroll * t, unroll, bm, True),
                                  score(0, first, on_diagonal))
        several(first + unroll * n_loop, count - unroll * n_loop, block_max, False)

    pipeline(0, n_q, True, 2)
    pipeline(n_q, n_off, False, 8)

    def finish(qi, carry):
        out_t = jnp.concatenate([acc_ref[qi, hh, :MLA_V, :] / acc_ref[qi, hh, MLA_V:MLA_V + 1, :] for hh in heads],
                                axis=0)
        o_ref[0, pl.ds(pl.multiple_of(qi * tq, tq), tq), :] = out_t.T.astype(BF16)
        return carry

    lax.fori_loop(0, n_q, finish, 0)


def _attn_call(q, k, vt):
    b, _, s, _ = q.shape
    tq = min(TQ_ATTN, s)
    n_q = s // tq
    off = [(qi, kj) for kj in range(n_q) for qi in range(kj + 1, n_q)]
    order = [(qi, qi) for qi in range(n_q)] + off
    qtab = jnp.asarray([p[0] for p in order], jnp.int32)
    ktab = jnp.asarray([p[1] for p in order], jnp.int32)
    grid_spec = pltpu.PrefetchScalarGridSpec(
        num_scalar_prefetch=2,
        grid=(b, N_GROUPS),
        in_specs=[pl.BlockSpec((1, HEADS_PER_GROUP, s, HEAD_PAD), lambda bi, g, qt, kt: (bi, g, 0, 0)),
                  pl.BlockSpec((1, HEADS_PER_GROUP, s, HEAD_PAD), lambda bi, g, qt, kt: (bi, g, 0, 0)),
                  pl.BlockSpec((1, LANES, s), lambda bi, g, qt, kt: (bi, g, 0))],
        out_specs=pl.BlockSpec((1, s, LANES), lambda bi, g, qt, kt: (bi, 0, g)),
        scratch_shapes=[pltpu.VMEM((2, HEADS_PER_GROUP, tq, tq), F32),
                        pltpu.VMEM((n_q, HEADS_PER_GROUP, 1, tq), F32),
                        pltpu.VMEM((n_q, HEADS_PER_GROUP, MLA_V + SUM_ROWS, tq), F32)])
    return pl.pallas_call(
        _attn_body,
        grid_spec=grid_spec,
        out_shape=jax.ShapeDtypeStruct((b, s, MLA_WIDTH), BF16),
        compiler_params=pltpu.CompilerParams(dimension_semantics=("parallel", "parallel"),
                                             vmem_limit_bytes=VMEM_LIMIT),
        name="attn",
    )(qtab, ktab, q, k, vt)


def _out_body(x_ref, attn_ref, ga_ref, o_ref, gb_ref, gt_ref, wa_ref, wb_ref, wo_ref, gpost_ref, ghg_ref, hsum_ref,
              out_ref):
    tm = x_ref.shape[1]
    parts = [slice(i * tm // OUT_PARTS, (i + 1) * tm // OUT_PARTS) for i in range(OUT_PARTS)]

    def sq_sum(o):
        return _dot((o * o).astype(BF16), hsum_ref[...])

    def merged(r, o, sq):
        ya = _dot(attn_ref[0, r] * ga_ref[0, r], wa_ref[...])
        o = o * lax.rsqrt(sq * (1.0 / HG_DV) + EPS) * ghg_ref[...]
        yb = _dot((o * gb_ref[0, r].astype(F32)).astype(BF16), wb_ref[...])
        m = gt_ref[0, r, :D_MODEL].astype(F32) * ya + gt_ref[0, r, D_MODEL:].astype(F32) * yb
        return m.astype(BF16)

    os_ = [o_ref[0, r].astype(F32) for r in parts]
    sqs = [sq_sum(o) for o in os_]
    ms = [merged(r, o, sq) for r, o, sq in zip(parts, os_, sqs)]
    ys = [_dot(m, wo_ref[...]) for m in ms]
    for r, y in zip(parts, ys):
        out_ref[0, r] = x_ref[0, r] + _rms(y) * gpost_ref[...]


def _out_call(x, attn, ga, o, gb, gt, wa, wb, wo, gpost, ghg):
    b, s, _ = x.shape
    hcol = np.arange(HG_WIDTH) // HG_DV
    hsum = jnp.asarray(hcol[:, None] == hcol[None, :], BF16)
    tm = min(TM_OUT, s)
    tok = lambda bi, si: (bi, si, 0)
    const = lambda bi, si: (0, 0)

    def tspec(a):
        return pl.BlockSpec((1, tm, a.shape[-1]), tok)

    def full(a):
        return pl.BlockSpec(a.shape, const)

    return pl.pallas_call(
        _out_body,
        grid=(b, s // tm),
        in_specs=[tspec(x), tspec(attn), tspec(ga), tspec(o), tspec(gb), tspec(gt),
                  full(wa), full(wb), full(wo), full(gpost), full(ghg), full(hsum)],
        out_specs=tspec(x),
        out_shape=jax.ShapeDtypeStruct(x.shape, x.dtype),
        compiler_params=pltpu.CompilerParams(dimension_semantics=("parallel", "parallel"),
                                             vmem_limit_bytes=VMEM_LIMIT),
        name="merge_out",
    )(x, attn, ga, o, gb, gt, wa, wb, wo, gpost, ghg, hsum)


def _rope_tables(s):
    inv = ROPE_THETA ** (-jnp.arange(0, MLA_ROPE, 2, dtype=F32) / MLA_ROPE)
    ang = jnp.arange(s, dtype=F32)[:, None] * inv[None, :]
    cos, sin = jnp.cos(ang), jnp.sin(ang)
    pad = jnp.zeros((s, HEAD_PAD - MLA_QK), F32)
    ra = jnp.concatenate([jnp.ones((s, MLA_NOPE), F32), cos, cos, pad], axis=1)
    rs = jnp.concatenate([jnp.zeros((s, MLA_NOPE), F32), -sin, sin, pad], axis=1)
    return ra, rs


def _split_w_in_body(w_ref, lat_ref, kpe_ref, wide_ref):
    kpe_lo = Q_LORA + KV_LORA
    w = w_ref[0]
    rows = w.shape[0]
    lat_ref[...] = w[:, :kpe_lo].astype(BF16)
    kpe_ref[...] = jnp.concatenate([jnp.zeros((rows, MLA_NOPE), F32), w[:, kpe_lo:kpe_lo + MLA_ROPE],
                                    jnp.zeros((rows, HEAD_PAD - MLA_QK), F32)], axis=1).astype(BF16)
    wide_ref[...] = w[:, kpe_lo + MLA_ROPE:].astype(BF16)


def _split_w_in(w_in):
    _, d, n = w_in.shape
    kpe_lo = Q_LORA + KV_LORA
    n_wide = n - kpe_lo - MLA_ROPE
    rows = W_SPLIT_ROWS
    return pl.pallas_call(
        _split_w_in_body,
        grid=(d // rows,),
        in_specs=[pl.BlockSpec((1, rows, n), lambda i: (0, i, 0))],
        out_specs=[pl.BlockSpec((rows, kpe_lo), lambda i: (i, 0)), pl.BlockSpec((rows, HEAD_PAD), lambda i: (i, 0)),
                   pl.BlockSpec((rows, n_wide), lambda i: (i, 0))],
        out_shape=[jax.ShapeDtypeStruct((d, kpe_lo), BF16), jax.ShapeDtypeStruct((d, HEAD_PAD), BF16),
                   jax.ShapeDtypeStruct((d, n_wide), BF16)],
        compiler_params=pltpu.CompilerParams(dimension_semantics=("parallel",), vmem_limit_bytes=VMEM_LIMIT),
        name="split_w_in",
    )(w_in)


def _pad_weights(w_in, w_uq, w_ukv):
    wuq = jnp.pad(w_uq.reshape(Q_LORA, MLA_HEADS, MLA_QK), ((0, 0), (0, 0), (0, HEAD_PAD - MLA_QK)))
    wkv = w_ukv.reshape(KV_LORA, MLA_HEADS, MLA_NOPE + MLA_V)
    wk = jnp.pad(wkv[..., :MLA_NOPE], ((0, 0), (0, 0), (0, HEAD_PAD - MLA_NOPE)))
    wukv = jnp.concatenate([wk.reshape(KV_LORA, MLA_HEADS * HEAD_PAD),
                            wkv[..., MLA_NOPE:].reshape(KV_LORA, MLA_WIDTH)], axis=1)
    return _split_w_in(w_in), wuq.reshape(Q_LORA, MLA_HEADS * HEAD_PAD).astype(BF16), wukv.astype(BF16)


def kernel(x, g_pre, w_in, b_gate, g_q, w_uq, g_kv, w_ukv, lb_logits, g_hgrn, w_branch_a, w_branch_b, w_out,
           g_post):
    assert g_pre.shape[0] == 1, "single-layer block"
    s = x.shape[1]
    win, wuq, wukv = _pad_weights(w_in, w_uq[0], w_ukv[0])
    ra, rs = _rope_tables(s)
    lower_bound = jax.nn.softmax(lb_logits.astype(F32), axis=0)[0:1]
    ghg = jnp.tile(g_hgrn[0], HG_HEADS)[None, :]

    q, k, vt, ga, o, gb, gt = _proj_call(x, g_pre, win, g_q, wuq, g_kv, wukv, ra, rs, b_gate, lower_bound)
    attn = _attn_call(q, k, vt)
    return _out_call(x, attn, ga, o, gb, gt, w_branch_a[0].astype(BF16), w_branch_b[0].astype(BF16),
                     w_out[0].astype(BF16), g_post, ghg)
```

```python
import functools
import math

import jax
import jax.numpy as jnp
import numpy as np
from jax import lax
from jax.experimental import pallas as pl
from jax.experimental.pallas import tpu as pltpu

F32 = jnp.float32
BF16 = jnp.bfloat16

D_MODEL = 1024
CHUNK = 64
EPS = 1e-6

MLA_HEADS = 8
MLA_NOPE = 64
MLA_ROPE = 32
MLA_V = 64
MLA_QK = MLA_NOPE + MLA_ROPE
Q_LORA = 768
KV_LORA = 256
ROPE_THETA = 10000.0
MLA_WIDTH = MLA_HEADS * MLA_V

HG_HEADS = 8
HG_DK = 64
HG_DV = 64
HG_WIDTH = HG_HEADS * HG_DV

LANES = 128
HEAD_PAD = LANES
HEADS_PER_GROUP = LANES // MLA_V
N_GROUPS = MLA_HEADS // HEADS_PER_GROUP

COL_CQ = 0
COL_CKV = COL_CQ + Q_LORA
COL_KPE = COL_CKV + KV_LORA
COL_GA = COL_KPE + HEAD_PAD
COL_HQ = COL_GA + MLA_WIDTH
COL_HF = COL_HQ + HG_WIDTH
COL_HI = COL_HF + HG_WIDTH
COL_GB = COL_HI + HG_WIDTH
COL_MG = COL_GB + HG_WIDTH
D_IN_PAD = COL_MG + 2 * D_MODEL

VMEM_LIMIT = 56 * 1024 * 1024

TM_PROJ = 512
TQ_ATTN = 512
SUM_ROWS = 16
C_HGRN = 64
TM_OUT = 1024
OUT_PARTS = 2


def _sigmoid(z):
    return 1.0 / (1.0 + jnp.exp(-z))


def _rms(t):
    return t * lax.rsqrt(jnp.mean(t * t, axis=-1, keepdims=True) + EPS)


def _dot(a, b):
    return jnp.dot(a, b, preferred_element_type=F32)


def _dot_nt(a, b):
    return lax.dot_general(a, b, (((1,), (1,)), ((), ())), preferred_element_type=F32)


def _dot_split(a_bf16, b_f32, terms):
    acc = None
    rem = b_f32
    for _ in range(terms):
        piece = rem.astype(BF16)
        rem = rem - piece.astype(F32)
        part = _dot(a_bf16, piece)
        acc = part if acc is None else acc + part
    return acc


def _recurrence_stages(proj, lb, ltri, o_ref):
    c_len = C_HGRN
    groups = range(HG_HEADS // 2)
    low = lax.broadcasted_iota(jnp.int32, (c_len, LANES), 1) < HG_DK
    causal = (lax.broadcasted_iota(jnp.int32, (2 * c_len, c_len), 0) % c_len
              >= lax.broadcasted_iota(jnp.int32, (2 * c_len, c_len), 1))
    same_head = ((lax.broadcasted_iota(jnp.int32, (LANES, LANES), 0) < HG_DV)
                 == (lax.broadcasted_iota(jnp.int32, (LANES, LANES), 1) < HG_DK))

    def stack_heads(t):
        zero = jnp.zeros_like(t)
        return jnp.concatenate([jnp.where(low, t, zero), jnp.where(low, zero, t)], axis=0)

    def gates(c):
        f = lb + (1.0 - lb) * _sigmoid(proj["hf"][c * c_len:(c + 1) * c_len])
        return 1.0 - f, _dot_split(ltri, jnp.log2(f), 3)

    def scores(c, gated):
        k_in, cum = gated
        rows = slice(c * c_len, (c + 1) * c_len)
        mid = cum[c_len // 2 - 1:c_len // 2, :]
        tot = cum[c_len - 1:c_len, :]
        q_mid = proj["hq"][rows] * jnp.exp2(cum - mid)
        k_mid = k_in * jnp.exp2(mid - cum)
        q_dec = (q_mid * jnp.exp2(mid)).astype(BF16)
        k_end = (k_mid * jnp.exp2(tot - mid)).astype(BF16)
        q_mid = q_mid.astype(BF16)
        k_mid = k_mid.astype(BF16)
        v = proj["hi"][rows]
        per_group = []
        for g in groups:
            cols = slice(g * LANES, (g + 1) * LANES)
            a = _dot_nt(stack_heads(q_mid[:, cols]), k_mid[:, cols])
            v_t = v[:, cols].T.astype(BF16)
            upd = _dot(v_t, k_end[:, cols])
            per_group.append((a, v_t, upd, stack_heads(q_dec[:, cols])))
        return per_group, jnp.exp2(tot)

    def output(c, scored, states):
        per_group, decay = scored
        rows = slice(c * c_len, (c + 1) * c_len)
        new_states = []
        for g in groups:
            cols = slice(g * LANES, (g + 1) * LANES)
            a, v_t, upd, q_dec = per_group[g]
            a = jnp.where(causal, a, 0.0).astype(BF16)
            lhs = jnp.concatenate([q_dec, a], axis=1)
            rhs_t = jnp.concatenate([states[g].astype(BF16), v_t], axis=1)
            res = _dot_nt(lhs, rhs_t)
            o_ref[0, rows, cols] = jnp.where(low, res[:c_len], res[c_len:]).astype(BF16)
            new_states.append(states[g] * decay[:, cols] + jnp.where(same_head, upd, 0.0))
        return new_states

    return gates, scores, output


def _proj_body(x_ref, gpre_ref, wlat_ref, wkpe_ref, wwide_ref, gq_ref, wuq_ref, gkv_ref, wukv_ref, ra_ref, rs_ref,
               bg_ref, lb_ref, ltri_ref, q_ref, k_ref, v_ref, ga_ref, o_ref, gb_ref, gt_ref, st_ref):
    tm = x_ref.shape[1]
    groups = range(HG_HEADS // 2)

    @pl.when(pl.program_id(1) == 0)
    def _():
        st_ref[...] = jnp.zeros_like(st_ref)

    h = (_rms(x_ref[0]) * gpre_ref[...]).astype(BF16)

    def proj(lo, width):
        if lo + width <= COL_KPE:
            return _dot(h, wlat_ref[:, lo:lo + width])
        if lo == COL_KPE:
            return _dot(h, wkpe_ref[...])
        return _dot(h, wwide_ref[:, lo - COL_GA:lo - COL_GA + width])

    ra = ra_ref[...]
    rs = rs_ref[...]
    lane = lax.broadcasted_iota(jnp.int32, ra.shape, 1)
    takes_upper = lane < MLA_NOPE + MLA_ROPE // 2

    def rope(t):
        partner = jnp.where(takes_upper, pltpu.roll(t, LANES - MLA_ROPE // 2, 1), pltpu.roll(t, MLA_ROPE // 2, 1))
        return t * ra + partner * rs

    held = {}

    def q_down():
        held["cqn"] = (_rms(proj(COL_CQ, Q_LORA)) * gq_ref[...]).astype(BF16)

    def q_up():
        scale = math.log2(math.e) / math.sqrt(MLA_QK)
        qu = _dot(held["cqn"], wuq_ref[...])
        for hh in range(MLA_HEADS):
            q_ref[0, hh] = (rope(qu[:, hh * HEAD_PAD:(hh + 1) * HEAD_PAD]) * scale).T.astype(BF16)

    def kv_down():
        held["ckvn"] = (_rms(proj(COL_CKV, KV_LORA)) * gkv_ref[...]).astype(BF16)
        held["kpe"] = rope(proj(COL_KPE, HEAD_PAD))

    def kv_up():
        kvu = _dot(held["ckvn"], wukv_ref[...])
        for hh in range(MLA_HEADS):
            k_ref[0, hh] = (kvu[:, hh * HEAD_PAD:(hh + 1) * HEAD_PAD] + held["kpe"]).astype(BF16)
        v_ref[0] = kvu[:, MLA_HEADS * HEAD_PAD:].T.astype(BF16)

    def silu_gate(lo, out_ref):
        z = proj(lo, out_ref.shape[-1])
        out_ref[0] = (z * _sigmoid(z)).astype(BF16)

    def merge_gates(c, width=2 * LANES):
        z = proj(COL_MG + c * width, width) + bg_ref[:, c * width:(c + 1) * width]
        gt_ref[0, :, c * width:(c + 1) * width] = _sigmoid(z).astype(BF16)

    pieces = [functools.partial(merge_gates, c) for c in range(D_MODEL // LANES)]
    pieces[1:1] = [q_down]
    pieces[3:3] = [kv_down]
    pieces[5:5] = [q_up]
    pieces[7:7] = [kv_up]
    pieces += [functools.partial(silu_gate, COL_GA, ga_ref), functools.partial(silu_gate, COL_GB, gb_ref)]

    def next_piece():
        if pieces:
            pieces.pop(0)()

    rec = {}
    gates, scores, output = _recurrence_stages(rec, lb_ref[...], ltri_ref[...], o_ref)
    n_chunks = tm // C_HGRN
    rec["hf"] = proj(COL_HF, HG_WIDTH)
    gated = {0: gates(0)}
    rec["hq"] = proj(COL_HQ, HG_WIDTH)
    if n_chunks > 1:
        gated[1] = gates(1)
    rec["hi"] = proj(COL_HI, HG_WIDTH)
    scored = {0: scores(0, gated.pop(0))}
    states = [st_ref[g] for g in groups]
    for c in range(n_chunks):
        next_piece()
        if c + 2 < n_chunks:
            gated[c + 2] = gates(c + 2)
        next_piece()
        if c + 1 < n_chunks:
            scored[c + 1] = scores(c + 1, gated.pop(c + 1))
        states = output(c, scored.pop(c), states)
    for g in groups:
        st_ref[g] = states[g]
    while pieces:
        next_piece()


def _proj_call(x, gpre, win, gq, wuq, gkv, wukv, ra, rs, bg, lb):
    b, s, _ = x.shape
    tm = min(TM_PROJ, s)
    r = np.arange(C_HGRN)
    ltri = jnp.asarray(r[:, None] >= r[None, :], BF16)
    const = lambda bi, si: (0, 0)
    tok = lambda bi, si: (bi, si, 0)

    def full(a):
        return pl.BlockSpec(a.shape, const, pipeline_mode=pl.Buffered(1))

    def tok_out(width, dtype):
        return jax.ShapeDtypeStruct((b, s, width), dtype), pl.BlockSpec((1, tm, width), tok)

    head_shape = jax.ShapeDtypeStruct((b, MLA_HEADS, s, HEAD_PAD), BF16)
    head_spec = pl.BlockSpec((1, MLA_HEADS, tm, HEAD_PAD), lambda bi, si: (bi, 0, si, 0))
    outs = [(jax.ShapeDtypeStruct((b, MLA_HEADS, HEAD_PAD, s), BF16),
             pl.BlockSpec((1, MLA_HEADS, HEAD_PAD, tm), lambda bi, si: (bi, 0, 0, si))),
            (head_shape, head_spec),
            (jax.ShapeDtypeStruct((b, MLA_WIDTH, s), BF16),
             pl.BlockSpec((1, MLA_WIDTH, tm), lambda bi, si: (bi, 0, si))),
            tok_out(MLA_WIDTH, BF16), tok_out(HG_WIDTH, BF16), tok_out(HG_WIDTH, BF16),
            tok_out(2 * D_MODEL, BF16)]
    rope_spec = pl.BlockSpec((tm, HEAD_PAD), lambda bi, si: (si, 0))
    return pl.pallas_call(
        _proj_body,
        grid=(b, s // tm),
        in_specs=[pl.BlockSpec((1, tm, D_MODEL), tok), full(gpre), *(full(w) for w in win), full(gq), full(wuq),
                  full(gkv), full(wukv), rope_spec, rope_spec, full(bg), full(lb), full(ltri)],
        out_specs=[o[1] for o in outs],
        out_shape=[o[0] for o in outs],
        scratch_shapes=[pltpu.VMEM((HG_HEADS // 2, LANES, LANES), F32)],
        compiler_params=pltpu.CompilerParams(dimension_semantics=("parallel", "arbitrary"),
                                             vmem_limit_bytes=VMEM_LIMIT),
        name="proj_hgrn",
    )(x, gpre, *win, gq, wuq, gkv, wukv, ra, rs, bg, lb, ltri)


def _attn_body(qtab_ref, ktab_ref, q_ref, k_ref, vt_ref, o_ref, s_ref, m_ref, acc_ref):
    tq = s_ref.shape[-1]
    n_q = k_ref.shape[2] // tq
    n_off = n_q * (n_q - 1) // 2
    heads = range(HEADS_PER_GROUP)
    key_chunk = lax.broadcasted_iota(jnp.int32, (tq, tq), 0) // CHUNK
    query_chunk = lax.broadcasted_iota(jnp.int32, (tq, tq), 1) // CHUNK
    visible = key_chunk <= query_chunk
    ones = jnp.ones((SUM_ROWS, tq), BF16)

    m_ref[...] = jnp.full(m_ref.shape, -jnp.inf, F32)
    acc_ref[...] = jnp.zeros(acc_ref.shape, F32)

    half = tq // 2
    assert half % CHUNK == 0 and half % LANES == 0
    early, late, everything = slice(0, half), slice(half, tq), slice(0, tq)

    def column_parts(on_diagonal):
        return [(early, early), (late, everything)] if on_diagonal else [(everything, everything)]

    def score(slot, pos, hh, on_diagonal):
        q_off = qtab_ref[pos] * tq
        k_off = ktab_ref[pos] * tq
        part_max = []
        for queries, keys in column_parts(on_diagonal):
            k_rows = pl.ds(pl.multiple_of(k_off + keys.start, half), keys.stop - keys.start)
            q_cols = pl.ds(pl.multiple_of(q_off + queries.start, half), queries.stop - queries.start)
            sc = _dot(k_ref[0, hh, k_rows, :], q_ref[0, hh, :, q_cols])
            if on_diagonal:
                sc = jnp.where(visible[keys, queries], sc, -jnp.inf)
            s_ref[slot, hh, keys, queries] = sc
            part_max.append(jnp.max(sc, axis=0, keepdims=True))
        return jnp.concatenate(part_max, axis=1)

    def absorb(slot, pos, hh, block_max, on_diagonal):
        qi = qtab_ref[pos]
        k_off = ktab_ref[pos] * tq
        m_old = m_ref[qi, hh]
        m_new = jnp.maximum(m_old, block_max)
        m_ref[qi, hh] = m_new
        alpha = jnp.exp2(m_old - m_new)
        for queries, keys in column_parts(on_diagonal):
            p = jnp.exp2(s_ref[slot, hh, keys, queries] - m_new[:, queries]).astype(BF16)
            k_cols = pl.ds(pl.multiple_of(k_off + keys.start, half), keys.stop - keys.start)
            vt = vt_ref[0, hh * MLA_V:(hh + 1) * MLA_V, k_cols]
            vt_ones = jnp.concatenate([vt, ones[:, keys]], axis=0)
            acc_ref[qi, hh, :, queries] = (alpha[:, queries] * acc_ref[qi, hh, :, queries]
                                           + _dot(vt_ones, p))

    def pipeline(first, count, on_diagonal, unroll):
        if count == 0:
            return
        assert unroll % 2 == 0
        n_loop = (count - 1) // unroll

        def several(pos, n, block_max, score_last):
            for u in range(n):
                nxt = []
                for hh in heads:
                    if u + 1 < n or score_last:
                        nxt.append(score((u + 1) % 2, pos + u + 1, hh, on_diagonal))
                    absorb(u % 2, pos + u, hh, block_max[hh], on_diagonal)
                block_max = tuple(nxt)
            return block_max

        first_max = tuple(score(0, first, hh, on_diagonal) for hh in heads)
        block_max = lax.fori_loop(0, n_loop, lambda t, bm: several(first + unroll * t, unroll, bm, True), first_max)
        several(first + unroll * n_loop, count - unroll * n_loop, block_max, False)

    pipeline(0, n_q, True, 2)
    pipeline(n_q, n_off, False, 8)

    def finish(qi, carry):
        out_t = jnp.concatenate([acc_ref[qi, hh, :MLA_V, :] / acc_ref[qi, hh, MLA_V:MLA_V + 1, :] for hh in heads],
                                axis=0)
        o_ref[0, pl.ds(pl.multiple_of(qi * tq, tq), tq), :] = out_t.T.astype(BF16)
        return carry

    lax.fori_loop(0, n_q, finish, 0)


def _attn_call(q, k, vt):
    b, _, s, _ = k.shape
    tq = min(TQ_ATTN, s)
    n_q = s // tq
    off = [(qi, kj) for kj in range(n_q) for qi in range(kj + 1, n_q)]
    order = [(qi, qi) for qi in range(n_q)] + off
    qtab = jnp.asarray([p[0] for p in order], jnp.int32)
    ktab = jnp.asarray([p[1] for p in order], jnp.int32)
    grid_spec = pltpu.PrefetchScalarGridSpec(
        num_scalar_prefetch=2,
        grid=(b, N_GROUPS),
        in_specs=[pl.BlockSpec((1, HEADS_PER_GROUP, HEAD_PAD, s), lambda bi, g, qt, kt: (bi, g, 0, 0)),
                  pl.BlockSpec((1, HEADS_PER_GROUP, s, HEAD_PAD), lambda bi, g, qt, kt: (bi, g, 0, 0)),
                  pl.BlockSpec((1, LANES, s), lambda bi, g, qt, kt: (bi, g, 0))],
        out_specs=pl.BlockSpec((1, s, LANES), lambda bi, g, qt, kt: (bi, 0, g)),
        scratch_shapes=[pltpu.VMEM((2, HEADS_PER_GROUP, tq, tq), F32),
                        pltpu.VMEM((n_q, HEADS_PER_GROUP, 1, tq), F32),
                        pltpu.VMEM((n_q, HEADS_PER_GROUP, MLA_V + SUM_ROWS, tq), F32)])
    return pl.pallas_call(
        _attn_body,
        grid_spec=grid_spec,
        out_shape=jax.ShapeDtypeStruct((b, s, MLA_WIDTH), BF16),
        compiler_params=pltpu.CompilerParams(dimension_semantics=("parallel", "parallel"),
                                             vmem_limit_bytes=VMEM_LIMIT),
        name="attn",
    )(qtab, ktab, q, k, vt)


def _out_body(x_ref, attn_ref, ga_ref, o_ref, gb_ref, gt_ref, wa_ref, wb_ref, wo_ref, gpost_ref, ghg_ref, hsum_ref,
              out_ref):
    tm = x_ref.shape[1]
    parts = [slice(i * tm // OUT_PARTS, (i + 1) * tm // OUT_PARTS) for i in range(OUT_PARTS)]

    def sq_sum(o):
        return _dot((o * o).astype(BF16), hsum_ref[...])

    def merged(r, o, sq):
        ya = _dot(attn_ref[0, r] * ga_ref[0, r], wa_ref[...])
        o = o * lax.rsqrt(sq * (1.0 / HG_DV) + EPS) * ghg_ref[...]
        yb = _dot((o * gb_ref[0, r].astype(F32)).astype(BF16), wb_ref[...])
        m = gt_ref[0, r, :D_MODEL].astype(F32) * ya + gt_ref[0, r, D_MODEL:].astype(F32) * yb
        return m.astype(BF16)

    os_ = [o_ref[0, r].astype(F32) for r in parts]
    sqs = [sq_sum(o) for o in os_]
    ms = [merged(r, o, sq) for r, o, sq in zip(parts, os_, sqs)]
    ys = [_dot(m, wo_ref[...]) for m in ms]
    for r, y in zip(parts, ys):
        out_ref[0, r] = x_ref[0, r] + _rms(y) * gpost_ref[...]


def _out_call(x, attn, ga, o, gb, gt, wa, wb, wo, gpost, ghg):
    b, s, _ = x.shape
    hcol = np.arange(HG_WIDTH) // HG_DV
    hsum = jnp.asarray(hcol[:, None] == hcol[None, :], BF16)
    tm = min(TM_OUT, s)
    tok = lambda bi, si: (bi, si, 0)
    const = lambda bi, si: (0, 0)

    def tspec(a):
        return pl.BlockSpec((1, tm, a.shape[-1]), tok)

    def full(a):
        return pl.BlockSpec(a.shape, const)

    return pl.pallas_call(
        _out_body,
        grid=(b, s // tm),
        in_specs=[tspec(x), tspec(attn), tspec(ga), tspec(o), tspec(gb), tspec(gt),
                  full(wa), full(wb), full(wo), full(gpost), full(ghg), full(hsum)],
        out_specs=tspec(x),
        out_shape=jax.ShapeDtypeStruct(x.shape, x.dtype),
        compiler_params=pltpu.CompilerParams(dimension_semantics=("parallel", "parallel"),
                                             vmem_limit_bytes=VMEM_LIMIT),
        name="merge_out",
    )(x, attn, ga, o, gb, gt, wa, wb, wo, gpost, ghg, hsum)


def _rope_tables(s):
    inv = ROPE_THETA ** (-jnp.arange(0, MLA_ROPE, 2, dtype=F32) / MLA_ROPE)
    ang = jnp.arange(s, dtype=F32)[:, None] * inv[None, :]
    cos, sin = jnp.cos(ang), jnp.sin(ang)
    pad = jnp.zeros((s, HEAD_PAD - MLA_QK), F32)
    ra = jnp.concatenate([jnp.ones((s, MLA_NOPE), F32), cos, cos, pad], axis=1)
    rs = jnp.concatenate([jnp.zeros((s, MLA_NOPE), F32), -sin, sin, pad], axis=1)
    return ra, rs


def _pad_weights(w_in, w_uq, w_ukv):
    kpe_lo = Q_LORA + KV_LORA
    w_lat = w_in[:, :kpe_lo].astype(BF16)
    w_kpe = jnp.pad(w_in[:, kpe_lo:kpe_lo + MLA_ROPE].astype(BF16), ((0, 0), (MLA_NOPE, HEAD_PAD - MLA_QK)))
    w_wide = w_in[:, kpe_lo + MLA_ROPE:].astype(BF16)
    wuq = jnp.pad(w_uq.reshape(Q_LORA, MLA_HEADS, MLA_QK), ((0, 0), (0, 0), (0, HEAD_PAD - MLA_QK)))
    wkv = w_ukv.reshape(KV_LORA, MLA_HEADS, MLA_NOPE + MLA_V)
    wk = jnp.pad(wkv[..., :MLA_NOPE], ((0, 0), (0, 0), (0, HEAD_PAD - MLA_NOPE)))
    wukv = jnp.concatenate([wk.reshape(KV_LORA, MLA_HEADS * HEAD_PAD),
                            wkv[..., MLA_NOPE:].reshape(KV_LORA, MLA_WIDTH)], axis=1)
    return (w_lat, w_kpe, w_wide), wuq.reshape(Q_LORA, MLA_HEADS * HEAD_PAD).astype(BF16), wukv.astype(BF16)


def kernel(x, g_pre, w_in, b_gate, g_q, w_uq, g_kv, w_ukv, lb_logits, g_hgrn, w_branch_a, w_branch_b, w_out,
           g_post):
    assert g_pre.shape[0] == 1, "single-layer block"
    s = x.shape[1]
    win, wuq, wukv = _pad_weights(w_in[0], w_uq[0], w_ukv[0])
    ra, rs = _rope_tables(s)
    lower_bound = jax.nn.softmax(lb_logits.astype(F32), axis=0)[0:1]
    ghg = jnp.tile(g_hgrn[0], HG_HEADS)[None, :]

    q, k, vt, ga, o, gb, gt = _proj_call(x, g_pre, win, g_q, wuq, g_kv, wukv, ra, rs, b_gate, lower_bound)
    attn = _attn_call(q, k, vt)
    return _out_call(x, attn, ga, o, gb, gt, w_branch_a[0].astype(BF16), w_branch_b[0].astype(BF16),
                     w_out[0].astype(BF16), g_post, ghg)
```

```python
import functools
import math

import jax
import jax.numpy as jnp
import numpy as np
from jax import lax
from jax.experimental import pallas as pl
from jax.experimental.pallas import tpu as pltpu

F32 = jnp.float32
BF16 = jnp.bfloat16

D_MODEL = 1024
CHUNK = 64
EPS = 1e-6

MLA_HEADS = 8
MLA_NOPE = 64
MLA_ROPE = 32
MLA_V = 64
MLA_QK = MLA_NOPE + MLA_ROPE
Q_LORA = 768
KV_LORA = 256
ROPE_THETA = 10000.0
MLA_WIDTH = MLA_HEADS * MLA_V

HG_HEADS = 8
HG_DK = 64
HG_DV = 64
HG_WIDTH = HG_HEADS * HG_DV

LANES = 128
HEAD_PAD = LANES
HEADS_PER_GROUP = LANES // MLA_V
N_GROUPS = MLA_HEADS // HEADS_PER_GROUP

COL_CQ = 0
COL_CKV = COL_CQ + Q_LORA
COL_KPE = COL_CKV + KV_LORA
COL_GA = COL_KPE + HEAD_PAD
COL_HQ = COL_GA + MLA_WIDTH
COL_HF = COL_HQ + HG_WIDTH
COL_HI = COL_HF + HG_WIDTH
COL_GB = COL_HI + HG_WIDTH
COL_MG = COL_GB + HG_WIDTH
D_IN_PAD = COL_MG + 2 * D_MODEL

VMEM_LIMIT = 56 * 1024 * 1024

TM_PROJ = 512
TQ_ATTN = 512
SUM_ROWS = 16
UNROLL_DIAGONAL = 2
UNROLL_VISIBLE = 8
C_HGRN = 64
TM_OUT = 1024
OUT_PARTS = 2


def _sigmoid(z):
    return 1.0 / (1.0 + jnp.exp(-z))


def _rms(t):
    return t * lax.rsqrt(jnp.mean(t * t, axis=-1, keepdims=True) + EPS)


def _dot(a, b):
    return jnp.dot(a, b, preferred_element_type=F32)


def _dot_nt(a, b):
    return lax.dot_general(a, b, (((1,), (1,)), ((), ())), preferred_element_type=F32)


def _cumsum_rows(t):
    rows = t.shape[0]
    row = lax.broadcasted_iota(jnp.int32, t.shape, 0)
    step = 1
    while step < rows:
        t = t + jnp.where(row >= step, pltpu.roll(t, step, 0), 0.0)
        step *= 2
    return t


def _recurrence_stages(proj, lb, o_ref):
    c_len = C_HGRN
    groups = range(HG_HEADS // 2)
    low = lax.broadcasted_iota(jnp.int32, (c_len, LANES), 1) < HG_DK
    causal = (lax.broadcasted_iota(jnp.int32, (2 * c_len, c_len), 0) % c_len
              >= lax.broadcasted_iota(jnp.int32, (2 * c_len, c_len), 1))
    same_head = ((lax.broadcasted_iota(jnp.int32, (LANES, LANES), 0) < HG_DV)
                 == (lax.broadcasted_iota(jnp.int32, (LANES, LANES), 1) < HG_DK))

    def stack_heads(t):
        zero = jnp.zeros_like(t)
        return jnp.concatenate([jnp.where(low, t, zero), jnp.where(low, zero, t)], axis=0)

    def gates(c):
        f = lb + (1.0 - lb) * _sigmoid(proj["hf"][c * c_len:(c + 1) * c_len])
        return 1.0 - f, _cumsum_rows(jnp.log2(f))

    def scores(c, gated):
        k_in, cum = gated
        rows = slice(c * c_len, (c + 1) * c_len)
        mid = cum[c_len // 2 - 1:c_len // 2, :]
        tot = cum[c_len - 1:c_len, :]
        q_mid = proj["hq"][rows] * jnp.exp2(cum - mid)
        k_mid = k_in * jnp.exp2(mid - cum)
        q_dec = (q_mid * jnp.exp2(mid)).astype(BF16)
        k_end = (k_mid * jnp.exp2(tot - mid)).astype(BF16)
        q_mid = q_mid.astype(BF16)
        k_mid = k_mid.astype(BF16)
        v = proj["hi"][rows]
        per_group = []
        for g in groups:
            cols = slice(g * LANES, (g + 1) * LANES)
            a = _dot_nt(stack_heads(q_mid[:, cols]), k_mid[:, cols])
            v_t = v[:, cols].T.astype(BF16)
            upd = _dot(v_t, k_end[:, cols])
            per_group.append((a, v_t, upd, stack_heads(q_dec[:, cols])))
        return per_group, jnp.exp2(tot)

    def output(c, scored, states):
        per_group, decay = scored
        rows = slice(c * c_len, (c + 1) * c_len)
        new_states = []
        for g in groups:
            cols = slice(g * LANES, (g + 1) * LANES)
            a, v_t, upd, q_dec = per_group[g]
            a = jnp.where(causal, a, 0.0).astype(BF16)
            lhs = jnp.concatenate([q_dec, a], axis=1)
            rhs_t = jnp.concatenate([states[g].astype(BF16), v_t], axis=1)
            res = _dot_nt(lhs, rhs_t)
            o_ref[0, rows, cols] = jnp.where(low, res[:c_len], res[c_len:]).astype(BF16)
            new_states.append(states[g] * decay[:, cols] + jnp.where(same_head, upd, 0.0))
        return new_states

    return gates, scores, output


def _proj_body(x_ref, gpre_ref, wlat_ref, wkpe_ref, wwide_ref, gq_ref, wuq_ref, gkv_ref, wukv_ref, ra_ref, rs_ref,
               bg_ref, lb_ref, q_ref, k_ref, v_ref, ga_ref, o_ref, gb_ref, gt_ref, st_ref):
    tm = x_ref.shape[1]
    groups = range(HG_HEADS // 2)

    @pl.when(pl.program_id(1) == 0)
    def _():
        st_ref[...] = jnp.zeros_like(st_ref)

    h = (_rms(x_ref[0]) * gpre_ref[...]).astype(BF16)

    def proj(lo, width):
        if lo + width <= COL_KPE:
            return _dot(h, wlat_ref[:, lo:lo + width])
        if lo == COL_KPE:
            return _dot(h, wkpe_ref[...])
        return _dot(h, wwide_ref[:, lo - COL_GA:lo - COL_GA + width])

    ra = ra_ref[...]
    rs = rs_ref[...]
    lane = lax.broadcasted_iota(jnp.int32, ra.shape, 1)
    takes_upper = lane < MLA_NOPE + MLA_ROPE // 2

    def rope(t):
        partner = jnp.where(takes_upper, pltpu.roll(t, LANES - MLA_ROPE // 2, 1), pltpu.roll(t, MLA_ROPE // 2, 1))
        return t * ra + partner * rs

    held = {}

    def q_down():
        held["cqn"] = (_rms(proj(COL_CQ, Q_LORA)) * gq_ref[...]).astype(BF16)

    def q_up():
        scale = math.log2(math.e) / math.sqrt(MLA_QK)
        qu = _dot(held["cqn"], wuq_ref[...])
        for hh in range(MLA_HEADS):
            q_ref[0, hh] = (rope(qu[:, hh * HEAD_PAD:(hh + 1) * HEAD_PAD]) * scale).T.astype(BF16)

    def kv_down():
        held["ckvn"] = (_rms(proj(COL_CKV, KV_LORA)) * gkv_ref[...]).astype(BF16)
        held["kpe"] = rope(proj(COL_KPE, HEAD_PAD))

    def kv_up():
        kvu = _dot(held["ckvn"], wukv_ref[...])
        for hh in range(MLA_HEADS):
            k_ref[0, hh] = (kvu[:, hh * HEAD_PAD:(hh + 1) * HEAD_PAD] + held["kpe"]).astype(BF16)
        v_ref[0] = kvu[:, MLA_HEADS * HEAD_PAD:].T.astype(BF16)

    def silu_gate(lo, out_ref):
        z = proj(lo, out_ref.shape[-1])
        out_ref[0] = (z * _sigmoid(z)).astype(BF16)

    def merge_gates(c, width=2 * LANES):
        z = proj(COL_MG + c * width, width) + bg_ref[:, c * width:(c + 1) * width]
        gt_ref[0, :, c * width:(c + 1) * width] = _sigmoid(z).astype(BF16)

    pieces = [functools.partial(merge_gates, c) for c in range(D_MODEL // LANES)]
    pieces[1:1] = [q_down]
    pieces[3:3] = [kv_down]
    pieces[5:5] = [q_up]
    pieces[7:7] = [kv_up]
    pieces += [functools.partial(silu_gate, COL_GA, ga_ref), functools.partial(silu_gate, COL_GB, gb_ref)]

    def next_piece():
        if pieces:
            pieces.pop(0)()

    rec = {}
    gates, scores, output = _recurrence_stages(rec, lb_ref[...], o_ref)
    n_chunks = tm // C_HGRN
    rec["hf"] = proj(COL_HF, HG_WIDTH)
    gated = {0: gates(0)}
    rec["hq"] = proj(COL_HQ, HG_WIDTH)
    if n_chunks > 1:
        gated[1] = gates(1)
    rec["hi"] = proj(COL_HI, HG_WIDTH)
    scored = {0: scores(0, gated.pop(0))}
    states = [st_ref[g] for g in groups]
    for c in range(n_chunks):
        next_piece()
        if c + 2 < n_chunks:
            gated[c + 2] = gates(c + 2)
        next_piece()
        if c + 1 < n_chunks:
            scored[c + 1] = scores(c + 1, gated.pop(c + 1))
        states = output(c, scored.pop(c), states)
    for g in groups:
        st_ref[g] = states[g]
    while pieces:
        next_piece()


def _proj_call(x, gpre, win, gq, wuq, gkv, wukv, ra, rs, bg, lb):
    b, s, _ = x.shape
    tm = min(TM_PROJ, s)
    const = lambda bi, si: (0, 0)
    tok = lambda bi, si: (bi, si, 0)

    def full(a):
        return pl.BlockSpec(a.shape, const, pipeline_mode=pl.Buffered(1))

    def tok_out(width, dtype):
        return jax.ShapeDtypeStruct((b, s, width), dtype), pl.BlockSpec((1, tm, width), tok)

    head_shape = jax.ShapeDtypeStruct((b, MLA_HEADS, s, HEAD_PAD), BF16)
    head_spec = pl.BlockSpec((1, MLA_HEADS, tm, HEAD_PAD), lambda bi, si: (bi, 0, si, 0))
    outs = [(jax.ShapeDtypeStruct((b, MLA_HEADS, HEAD_PAD, s), BF16),
             pl.BlockSpec((1, MLA_HEADS, HEAD_PAD, tm), lambda bi, si: (bi, 0, 0, si))),
            (head_shape, head_spec),
            (jax.ShapeDtypeStruct((b, MLA_WIDTH, s), BF16),
             pl.BlockSpec((1, MLA_WIDTH, tm), lambda bi, si: (bi, 0, si))),
            tok_out(MLA_WIDTH, BF16), tok_out(HG_WIDTH, BF16), tok_out(HG_WIDTH, BF16),
            tok_out(2 * D_MODEL, BF16)]
    rope_spec = pl.BlockSpec((tm, HEAD_PAD), lambda bi, si: (si, 0))
    return pl.pallas_call(
        _proj_body,
        grid=(b, s // tm),
        in_specs=[pl.BlockSpec((1, tm, D_MODEL), tok), full(gpre), *(full(w) for w in win), full(gq), full(wuq),
                  full(gkv), full(wukv), rope_spec, rope_spec, full(bg), full(lb)],
        out_specs=[o[1] for o in outs],
        out_shape=[o[0] for o in outs],
        scratch_shapes=[pltpu.VMEM((HG_HEADS // 2, LANES, LANES), F32)],
        compiler_params=pltpu.CompilerParams(dimension_semantics=("parallel", "arbitrary"),
                                             vmem_limit_bytes=VMEM_LIMIT),
        name="proj_hgrn",
    )(x, gpre, *win, gq, wuq, gkv, wukv, ra, rs, bg, lb)


def _attn_body(qtab_ref, ktab_ref, q_ref, k_ref, vt_ref, o_ref, s_ref, m_ref, acc_ref):
    tq = s_ref.shape[-1]
    n_q = k_ref.shape[2] // tq
    n_off = n_q * (n_q - 1) // 2
    heads = range(HEADS_PER_GROUP)
    key_chunk = lax.broadcasted_iota(jnp.int32, (tq, tq), 0) // CHUNK
    query_chunk = lax.broadcasted_iota(jnp.int32, (tq, tq), 1) // CHUNK
    visible = key_chunk <= query_chunk
    ones = jnp.ones((SUM_ROWS, tq), BF16)

    m_ref[...] = jnp.full(m_ref.shape, -jnp.inf, F32)
    acc_ref[...] = jnp.zeros(acc_ref.shape, F32)

    half = tq // 2
    assert half % CHUNK == 0 and half % LANES == 0
    early, late, everything = slice(0, half), slice(half, tq), slice(0, tq)

    def column_parts(on_diagonal):
        return [(early, early), (late, everything)] if on_diagonal else [(everything, everything)]

    def score(slot, pos, hh, on_diagonal):
        q_off = qtab_ref[pos] * tq
        k_off = ktab_ref[pos] * tq
        part_max = []
        for queries, keys in column_parts(on_diagonal):
            k_rows = pl.ds(pl.multiple_of(k_off + keys.start, half), keys.stop - keys.start)
            q_cols = pl.ds(pl.multiple_of(q_off + queries.start, half), queries.stop - queries.start)
            sc = _dot(k_ref[0, hh, k_rows, :], q_ref[0, hh, :, q_cols])
            if on_diagonal:
                sc = jnp.where(visible[keys, queries], sc, -jnp.inf)
            s_ref[slot, hh, keys, queries] = sc
            part_max.append(jnp.max(sc, axis=0, keepdims=True))
        return jnp.concatenate(part_max, axis=1)

    def absorb(slot, pos, hh, block_max, on_diagonal):
        qi = qtab_ref[pos]
        k_off = ktab_ref[pos] * tq
        m_old = m_ref[qi, hh]
        m_new = jnp.maximum(m_old, block_max)
        m_ref[qi, hh] = m_new
        alpha = jnp.exp2(m_old - m_new)
        for queries, keys in column_parts(on_diagonal):
            p = jnp.exp2(s_ref[slot, hh, keys, queries] - m_new[:, queries]).astype(BF16)
            k_cols = pl.ds(pl.multiple_of(k_off + keys.start, half), keys.stop - keys.start)
            vt = vt_ref[0, hh * MLA_V:(hh + 1) * MLA_V, k_cols]
            vt_ones = jnp.concatenate([vt, ones[:, keys]], axis=0)
            acc_ref[qi, hh, :, queries] = (alpha[:, queries] * acc_ref[qi, hh, :, queries]
                                           + _dot(vt_ones, p))

    def pipeline(first, count, on_diagonal, unroll):
        if count == 0:
            return
        assert unroll % 2 == 0
        n_loop = (count - 1) // unroll

        def several(pos, n, block_max, score_last):
            for u in range(n):
                nxt = []
                for hh in heads:
                    if u + 1 < n or score_last:
                        nxt.append(score((u + 1) % 2, pos + u + 1, hh, on_diagonal))
                    absorb(u % 2, pos + u, hh, block_max[hh], on_diagonal)
                block_max = tuple(nxt)
            return block_max

        first_max = tuple(score(0, first, hh, on_diagonal) for hh in heads)
        block_max = lax.fori_loop(0, n_loop, lambda t, bm: several(first + unroll * t, unroll, bm, True), first_max)
        several(first + unroll * n_loop, count - unroll * n_loop, block_max, False)

    pipeline(0, n_q, True, UNROLL_DIAGONAL)
    pipeline(n_q, n_off, False, UNROLL_VISIBLE)

    def finish(qi, carry):
        out_t = jnp.concatenate([acc_ref[qi, hh, :MLA_V, :] / acc_ref[qi, hh, MLA_V:MLA_V + 1, :] for hh in heads],
                                axis=0)
        o_ref[0, pl.ds(pl.multiple_of(qi * tq, tq), tq), :] = out_t.T.astype(BF16)
        return carry

    lax.fori_loop(0, n_q, finish, 0)


def _attn_call(q, k, vt):
    b, _, s, _ = k.shape
    tq = min(TQ_ATTN, s)
    n_q = s // tq
    off = [(qi, kj) for kj in range(n_q) for qi in range(kj + 1, n_q)]
    order = [(qi, qi) for qi in range(n_q)] + off
    qtab = jnp.asarray([p[0] for p in order], jnp.int32)
    ktab = jnp.asarray([p[1] for p in order], jnp.int32)
    grid_spec = pltpu.PrefetchScalarGridSpec(
        num_scalar_prefetch=2,
        grid=(b, N_GROUPS),
        in_specs=[pl.BlockSpec((1, HEADS_PER_GROUP, HEAD_PAD, s), lambda bi, g, qt, kt: (bi, g, 0, 0)),
                  pl.BlockSpec((1, HEADS_PER_GROUP, s, HEAD_PAD), lambda bi, g, qt, kt: (bi, g, 0, 0)),
                  pl.BlockSpec((1, LANES, s), lambda bi, g, qt, kt: (bi, g, 0))],
        out_specs=pl.BlockSpec((1, s, LANES), lambda bi, g, qt, kt: (bi, 0, g)),
        scratch_shapes=[pltpu.VMEM((2, HEADS_PER_GROUP, tq, tq), F32),
                        pltpu.VMEM((n_q, HEADS_PER_GROUP, 1, tq), F32),
                        pltpu.VMEM((n_q, HEADS_PER_GROUP, MLA_V + SUM_ROWS, tq), F32)])
    return pl.pallas_call(
        _attn_body,
        grid_spec=grid_spec,
        out_shape=jax.ShapeDtypeStruct((b, s, MLA_WIDTH), BF16),
        compiler_params=pltpu.CompilerParams(dimension_semantics=("parallel", "parallel"),
                                             vmem_limit_bytes=VMEM_LIMIT),
        name="attn",
    )(qtab, ktab, q, k, vt)


def _out_body(x_ref, attn_ref, ga_ref, o_ref, gb_ref, gt_ref, wa_ref, wb_ref, wo_ref, gpost_ref, ghg_ref, hsum_ref,
              out_ref):
    tm = x_ref.shape[1]
    parts = [slice(i * tm // OUT_PARTS, (i + 1) * tm // OUT_PARTS) for i in range(OUT_PARTS)]

    def sq_sum(o):
        return _dot((o * o).astype(BF16), hsum_ref[...])

    def merged(r, o, sq):
        ya = _dot(attn_ref[0, r] * ga_ref[0, r], wa_ref[...])
        o = o * lax.rsqrt(sq * (1.0 / HG_DV) + EPS) * ghg_ref[...]
        yb = _dot((o * gb_ref[0, r].astype(F32)).astype(BF16), wb_ref[...])
        m = gt_ref[0, r, :D_MODEL].astype(F32) * ya + gt_ref[0, r, D_MODEL:].astype(F32) * yb
        return m.astype(BF16)

    os_ = [o_ref[0, r].astype(F32) for r in parts]
    sqs = [sq_sum(o) for o in os_]
    ms = [merged(r, o, sq) for r, o, sq in zip(parts, os_, sqs)]
    ys = [_dot(m, wo_ref[...]) for m in ms]
    for r, y in zip(parts, ys):
        out_ref[0, r] = x_ref[0, r] + _rms(y) * gpost_ref[...]


def _out_call(x, attn, ga, o, gb, gt, wa, wb, wo, gpost, ghg):
    b, s, _ = x.shape
    hcol = np.arange(HG_WIDTH) // HG_DV
    hsum = jnp.asarray(hcol[:, None] == hcol[None, :], BF16)
    tm = min(TM_OUT, s)
    tok = lambda bi, si: (bi, si, 0)
    const = lambda bi, si: (0, 0)

    def tspec(a):
        return pl.BlockSpec((1, tm, a.shape[-1]), tok)

    def full(a):
        return pl.BlockSpec(a.shape, const)

    return pl.pallas_call(
        _out_body,
        grid=(b, s // tm),
        in_specs=[tspec(x), tspec(attn), tspec(ga), tspec(o), tspec(gb), tspec(gt),
                  full(wa), full(wb), full(wo), full(gpost), full(ghg), full(hsum)],
        out_specs=tspec(x),
        out_shape=jax.ShapeDtypeStruct(x.shape, x.dtype),
        compiler_params=pltpu.CompilerParams(dimension_semantics=("parallel", "parallel"),
                                             vmem_limit_bytes=VMEM_LIMIT),
        name="merge_out",
    )(x, attn, ga, o, gb, gt, wa, wb, wo, gpost, ghg, hsum)


def _rope_tables(s):
    inv = ROPE_THETA ** (-jnp.arange(0, MLA_ROPE, 2, dtype=F32) / MLA_ROPE)
    ang = jnp.arange(s, dtype=F32)[:, None] * inv[None, :]
    cos, sin = jnp.cos(ang), jnp.sin(ang)
    pad = jnp.zeros((s, HEAD_PAD - MLA_QK), F32)
    ra = jnp.concatenate([jnp.ones((s, MLA_NOPE), F32), cos, cos, pad], axis=1)
    rs = jnp.concatenate([jnp.zeros((s, MLA_NOPE), F32), -sin, sin, pad], axis=1)
    return ra, rs


def _pad_weights(w_in, w_uq, w_ukv):
    kpe_lo = Q_LORA + KV_LORA
    w_lat = w_in[:, :kpe_lo].astype(BF16)
    w_kpe = jnp.pad(w_in[:, kpe_lo:kpe_lo + MLA_ROPE].astype(BF16), ((0, 0), (MLA_NOPE, HEAD_PAD - MLA_QK)))
    w_wide = w_in[:, kpe_lo + MLA_ROPE:].astype(BF16)
    wuq = jnp.pad(w_uq.reshape(Q_LORA, MLA_HEADS, MLA_QK), ((0, 0), (0, 0), (0, HEAD_PAD - MLA_QK)))
    wkv = w_ukv.reshape(KV_LORA, MLA_HEADS, MLA_NOPE + MLA_V)
    wk = jnp.pad(wkv[..., :MLA_NOPE], ((0, 0), (0, 0), (0, HEAD_PAD - MLA_NOPE)))
    wukv = jnp.concatenate([wk.reshape(KV_LORA, MLA_HEADS * HEAD_PAD),
                            wkv[..., MLA_NOPE:].reshape(KV_LORA, MLA_WIDTH)], axis=1)
    return (w_lat, w_kpe, w_wide), wuq.reshape(Q_LORA, MLA_HEADS * HEAD_PAD).astype(BF16), wukv.astype(BF16)


def kernel(x, g_pre, w_in, b_gate, g_q, w_uq, g_kv, w_ukv, lb_logits, g_hgrn, w_branch_a, w_branch_b, w_out,
           g_post):
    assert g_pre.shape[0] == 1, "single-layer block"
    s = x.shape[1]
    win, wuq, wukv = _pad_weights(w_in[0], w_uq[0], w_ukv[0])
    ra, rs = _rope_tables(s)
    lower_bound = jax.nn.softmax(lb_logits.astype(F32), axis=0)[0:1]
    ghg = jnp.tile(g_hgrn[0], HG_HEADS)[None, :]

    q, k, vt, ga, o, gb, gt = _proj_call(x, g_pre, win, g_q, wuq, g_kv, wukv, ra, rs, b_gate, lower_bound)
    attn = _attn_call(q, k, vt)
    return _out_call(x, attn, ga, o, gb, gt, w_branch_a[0].astype(BF16), w_branch_b[0].astype(BF16),
                     w_out[0].astype(BF16), g_post, ghg)
```

```python
import functools
import math

import jax
import jax.numpy as jnp
import numpy as np
from jax import lax
from jax.experimental import pallas as pl
from jax.experimental.pallas import tpu as pltpu

F32 = jnp.float32
BF16 = jnp.bfloat16

D_MODEL = 1024
CHUNK = 64
EPS = 1e-6

MLA_HEADS = 8
MLA_NOPE = 64
MLA_ROPE = 32
MLA_V = 64
MLA_QK = MLA_NOPE + MLA_ROPE
Q_LORA = 768
KV_LORA = 256
ROPE_THETA = 10000.0
MLA_WIDTH = MLA_HEADS * MLA_V

HG_HEADS = 8
HG_DK = 64
HG_DV = 64
HG_WIDTH = HG_HEADS * HG_DV

LANES = 128
HEAD_PAD = LANES
HEADS_PER_GROUP = LANES // MLA_V
N_GROUPS = MLA_HEADS // HEADS_PER_GROUP

COL_CQ = 0
COL_CKV = COL_CQ + Q_LORA
COL_KPE = COL_CKV + KV_LORA
COL_GA = COL_KPE + HEAD_PAD
COL_HQ = COL_GA + MLA_WIDTH
COL_HF = COL_HQ + HG_WIDTH
COL_HI = COL_HF + HG_WIDTH
COL_GB = COL_HI + HG_WIDTH
COL_MG = COL_GB + HG_WIDTH
D_IN_PAD = COL_MG + 2 * D_MODEL

VMEM_LIMIT = 56 * 1024 * 1024

TM_PROJ = 512
TQ_ATTN = 512
SUM_ROWS = 16
UNROLL_DIAGONAL = 2
UNROLL_VISIBLE = 8
C_HGRN = 64
TM_OUT = 1024
OUT_PARTS = 2
W_RELAYOUT_COLS = 512


def _sigmoid(z):
    return 1.0 / (1.0 + jnp.exp(-z))


def _rms(t):
    return t * lax.rsqrt(jnp.mean(t * t, axis=-1, keepdims=True) + EPS)


def _dot(a, b):
    return jnp.dot(a, b, preferred_element_type=F32)


def _dot_nt(a, b):
    return lax.dot_general(a, b, (((1,), (1,)), ((), ())), preferred_element_type=F32)


def _cumsum_rows(t):
    rows = t.shape[0]
    row = lax.broadcasted_iota(jnp.int32, t.shape, 0)
    step = 1
    while step < rows:
        t = t + jnp.where(row >= step, pltpu.roll(t, step, 0), 0.0)
        step *= 2
    return t


def _recurrence_stages(proj, lb, o_ref):
    c_len = C_HGRN
    groups = range(HG_HEADS // 2)
    low = lax.broadcasted_iota(jnp.int32, (c_len, LANES), 1) < HG_DK
    causal = (lax.broadcasted_iota(jnp.int32, (2 * c_len, c_len), 0) % c_len
              >= lax.broadcasted_iota(jnp.int32, (2 * c_len, c_len), 1))
    same_head = ((lax.broadcasted_iota(jnp.int32, (LANES, LANES), 0) < HG_DV)
                 == (lax.broadcasted_iota(jnp.int32, (LANES, LANES), 1) < HG_DK))

    def stack_heads(t):
        zero = jnp.zeros_like(t)
        return jnp.concatenate([jnp.where(low, t, zero), jnp.where(low, zero, t)], axis=0)

    def gates(c):
        f = lb + (1.0 - lb) * _sigmoid(proj["hf"][c * c_len:(c + 1) * c_len])
        return 1.0 - f, _cumsum_rows(jnp.log2(f))

    def scores(c, gated):
        k_in, cum = gated
        rows = slice(c * c_len, (c + 1) * c_len)
        mid = cum[c_len // 2 - 1:c_len // 2, :]
        tot = cum[c_len - 1:c_len, :]
        q_mid = proj["hq"][rows] * jnp.exp2(cum - mid)
        k_mid = k_in * jnp.exp2(mid - cum)
        q_dec = (q_mid * jnp.exp2(mid)).astype(BF16)
        k_end = (k_mid * jnp.exp2(tot - mid)).astype(BF16)
        q_mid = q_mid.astype(BF16)
        k_mid = k_mid.astype(BF16)
        v = proj["hi"][rows]
        per_group = []
        for g in groups:
            cols = slice(g * LANES, (g + 1) * LANES)
            a = _dot_nt(stack_heads(q_mid[:, cols]), k_mid[:, cols])
            v_t = v[:, cols].T.astype(BF16)
            upd = _dot(v_t, k_end[:, cols])
            per_group.append((a, v_t, upd, stack_heads(q_dec[:, cols])))
        return per_group, jnp.exp2(tot)

    def output(c, scored, states):
        per_group, decay = scored
        rows = slice(c * c_len, (c + 1) * c_len)
        new_states = []
        for g in groups:
            cols = slice(g * LANES, (g + 1) * LANES)
            a, v_t, upd, q_dec = per_group[g]
            a = jnp.where(causal, a, 0.0).astype(BF16)
            lhs = jnp.concatenate([q_dec, a], axis=1)
            rhs_t = jnp.concatenate([states[g].astype(BF16), v_t], axis=1)
            res = _dot_nt(lhs, rhs_t)
            o_ref[0, rows, cols] = jnp.where(low, res[:c_len], res[c_len:]).astype(BF16)
            new_states.append(states[g] * decay[:, cols] + jnp.where(same_head, upd, 0.0))
        return new_states

    return gates, scores, output


def _proj_body(x_ref, gpre_ref, wcat_ref, wkpe_ref, gq_ref, wuq_ref, gkv_ref, wukv_ref, ra_ref, rs_ref,
               bg_ref, lb_ref, q_ref, k_ref, v_ref, ga_ref, o_ref, gb_ref, gt_ref, st_ref):
    tm = x_ref.shape[1]
    groups = range(HG_HEADS // 2)

    @pl.when(pl.program_id(1) == 0)
    def _():
        st_ref[...] = jnp.zeros_like(st_ref)

    h = (_rms(x_ref[0]) * gpre_ref[...]).astype(BF16)

    def proj(lo, width):
        if lo == COL_KPE:
            return _dot(h, wkpe_ref[...])
        if lo > COL_KPE:
            lo -= HEAD_PAD
        return _dot(h, wcat_ref[:, lo:lo + width])

    ra = ra_ref[...]
    rs = rs_ref[...]
    lane = lax.broadcasted_iota(jnp.int32, ra.shape, 1)
    takes_upper = lane < MLA_NOPE + MLA_ROPE // 2

    def rope(t):
        partner = jnp.where(takes_upper, pltpu.roll(t, LANES - MLA_ROPE // 2, 1), pltpu.roll(t, MLA_ROPE // 2, 1))
        return t * ra + partner * rs

    held = {}

    def q_down():
        held["cqn"] = (_rms(proj(COL_CQ, Q_LORA)) * gq_ref[...]).astype(BF16)

    def q_up():
        scale = math.log2(math.e) / math.sqrt(MLA_QK)
        qu = _dot(held["cqn"], wuq_ref[...])
        for hh in range(MLA_HEADS):
            q_ref[0, hh] = (rope(qu[:, hh * HEAD_PAD:(hh + 1) * HEAD_PAD]) * scale).T.astype(BF16)

    def kv_down():
        held["ckvn"] = (_rms(proj(COL_CKV, KV_LORA)) * gkv_ref[...]).astype(BF16)
        held["kpe"] = rope(proj(COL_KPE, HEAD_PAD))

    def kv_up():
        kvu = _dot(held["ckvn"], wukv_ref[...])
        for hh in range(MLA_HEADS):
            k_ref[0, hh] = (kvu[:, hh * HEAD_PAD:(hh + 1) * HEAD_PAD] + held["kpe"]).astype(BF16)
        v_ref[0] = kvu[:, MLA_HEADS * HEAD_PAD:].T.astype(BF16)

    def silu_gate(lo, out_ref):
        z = proj(lo, out_ref.shape[-1])
        out_ref[0] = (z * _sigmoid(z)).astype(BF16)

    def merge_gates(c, width=2 * LANES):
        z = proj(COL_MG + c * width, width) + bg_ref[:, c * width:(c + 1) * width]
        gt_ref[0, :, c * width:(c + 1) * width] = _sigmoid(z).astype(BF16)

    pieces = [functools.partial(merge_gates, c) for c in range(D_MODEL // LANES)]
    pieces[1:1] = [q_down]
    pieces[3:3] = [kv_down]
    pieces[5:5] = [q_up]
    pieces[7:7] = [kv_up]
    pieces += [functools.partial(silu_gate, COL_GA, ga_ref), functools.partial(silu_gate, COL_GB, gb_ref)]

    def next_piece():
        if pieces:
            pieces.pop(0)()

    rec = {}
    gates, scores, output = _recurrence_stages(rec, lb_ref[...], o_ref)
    n_chunks = tm // C_HGRN
    rec["hf"] = proj(COL_HF, HG_WIDTH)
    gated = {0: gates(0)}
    rec["hq"] = proj(COL_HQ, HG_WIDTH)
    if n_chunks > 1:
        gated[1] = gates(1)
    rec["hi"] = proj(COL_HI, HG_WIDTH)
    scored = {0: scores(0, gated.pop(0))}
    states = [st_ref[g] for g in groups]
    for c in range(n_chunks):
        next_piece()
        if c + 2 < n_chunks:
            gated[c + 2] = gates(c + 2)
        next_piece()
        if c + 1 < n_chunks:
            scored[c + 1] = scores(c + 1, gated.pop(c + 1))
        states = output(c, scored.pop(c), states)
    for g in groups:
        st_ref[g] = states[g]
    while pieces:
        next_piece()


def _proj_call(x, gpre, win, gq, wuq, gkv, wukv, ra, rs, bg, lb):
    b, s, _ = x.shape
    tm = min(TM_PROJ, s)
    const = lambda bi, si: (0, 0)
    tok = lambda bi, si: (bi, si, 0)

    def full(a):
        return pl.BlockSpec(a.shape, const, pipeline_mode=pl.Buffered(1))

    def tok_out(width, dtype):
        return jax.ShapeDtypeStruct((b, s, width), dtype), pl.BlockSpec((1, tm, width), tok)

    head_shape = jax.ShapeDtypeStruct((b, MLA_HEADS, s, HEAD_PAD), BF16)
    head_spec = pl.BlockSpec((1, MLA_HEADS, tm, HEAD_PAD), lambda bi, si: (bi, 0, si, 0))
    outs = [(jax.ShapeDtypeStruct((b, MLA_HEADS, HEAD_PAD, s), BF16),
             pl.BlockSpec((1, MLA_HEADS, HEAD_PAD, tm), lambda bi, si: (bi, 0, 0, si))),
            (head_shape, head_spec),
            (jax.ShapeDtypeStruct((b, MLA_WIDTH, s), BF16),
             pl.BlockSpec((1, MLA_WIDTH, tm), lambda bi, si: (bi, 0, si))),
            tok_out(MLA_WIDTH, BF16), tok_out(HG_WIDTH, BF16), tok_out(HG_WIDTH, BF16),
            tok_out(2 * D_MODEL, BF16)]
    rope_spec = pl.BlockSpec((tm, HEAD_PAD), lambda bi, si: (si, 0))
    return pl.pallas_call(
        _proj_body,
        grid=(b, s // tm),
        in_specs=[pl.BlockSpec((1, tm, D_MODEL), tok), full(gpre), *(full(w) for w in win), full(gq), full(wuq),
                  full(gkv), full(wukv), rope_spec, rope_spec, full(bg), full(lb)],
        out_specs=[o[1] for o in outs],
        out_shape=[o[0] for o in outs],
        scratch_shapes=[pltpu.VMEM((HG_HEADS // 2, LANES, LANES), F32)],
        compiler_params=pltpu.CompilerParams(dimension_semantics=("parallel", "arbitrary"),
                                             vmem_limit_bytes=VMEM_LIMIT),
        name="proj_hgrn",
    )(x, gpre, *win, gq, wuq, gkv, wukv, ra, rs, bg, lb)


def _attn_body(qtab_ref, ktab_ref, q_ref, k_ref, vt_ref, o_ref, s_ref, m_ref, acc_ref):
    tq = s_ref.shape[-1]
    n_q = k_ref.shape[2] // tq
    n_off = n_q * (n_q - 1) // 2
    heads = range(HEADS_PER_GROUP)
    key_chunk = lax.broadcasted_iota(jnp.int32, (tq, tq), 0) // CHUNK
    query_chunk = lax.broadcasted_iota(jnp.int32, (tq, tq), 1) // CHUNK
    visible = key_chunk <= query_chunk
    ones = jnp.ones((SUM_ROWS, tq), BF16)

    m_ref[...] = jnp.full(m_ref.shape, -jnp.inf, F32)
    acc_ref[...] = jnp.zeros(acc_ref.shape, F32)

    half = tq // 2
    assert half % CHUNK == 0 and half % LANES == 0
    early, late, everything = slice(0, half), slice(half, tq), slice(0, tq)

    def column_parts(on_diagonal):
        return [(early, early), (late, everything)] if on_diagonal else [(everything, everything)]

    def score(slot, pos, hh, on_diagonal):
        q_off = qtab_ref[pos] * tq
        k_off = ktab_ref[pos] * tq
        part_max = []
        for queries, keys in column_parts(on_diagonal):
            k_rows = pl.ds(pl.multiple_of(k_off + keys.start, half), keys.stop - keys.start)
            q_cols = pl.ds(pl.multiple_of(q_off + queries.start, half), queries.stop - queries.start)
            sc = _dot(k_ref[0, hh, k_rows, :], q_ref[0, hh, :, q_cols])
            if on_diagonal:
                sc = jnp.where(visible[keys, queries], sc, -jnp.inf)
            s_ref[slot, hh, keys, queries] = sc
            part_max.append(jnp.max(sc, axis=0, keepdims=True))
        return jnp.concatenate(part_max, axis=1)

    def absorb(slot, pos, hh, block_max, on_diagonal):
        qi = qtab_ref[pos]
        k_off = ktab_ref[pos] * tq
        m_old = m_ref[qi, hh]
        m_new = jnp.maximum(m_old, block_max)
        m_ref[qi, hh] = m_new
        alpha = jnp.exp2(m_old - m_new)
        for queries, keys in column_parts(on_diagonal):
            p = jnp.exp2(s_ref[slot, hh, keys, queries] - m_new[:, queries]).astype(BF16)
            k_cols = pl.ds(pl.multiple_of(k_off + keys.start, half), keys.stop - keys.start)
            vt = vt_ref[0, hh * MLA_V:(hh + 1) * MLA_V, k_cols]
            vt_ones = jnp.concatenate([vt, ones[:, keys]], axis=0)
            acc_ref[qi, hh, :, queries] = (alpha[:, queries] * acc_ref[qi, hh, :, queries]
                                           + _dot(vt_ones, p))

    def pipeline(first, count, on_diagonal, unroll):
        if count == 0:
            return
        assert unroll % 2 == 0
        n_loop = (count - 1) // unroll

        def several(pos, n, block_max, score_last):
            for u in range(n):
                nxt = []
                for hh in heads:
                    if u + 1 < n or score_last:
                        nxt.append(score((u + 1) % 2, pos + u + 1, hh, on_diagonal))
                    absorb(u % 2, pos + u, hh, block_max[hh], on_diagonal)
                block_max = tuple(nxt)
            return block_max

        first_max = tuple(score(0, first, hh, on_diagonal) for hh in heads)
        block_max = lax.fori_loop(0, n_loop, lambda t, bm: several(first + unroll * t, unroll, bm, True), first_max)
        several(first + unroll * n_loop, count - unroll * n_loop, block_max, False)

    pipeline(0, n_q, True, UNROLL_DIAGONAL)
    pipeline(n_q, n_off, False, UNROLL_VISIBLE)

    def finish(qi, carry):
        out_t = jnp.concatenate([acc_ref[qi, hh, :MLA_V, :] / acc_ref[qi, hh, MLA_V:MLA_V + 1, :] for hh in heads],
                                axis=0)
        o_ref[0, pl.ds(pl.multiple_of(qi * tq, tq), tq), :] = out_t.T.astype(BF16)
        return carry

    lax.fori_loop(0, n_q, finish, 0)


def _attn_call(q, k, vt):
    b, _, s, _ = k.shape
    tq = min(TQ_ATTN, s)
    n_q = s // tq
    off = [(qi, kj) for kj in range(n_q) for qi in range(kj + 1, n_q)]
    order = [(qi, qi) for qi in range(n_q)] + off
    qtab = jnp.asarray([p[0] for p in order], jnp.int32)
    ktab = jnp.asarray([p[1] for p in order], jnp.int32)
    grid_spec = pltpu.PrefetchScalarGridSpec(
        num_scalar_prefetch=2,
        grid=(b, N_GROUPS),
        in_specs=[pl.BlockSpec((1, HEADS_PER_GROUP, HEAD_PAD, s), lambda bi, g, qt, kt: (bi, g, 0, 0)),
                  pl.BlockSpec((1, HEADS_PER_GROUP, s, HEAD_PAD), lambda bi, g, qt, kt: (bi, g, 0, 0)),
                  pl.BlockSpec((1, LANES, s), lambda bi, g, qt, kt: (bi, g, 0))],
        out_specs=pl.BlockSpec((1, s, LANES), lambda bi, g, qt, kt: (bi, 0, g)),
        scratch_shapes=[pltpu.VMEM((2, HEADS_PER_GROUP, tq, tq), F32),
                        pltpu.VMEM((n_q, HEADS_PER_GROUP, 1, tq), F32),
                        pltpu.VMEM((n_q, HEADS_PER_GROUP, MLA_V + SUM_ROWS, tq), F32)])
    return pl.pallas_call(
        _attn_body,
        grid_spec=grid_spec,
        out_shape=jax.ShapeDtypeStruct((b, s, MLA_WIDTH), BF16),
        compiler_params=pltpu.CompilerParams(dimension_semantics=("parallel", "parallel"),
                                             vmem_limit_bytes=VMEM_LIMIT),
        name="attn",
    )(qtab, ktab, q, k, vt)


def _out_body(x_ref, attn_ref, ga_ref, o_ref, gb_ref, gt_ref, wa_ref, wb_ref, wo_ref, gpost_ref, ghg_ref, hsum_ref,
              out_ref):
    tm = x_ref.shape[1]
    parts = [slice(i * tm // OUT_PARTS, (i + 1) * tm // OUT_PARTS) for i in range(OUT_PARTS)]

    def sq_sum(o):
        return _dot((o * o).astype(BF16), hsum_ref[...])

    def merged(r, o, sq):
        ya = _dot(attn_ref[0, r] * ga_ref[0, r], wa_ref[...])
        o = o * lax.rsqrt(sq * (1.0 / HG_DV) + EPS) * ghg_ref[...]
        yb = _dot((o * gb_ref[0, r].astype(F32)).astype(BF16), wb_ref[...])
        m = gt_ref[0, r, :D_MODEL].astype(F32) * ya + gt_ref[0, r, D_MODEL:].astype(F32) * yb
        return m.astype(BF16)

    os_ = [o_ref[0, r].astype(F32) for r in parts]
    sqs = [sq_sum(o) for o in os_]
    ms = [merged(r, o, sq) for r, o, sq in zip(parts, os_, sqs)]
    ys = [_dot(m, wo_ref[...]) for m in ms]
    for r, y in zip(parts, ys):
        out_ref[0, r] = x_ref[0, r] + _rms(y) * gpost_ref[...]


def _out_call(x, attn, ga, o, gb, gt, wa, wb, wo, gpost, ghg):
    b, s, _ = x.shape
    hcol = np.arange(HG_WIDTH) // HG_DV
    hsum = jnp.asarray(hcol[:, None] == hcol[None, :], BF16)
    tm = min(TM_OUT, s)
    tok = lambda bi, si: (bi, si, 0)
    const = lambda bi, si: (0, 0)

    def tspec(a):
        return pl.BlockSpec((1, tm, a.shape[-1]), tok)

    def full(a):
        return pl.BlockSpec(a.shape, const)

    return pl.pallas_call(
        _out_body,
        grid=(b, s // tm),
        in_specs=[tspec(x), tspec(attn), tspec(ga), tspec(o), tspec(gb), tspec(gt),
                  full(wa), full(wb), full(wo), full(gpost), full(ghg), full(hsum)],
        out_specs=tspec(x),
        out_shape=jax.ShapeDtypeStruct(x.shape, x.dtype),
        compiler_params=pltpu.CompilerParams(dimension_semantics=("parallel", "parallel"),
                                             vmem_limit_bytes=VMEM_LIMIT),
        name="merge_out",
    )(x, attn, ga, o, gb, gt, wa, wb, wo, gpost, ghg, hsum)


def _rope_tables(s):
    inv = ROPE_THETA ** (-np.arange(0, MLA_ROPE, 2, dtype=np.float64) / MLA_ROPE)
    ang = np.arange(s, dtype=np.float64)[:, None] * inv[None, :]
    cos, sin = np.cos(ang), np.sin(ang)
    pad = np.zeros((s, HEAD_PAD - MLA_QK))
    ra = np.concatenate([np.ones((s, MLA_NOPE)), cos, cos, pad], axis=1)
    rs = np.concatenate([np.zeros((s, MLA_NOPE)), -sin, sin, pad], axis=1)
    return jnp.asarray(ra, F32), jnp.asarray(rs, F32)


def _relayout_body(wt_ref, kpe_src_ref, out_ref, kpe_ref):
    out_ref[...] = wt_ref[...].T.astype(BF16)

    @pl.when(pl.program_id(0) == 0)
    def _():
        t = pltpu.roll(kpe_src_ref[...].T, MLA_NOPE, 1)
        lane = lax.broadcasted_iota(jnp.int32, t.shape, 1)
        kpe_ref[...] = jnp.where((lane >= MLA_NOPE) & (lane < MLA_QK), t, 0.0).astype(BF16)


def _relayout_w_in(w_in):
    d, n = w_in.shape
    kpe_lo = Q_LORA + KV_LORA
    wide_lo = kpe_lo + MLA_ROPE
    cols = W_RELAYOUT_COLS
    assert kpe_lo % cols == 0 and (n - wide_lo) % cols == 0
    n_lat = kpe_lo // cols

    def source_row(i):
        return pl.multiple_of(i * cols + jnp.minimum(i // n_lat, 1) * MLA_ROPE, MLA_ROPE)

    return pl.pallas_call(
        _relayout_body,
        grid=((n - MLA_ROPE) // cols,),
        in_specs=[pl.BlockSpec((pl.Element(cols), pl.Element(d)), lambda i: (source_row(i), 0)),
                  pl.BlockSpec((pl.Element(HEAD_PAD), pl.Element(d)), lambda i: (kpe_lo, 0))],
        out_specs=[pl.BlockSpec((d, cols), lambda i: (0, i)), pl.BlockSpec((d, HEAD_PAD), lambda i: (0, 0))],
        out_shape=[jax.ShapeDtypeStruct((d, n - MLA_ROPE), BF16), jax.ShapeDtypeStruct((d, HEAD_PAD), BF16)],
        compiler_params=pltpu.CompilerParams(dimension_semantics=("arbitrary",), vmem_limit_bytes=VMEM_LIMIT),
        name="relayout_w_in",
    )(w_in.T, w_in.T)


def _pad_weights(w_in, w_uq, w_ukv):
    w_cat, w_kpe = _relayout_w_in(w_in)
    wuq = jnp.pad(w_uq.reshape(Q_LORA, MLA_HEADS, MLA_QK), ((0, 0), (0, 0), (0, HEAD_PAD - MLA_QK)))
    wkv = w_ukv.reshape(KV_LORA, MLA_HEADS, MLA_NOPE + MLA_V)
    wk = jnp.pad(wkv[..., :MLA_NOPE], ((0, 0), (0, 0), (0, HEAD_PAD - MLA_NOPE)))
    wukv = jnp.concatenate([wk.reshape(KV_LORA, MLA_HEADS * HEAD_PAD),
                            wkv[..., MLA_NOPE:].reshape(KV_LORA, MLA_WIDTH)], axis=1)
    return (w_cat, w_kpe), wuq.reshape(Q_LORA, MLA_HEADS * HEAD_PAD).astype(BF16), wukv.astype(BF16)


def kernel(x, g_pre, w_in, b_gate, g_q, w_uq, g_kv, w_ukv, lb_logits, g_hgrn, w_branch_a, w_branch_b, w_out,
           g_post):
    assert g_pre.shape[0] == 1, "single-layer block"
    s = x.shape[1]
    win, wuq, wukv = _pad_weights(w_in[0], w_uq[0], w_ukv[0])
    ra, rs = _rope_tables(s)
    lower_bound = jax.nn.softmax(lb_logits.astype(F32), axis=0)[0:1]
    ghg = jnp.tile(g_hgrn[0], HG_HEADS)[None, :]

    q, k, vt, ga, o, gb, gt = _proj_call(x, g_pre, win, g_q, wuq, g_kv, wukv, ra, rs, b_gate, lower_bound)
    attn = _attn_call(q, k, vt)
    return _out_call(x, attn, ga, o, gb, gt, w_branch_a[0].astype(BF16), w_branch_b[0].astype(BF16),
                     w_out[0].astype(BF16), g_post, ghg)
```

```python
import functools
import math

import jax
import jax.numpy as jnp
import numpy as np
from jax import lax
from jax.experimental import pallas as pl
from jax.experimental.pallas import tpu as pltpu

F32 = jnp.float32
BF16 = jnp.bfloat16

D_MODEL = 1024
CHUNK = 64
EPS = 1e-6

MLA_HEADS = 8
MLA_NOPE = 64
MLA_ROPE = 32
MLA_V = 64
MLA_QK = MLA_NOPE + MLA_ROPE
Q_LORA = 768
KV_LORA = 256
ROPE_THETA = 10000.0
MLA_WIDTH = MLA_HEADS * MLA_V

HG_HEADS = 8
HG_DK = 64
HG_DV = 64
HG_WIDTH = HG_HEADS * HG_DV

LANES = 128
HEAD_PAD = LANES
HEADS_PER_GROUP = LANES // MLA_V
N_GROUPS = MLA_HEADS // HEADS_PER_GROUP

COL_CQ = 0
COL_CKV = COL_CQ + Q_LORA
COL_KPE = COL_CKV + KV_LORA
COL_GA = COL_KPE + HEAD_PAD
COL_HQ = COL_GA + MLA_WIDTH
COL_HF = COL_HQ + HG_WIDTH
COL_HI = COL_HF + HG_WIDTH
COL_GB = COL_HI + HG_WIDTH
COL_MG = COL_GB + HG_WIDTH
D_IN_PAD = COL_MG + 2 * D_MODEL

VMEM_LIMIT = 56 * 1024 * 1024

TM_PROJ = 512
TQ_ATTN = 512
SUM_ROWS = 16
UNROLL_DIAGONAL = 2
UNROLL_VISIBLE = 8
C_HGRN = 64
TM_OUT = 1024
OUT_PARTS = 2
W_RELAYOUT_COLS = 512


def _sigmoid(z):
    return 1.0 / (1.0 + jnp.exp(-z))


def _rms(t):
    return t * lax.rsqrt(jnp.mean(t * t, axis=-1, keepdims=True) + EPS)


def _dot(a, b):
    return jnp.dot(a, b, preferred_element_type=F32)


def _dot_nt(a, b):
    return lax.dot_general(a, b, (((1,), (1,)), ((), ())), preferred_element_type=F32)


def _cumsum_rows(t):
    rows = t.shape[0]
    row = lax.broadcasted_iota(jnp.int32, t.shape, 0)
    step = 1
    while step < rows:
        t = t + jnp.where(row >= step, pltpu.roll(t, step, 0), 0.0)
        step *= 2
    return t


def _recurrence_stages(proj, lb, gain, o_ref):
    c_len = C_HGRN
    groups = range(HG_HEADS // 2)
    low = lax.broadcasted_iota(jnp.int32, (c_len, LANES), 1) < HG_DK
    causal = (lax.broadcasted_iota(jnp.int32, (2 * c_len, c_len), 0) % c_len
              >= lax.broadcasted_iota(jnp.int32, (2 * c_len, c_len), 1))
    same_head = ((lax.broadcasted_iota(jnp.int32, (LANES, LANES), 0) < HG_DV)
                 == (lax.broadcasted_iota(jnp.int32, (LANES, LANES), 1) < HG_DK))

    def stack_heads(t):
        zero = jnp.zeros_like(t)
        return jnp.concatenate([jnp.where(low, t, zero), jnp.where(low, zero, t)], axis=0)

    def gates(c):
        f = lb + (1.0 - lb) * _sigmoid(proj["hf"][c * c_len:(c + 1) * c_len])
        return 1.0 - f, _cumsum_rows(jnp.log2(f))

    def scores(c, gated):
        k_in, cum = gated
        rows = slice(c * c_len, (c + 1) * c_len)
        mid = cum[c_len // 2 - 1:c_len // 2, :]
        tot = cum[c_len - 1:c_len, :]
        q_mid = proj["hq"][rows] * jnp.exp2(cum - mid)
        k_mid = k_in * jnp.exp2(mid - cum)
        q_dec = (q_mid * jnp.exp2(mid)).astype(BF16)
        k_end = (k_mid * jnp.exp2(tot - mid)).astype(BF16)
        q_mid = q_mid.astype(BF16)
        k_mid = k_mid.astype(BF16)
        v = proj["hi"][rows]
        per_group = []
        for g in groups:
            cols = slice(g * LANES, (g + 1) * LANES)
            a = _dot_nt(stack_heads(q_mid[:, cols]), k_mid[:, cols])
            v_t = v[:, cols].T.astype(BF16)
            upd = _dot(v_t, k_end[:, cols])
            per_group.append((a, v_t, upd, stack_heads(q_dec[:, cols])))
        return per_group, jnp.exp2(tot)

    def output(c, scored, states):
        per_group, decay = scored
        rows = slice(c * c_len, (c + 1) * c_len)
        new_states = []
        for g in groups:
            cols = slice(g * LANES, (g + 1) * LANES)
            a, v_t, upd, q_dec = per_group[g]
            a = jnp.where(causal, a, 0.0).astype(BF16)
            lhs = jnp.concatenate([q_dec, a], axis=1)
            rhs_t = jnp.concatenate([states[g].astype(BF16), v_t], axis=1)
            res = _dot_nt(lhs, rhs_t)
            o = jnp.where(low, res[:c_len], res[c_len:])
            sq = o * o
            msq_lo = jnp.sum(jnp.where(low, sq, 0.0), axis=-1, keepdims=True) * (1.0 / HG_DV)
            msq_hi = jnp.sum(jnp.where(low, 0.0, sq), axis=-1, keepdims=True) * (1.0 / HG_DV)
            inv_rms = jnp.where(low, lax.rsqrt(msq_lo + EPS), lax.rsqrt(msq_hi + EPS))
            o_ref[0, rows, cols] = (o * inv_rms * gain[:, cols]).astype(BF16)
            new_states.append(states[g] * decay[:, cols] + jnp.where(same_head, upd, 0.0))
        return new_states

    return gates, scores, output


def _proj_body(x_ref, gpre_ref, wcat_ref, wkpe_ref, gq_ref, wuq_ref, gkv_ref, wukv_ref, ra_ref, rs_ref,
               bg_ref, lb_ref, ghg_ref, q_ref, k_ref, v_ref, ga_ref, o_ref, gb_ref, gt_ref, st_ref):
    tm = x_ref.shape[1]
    groups = range(HG_HEADS // 2)

    @pl.when(pl.program_id(1) == 0)
    def _():
        st_ref[...] = jnp.zeros_like(st_ref)

    h = (_rms(x_ref[0]) * gpre_ref[...]).astype(BF16)

    def proj(lo, width):
        if lo == COL_KPE:
            return _dot(h, wkpe_ref[...])
        if lo > COL_KPE:
            lo -= HEAD_PAD
        return _dot(h, wcat_ref[:, lo:lo + width])

    ra = ra_ref[...]
    rs = rs_ref[...]
    lane = lax.broadcasted_iota(jnp.int32, ra.shape, 1)
    takes_upper = lane < MLA_NOPE + MLA_ROPE // 2

    def rope(t):
        partner = jnp.where(takes_upper, pltpu.roll(t, LANES - MLA_ROPE // 2, 1), pltpu.roll(t, MLA_ROPE // 2, 1))
        return t * ra + partner * rs

    held = {}

    def q_down():
        held["cqn"] = (_rms(proj(COL_CQ, Q_LORA)) * gq_ref[...]).astype(BF16)

    def q_up():
        scale = math.log2(math.e) / math.sqrt(MLA_QK)
        qu = _dot(held["cqn"], wuq_ref[...])
        for hh in range(MLA_HEADS):
            q_ref[0, hh] = (rope(qu[:, hh * HEAD_PAD:(hh + 1) * HEAD_PAD]) * scale).T.astype(BF16)

    def kv_down():
        held["ckvn"] = (_rms(proj(COL_CKV, KV_LORA)) * gkv_ref[...]).astype(BF16)
        held["kpe"] = rope(proj(COL_KPE, HEAD_PAD))

    def kv_up():
        kvu = _dot(held["ckvn"], wukv_ref[...])
        for hh in range(MLA_HEADS):
            k_ref[0, hh] = (kvu[:, hh * HEAD_PAD:(hh + 1) * HEAD_PAD] + held["kpe"]).astype(BF16)
        v_ref[0] = kvu[:, MLA_HEADS * HEAD_PAD:].T.astype(BF16)

    def silu_gate(lo, out_ref):
        z = proj(lo, out_ref.shape[-1])
        out_ref[0] = (z * _sigmoid(z)).astype(BF16)

    def merge_gates(c, width=2 * LANES):
        z = proj(COL_MG + c * width, width) + bg_ref[:, c * width:(c + 1) * width]
        gt_ref[0, :, c * width:(c + 1) * width] = _sigmoid(z).astype(BF16)

    pieces = [functools.partial(merge_gates, c) for c in range(D_MODEL // LANES)]
    pieces[1:1] = [q_down]
    pieces[3:3] = [kv_down]
    pieces[5:5] = [q_up]
    pieces[7:7] = [kv_up]
    pieces += [functools.partial(silu_gate, COL_GA, ga_ref), functools.partial(silu_gate, COL_GB, gb_ref)]

    def next_piece():
        if pieces:
            pieces.pop(0)()

    rec = {}
    gates, scores, output = _recurrence_stages(rec, lb_ref[...], ghg_ref[...], o_ref)
    n_chunks = tm // C_HGRN
    rec["hf"] = proj(COL_HF, HG_WIDTH)
    gated = {0: gates(0)}
    rec["hq"] = proj(COL_HQ, HG_WIDTH)
    if n_chunks > 1:
        gated[1] = gates(1)
    rec["hi"] = proj(COL_HI, HG_WIDTH)
    scored = {0: scores(0, gated.pop(0))}
    states = [st_ref[g] for g in groups]
    for c in range(n_chunks):
        next_piece()
        if c + 2 < n_chunks:
            gated[c + 2] = gates(c + 2)
        next_piece()
        if c + 1 < n_chunks:
            scored[c + 1] = scores(c + 1, gated.pop(c + 1))
        states = output(c, scored.pop(c), states)
    for g in groups:
        st_ref[g] = states[g]
    while pieces:
        next_piece()


def _proj_call(x, gpre, win, gq, wuq, gkv, wukv, ra, rs, bg, lb, ghg):
    b, s, _ = x.shape
    tm = min(TM_PROJ, s)
    const = lambda bi, si: (0, 0)
    tok = lambda bi, si: (bi, si, 0)

    def full(a):
        return pl.BlockSpec(a.shape, const, pipeline_mode=pl.Buffered(1))

    def tok_out(width, dtype):
        return jax.ShapeDtypeStruct((b, s, width), dtype), pl.BlockSpec((1, tm, width), tok)

    head_shape = jax.ShapeDtypeStruct((b, MLA_HEADS, s, HEAD_PAD), BF16)
    head_spec = pl.BlockSpec((1, MLA_HEADS, tm, HEAD_PAD), lambda bi, si: (bi, 0, si, 0))
    outs = [(jax.ShapeDtypeStruct((b, MLA_HEADS, HEAD_PAD, s), BF16),
             pl.BlockSpec((1, MLA_HEADS, HEAD_PAD, tm), lambda bi, si: (bi, 0, 0, si))),
            (head_shape, head_spec),
            (jax.ShapeDtypeStruct((b, MLA_WIDTH, s), BF16),
             pl.BlockSpec((1, MLA_WIDTH, tm), lambda bi, si: (bi, 0, si))),
            tok_out(MLA_WIDTH, BF16), tok_out(HG_WIDTH, BF16), tok_out(HG_WIDTH, BF16),
            tok_out(2 * D_MODEL, BF16)]
    rope_spec = pl.BlockSpec((tm, HEAD_PAD), lambda bi, si: (si, 0))
    return pl.pallas_call(
        _proj_body,
        grid=(b, s // tm),
        in_specs=[pl.BlockSpec((1, tm, D_MODEL), tok), full(gpre), *(full(w) for w in win), full(gq), full(wuq),
                  full(gkv), full(wukv), rope_spec, rope_spec, full(bg), full(lb), full(ghg)],
        out_specs=[o[1] for o in outs],
        out_shape=[o[0] for o in outs],
        scratch_shapes=[pltpu.VMEM((HG_HEADS // 2, LANES, LANES), F32)],
        compiler_params=pltpu.CompilerParams(dimension_semantics=("parallel", "arbitrary"),
                                             vmem_limit_bytes=VMEM_LIMIT),
        name="proj_hgrn",
    )(x, gpre, *win, gq, wuq, gkv, wukv, ra, rs, bg, lb, ghg)


def _attn_body(qtab_ref, ktab_ref, q_ref, k_ref, vt_ref, o_ref, s_ref, m_ref, acc_ref):
    tq = s_ref.shape[-1]
    n_q = k_ref.shape[2] // tq
    n_off = n_q * (n_q - 1) // 2
    heads = range(HEADS_PER_GROUP)
    key_chunk = lax.broadcasted_iota(jnp.int32, (tq, tq), 0) // CHUNK
    query_chunk = lax.broadcasted_iota(jnp.int32, (tq, tq), 1) // CHUNK
    visible = key_chunk <= query_chunk
    ones = jnp.ones((SUM_ROWS, tq), BF16)

    m_ref[...] = jnp.full(m_ref.shape, -jnp.inf, F32)
    acc_ref[...] = jnp.zeros(acc_ref.shape, F32)

    half = tq // 2
    assert half % CHUNK == 0 and half % LANES == 0
    early, late, everything = slice(0, half), slice(half, tq), slice(0, tq)

    def column_parts(on_diagonal):
        return [(early, early), (late, everything)] if on_diagonal else [(everything, everything)]

    def score(slot, pos, hh, on_diagonal):
        q_off = qtab_ref[pos] * tq
        k_off = ktab_ref[pos] * tq
        part_max = []
        for queries, keys in column_parts(on_diagonal):
            k_rows = pl.ds(pl.multiple_of(k_off + keys.start, half), keys.stop - keys.start)
            q_cols = pl.ds(pl.multiple_of(q_off + queries.start, half), queries.stop - queries.start)
            sc = _dot(k_ref[0, hh, k_rows, :], q_ref[0, hh, :, q_cols])
            if on_diagonal:
                sc = jnp.where(visible[keys, queries], sc, -jnp.inf)
            s_ref[slot, hh, keys, queries] = sc
            part_max.append(jnp.max(sc, axis=0, keepdims=True))
        return jnp.concatenate(part_max, axis=1)

    def absorb(slot, pos, hh, block_max, on_diagonal):
        qi = qtab_ref[pos]
        k_off = ktab_ref[pos] * tq
        m_old = m_ref[qi, hh]
        m_new = jnp.maximum(m_old, block_max)
        m_ref[qi, hh] = m_new
        alpha = jnp.exp2(m_old - m_new)
        for queries, keys in column_parts(on_diagonal):
            p = jnp.exp2(s_ref[slot, hh, keys, queries] - m_new[:, queries]).astype(BF16)
            k_cols = pl.ds(pl.multiple_of(k_off + keys.start, half), keys.stop - keys.start)
            vt = vt_ref[0, hh * MLA_V:(hh + 1) * MLA_V, k_cols]
            vt_ones = jnp.concatenate([vt, ones[:, keys]], axis=0)
            acc_ref[qi, hh, :, queries] = (alpha[:, queries] * acc_ref[qi, hh, :, queries]
                                           + _dot(vt_ones, p))

    def pipeline(first, count, on_diagonal, unroll):
        if count == 0:
            return
        assert unroll % 2 == 0
        n_loop = (count - 1) // unroll

        def several(pos, n, block_max, score_last):
            for u in range(n):
                nxt = []
                for hh in heads:
                    if u + 1 < n or score_last:
                        nxt.append(score((u + 1) % 2, pos + u + 1, hh, on_diagonal))
                    absorb(u % 2, pos + u, hh, block_max[hh], on_diagonal)
                block_max = tuple(nxt)
            return block_max

        first_max = tuple(score(0, first, hh, on_diagonal) for hh in heads)
        block_max = lax.fori_loop(0, n_loop, lambda t, bm: several(first + unroll * t, unroll, bm, True), first_max)
        several(first + unroll * n_loop, count - unroll * n_loop, block_max, False)

    pipeline(0, n_q, True, UNROLL_DIAGONAL)
    pipeline(n_q, n_off, False, UNROLL_VISIBLE)

    def finish(qi, carry):
        out_t = jnp.concatenate([acc_ref[qi, hh, :MLA_V, :] / acc_ref[qi, hh, MLA_V:MLA_V + 1, :] for hh in heads],
                                axis=0)
        o_ref[0, pl.ds(pl.multiple_of(qi * tq, tq), tq), :] = out_t.T.astype(BF16)
        return carry

    lax.fori_loop(0, n_q, finish, 0)


def _attn_call(q, k, vt):
    b, _, s, _ = k.shape
    tq = min(TQ_ATTN, s)
    n_q = s // tq
    off = [(qi, kj) for kj in range(n_q) for qi in range(kj + 1, n_q)]
    order = [(qi, qi) for qi in range(n_q)] + off
    qtab = jnp.asarray([p[0] for p in order], jnp.int32)
    ktab = jnp.asarray([p[1] for p in order], jnp.int32)
    grid_spec = pltpu.PrefetchScalarGridSpec(
        num_scalar_prefetch=2,
        grid=(b, N_GROUPS),
        in_specs=[pl.BlockSpec((1, HEADS_PER_GROUP, HEAD_PAD, s), lambda bi, g, qt, kt: (bi, g, 0, 0)),
                  pl.BlockSpec((1, HEADS_PER_GROUP, s, HEAD_PAD), lambda bi, g, qt, kt: (bi, g, 0, 0)),
                  pl.BlockSpec((1, LANES, s), lambda bi, g, qt, kt: (bi, g, 0))],
        out_specs=pl.BlockSpec((1, s, LANES), lambda bi, g, qt, kt: (bi, 0, g)),
        scratch_shapes=[pltpu.VMEM((2, HEADS_PER_GROUP, tq, tq), F32),
                        pltpu.VMEM((n_q, HEADS_PER_GROUP, 1, tq), F32),
                        pltpu.VMEM((n_q, HEADS_PER_GROUP, MLA_V + SUM_ROWS, tq), F32)])
    return pl.pallas_call(
        _attn_body,
        grid_spec=grid_spec,
        out_shape=jax.ShapeDtypeStruct((b, s, MLA_WIDTH), BF16),
        compiler_params=pltpu.CompilerParams(dimension_semantics=("parallel", "parallel"),
                                             vmem_limit_bytes=VMEM_LIMIT),
        name="attn",
    )(qtab, ktab, q, k, vt)


def _out_body(x_ref, attn_ref, ga_ref, o_ref, gb_ref, gt_ref, wa_ref, wb_ref, wo_ref, gpost_ref, out_ref):
    tm = x_ref.shape[1]
    parts = [slice(i * tm // OUT_PARTS, (i + 1) * tm // OUT_PARTS) for i in range(OUT_PARTS)]

    def merged(r):
        ya = _dot(attn_ref[0, r] * ga_ref[0, r], wa_ref[...])
        yb = _dot(o_ref[0, r] * gb_ref[0, r], wb_ref[...])
        m = gt_ref[0, r, :D_MODEL].astype(F32) * ya + gt_ref[0, r, D_MODEL:].astype(F32) * yb
        return m.astype(BF16)

    ms = [merged(r) for r in parts]
    ys = [_dot(m, wo_ref[...]) for m in ms]
    for r, y in zip(parts, ys):
        out_ref[0, r] = x_ref[0, r] + _rms(y) * gpost_ref[...]


def _out_call(x, attn, ga, o, gb, gt, wa, wb, wo, gpost):
    b, s, _ = x.shape
    tm = min(TM_OUT, s)
    tok = lambda bi, si: (bi, si, 0)
    const = lambda bi, si: (0, 0)

    def tspec(a):
        return pl.BlockSpec((1, tm, a.shape[-1]), tok)

    def full(a):
        return pl.BlockSpec(a.shape, const)

    return pl.pallas_call(
        _out_body,
        grid=(b, s // tm),
        in_specs=[tspec(x), tspec(attn), tspec(ga), tspec(o), tspec(gb), tspec(gt),
                  full(wa), full(wb), full(wo), full(gpost)],
        out_specs=tspec(x),
        out_shape=jax.ShapeDtypeStruct(x.shape, x.dtype),
        compiler_params=pltpu.CompilerParams(dimension_semantics=("parallel", "parallel"),
                                             vmem_limit_bytes=VMEM_LIMIT),
        name="merge_out",
    )(x, attn, ga, o, gb, gt, wa, wb, wo, gpost)


def _rope_tables(s):
    inv = ROPE_THETA ** (-np.arange(0, MLA_ROPE, 2, dtype=np.float64) / MLA_ROPE)
    ang = np.arange(s, dtype=np.float64)[:, None] * inv[None, :]
    cos, sin = np.cos(ang), np.sin(ang)
    pad = np.zeros((s, HEAD_PAD - MLA_QK))
    ra = np.concatenate([np.ones((s, MLA_NOPE)), cos, cos, pad], axis=1)
    rs = np.concatenate([np.zeros((s, MLA_NOPE)), -sin, sin, pad], axis=1)
    return jnp.asarray(ra, F32), jnp.asarray(rs, F32)


def _relayout_body(wt_ref, kpe_src_ref, out_ref, kpe_ref):
    out_ref[...] = wt_ref[...].T.astype(BF16)

    @pl.when(pl.program_id(0) == 0)
    def _():
        t = pltpu.roll(kpe_src_ref[...].T, MLA_NOPE, 1)
        lane = lax.broadcasted_iota(jnp.int32, t.shape, 1)
        kpe_ref[...] = jnp.where((lane >= MLA_NOPE) & (lane < MLA_QK), t, 0.0).astype(BF16)


def _relayout_w_in(w_in):
    d, n = w_in.shape
    kpe_lo = Q_LORA + KV_LORA
    wide_lo = kpe_lo + MLA_ROPE
    cols = W_RELAYOUT_COLS
    assert kpe_lo % cols == 0 and (n - wide_lo) % cols == 0
    n_lat = kpe_lo // cols

    def source_row(i):
        return pl.multiple_of(i * cols + jnp.minimum(i // n_lat, 1) * MLA_ROPE, MLA_ROPE)

    return pl.pallas_call(
        _relayout_body,
        grid=((n - MLA_ROPE) // cols,),
        in_specs=[pl.BlockSpec((pl.Element(cols), pl.Element(d)), lambda i: (source_row(i), 0)),
                  pl.BlockSpec((pl.Element(HEAD_PAD), pl.Element(d)), lambda i: (kpe_lo, 0))],
        out_specs=[pl.BlockSpec((d, cols), lambda i: (0, i)), pl.BlockSpec((d, HEAD_PAD), lambda i: (0, 0))],
        out_shape=[jax.ShapeDtypeStruct((d, n - MLA_ROPE), BF16), jax.ShapeDtypeStruct((d, HEAD_PAD), BF16)],
        compiler_params=pltpu.CompilerParams(dimension_semantics=("arbitrary",), vmem_limit_bytes=VMEM_LIMIT),
        name="relayout_w_in",
    )(w_in.T, w_in.T)


def _pad_weights(w_in, w_uq, w_ukv):
    w_cat, w_kpe = _relayout_w_in(w_in)
    wuq = jnp.pad(w_uq.reshape(Q_LORA, MLA_HEADS, MLA_QK), ((0, 0), (0, 0), (0, HEAD_PAD - MLA_QK)))
    wkv = w_ukv.reshape(KV_LORA, MLA_HEADS, MLA_NOPE + MLA_V)
    wk = jnp.pad(wkv[..., :MLA_NOPE], ((0, 0), (0, 0), (0, HEAD_PAD - MLA_NOPE)))
    wukv = jnp.concatenate([wk.reshape(KV_LORA, MLA_HEADS * HEAD_PAD),
                            wkv[..., MLA_NOPE:].reshape(KV_LORA, MLA_WIDTH)], axis=1)
    return (w_cat, w_kpe), wuq.reshape(Q_LORA, MLA_HEADS * HEAD_PAD).astype(BF16), wukv.astype(BF16)


def kernel(x, g_pre, w_in, b_gate, g_q, w_uq, g_kv, w_ukv, lb_logits, g_hgrn, w_branch_a, w_branch_b, w_out,
           g_post):
    assert g_pre.shape[0] == 1, "single-layer block"
    s = x.shape[1]
    win, wuq, wukv = _pad_weights(w_in[0], w_uq[0], w_ukv[0])
    ra, rs = _rope_tables(s)
    lower_bound = jax.nn.softmax(lb_logits.astype(F32), axis=0)[0:1]
    ghg = jnp.tile(g_hgrn[0], HG_HEADS)[None, :]

    q, k, vt, ga, o, gb, gt = _proj_call(x, g_pre, win, g_q, wuq, g_kv, wukv, ra, rs, b_gate, lower_bound, ghg)
    attn = _attn_call(q, k, vt)
    return _out_call(x, attn, ga, o, gb, gt, w_branch_a[0].astype(BF16), w_branch_b[0].astype(BF16),
                     w_out[0].astype(BF16), g_post)
```

```python
import functools
import math

import jax
import jax.numpy as jnp
import numpy as np
from jax import lax
from jax.experimental import pallas as pl
from jax.experimental.pallas import tpu as pltpu

F32 = jnp.float32
BF16 = jnp.bfloat16

D_MODEL = 1024
CHUNK = 64
EPS = 1e-6

MLA_HEADS = 8
MLA_NOPE = 64
MLA_ROPE = 32
MLA_V = 64
MLA_QK = MLA_NOPE + MLA_ROPE
Q_LORA = 768
KV_LORA = 256
ROPE_THETA = 10000.0
MLA_WIDTH = MLA_HEADS * MLA_V

HG_HEADS = 8
HG_DK = 64
HG_DV = 64
HG_WIDTH = HG_HEADS * HG_DV

LANES = 128
HEAD_PAD = LANES
HEADS_PER_GROUP = LANES // MLA_V
N_GROUPS = MLA_HEADS // HEADS_PER_GROUP

COL_CQ = 0
COL_CKV = COL_CQ + Q_LORA
COL_KPE = COL_CKV + KV_LORA
COL_GA = COL_KPE + HEAD_PAD
COL_HQ = COL_GA + MLA_WIDTH
COL_HF = COL_HQ + HG_WIDTH
COL_HI = COL_HF + HG_WIDTH
COL_GB = COL_HI + HG_WIDTH
COL_MG = COL_GB + HG_WIDTH
D_IN_PAD = COL_MG + 2 * D_MODEL

VMEM_LIMIT = 56 * 1024 * 1024

TM_PROJ = 512
TQ_ATTN = 512
SUM_ROWS = 16
UNROLL_DIAGONAL = 4
UNROLL_VISIBLE = 8
C_HGRN = 64
TM_OUT = 1024
OUT_PARTS = 2
W_RELAYOUT_COLS = 512


def _sigmoid(z):
    return 1.0 / (1.0 + jnp.exp(-z))


def _rms(t):
    return t * lax.rsqrt(jnp.mean(t * t, axis=-1, keepdims=True) + EPS)


def _dot(a, b):
    return jnp.dot(a, b, preferred_element_type=F32)


def _dot_nt(a, b):
    return lax.dot_general(a, b, (((1,), (1,)), ((), ())), preferred_element_type=F32)


def _cumsum_rows(t):
    rows = t.shape[0]
    row = lax.broadcasted_iota(jnp.int32, t.shape, 0)
    step = 1
    while step < rows:
        t = t + jnp.where(row >= step, pltpu.roll(t, step, 0), 0.0)
        step *= 2
    return t


def _recurrence_stages(proj, lb, gain, o_ref):
    c_len = C_HGRN
    groups = range(HG_HEADS // 2)
    low = lax.broadcasted_iota(jnp.int32, (c_len, LANES), 1) < HG_DK
    causal = (lax.broadcasted_iota(jnp.int32, (2 * c_len, c_len), 0) % c_len
              >= lax.broadcasted_iota(jnp.int32, (2 * c_len, c_len), 1))
    same_head = ((lax.broadcasted_iota(jnp.int32, (LANES, LANES), 0) < HG_DV)
                 == (lax.broadcasted_iota(jnp.int32, (LANES, LANES), 1) < HG_DK))

    def stack_heads(t):
        zero = jnp.zeros_like(t)
        return jnp.concatenate([jnp.where(low, t, zero), jnp.where(low, zero, t)], axis=0)

    def gates(c):
        f = lb + (1.0 - lb) * _sigmoid(proj["hf"][c * c_len:(c + 1) * c_len])
        return 1.0 - f, _cumsum_rows(jnp.log2(f))

    def scores(c, gated):
        k_in, cum = gated
        rows = slice(c * c_len, (c + 1) * c_len)
        mid = cum[c_len // 2 - 1:c_len // 2, :]
        tot = cum[c_len - 1:c_len, :]
        q_mid = proj["hq"][rows] * jnp.exp2(cum - mid)
        k_mid = k_in * jnp.exp2(mid - cum)
        q_dec = (q_mid * jnp.exp2(mid)).astype(BF16)
        k_end = (k_mid * jnp.exp2(tot - mid)).astype(BF16)
        q_mid = q_mid.astype(BF16)
        k_mid = k_mid.astype(BF16)
        v = proj["hi"][rows]
        per_group = []
        for g in groups:
            cols = slice(g * LANES, (g + 1) * LANES)
            a = _dot_nt(stack_heads(q_mid[:, cols]), k_mid[:, cols])
            v_t = v[:, cols].T.astype(BF16)
            upd = _dot(v_t, k_end[:, cols])
            per_group.append((a, v_t, upd, stack_heads(q_dec[:, cols])))
        return per_group, jnp.exp2(tot)

    def output(c, scored, states):
        per_group, decay = scored
        rows = slice(c * c_len, (c + 1) * c_len)
        new_states = []
        for g in groups:
            cols = slice(g * LANES, (g + 1) * LANES)
            a, v_t, upd, q_dec = per_group[g]
            a = jnp.where(causal, a, 0.0).astype(BF16)
            lhs = jnp.concatenate([q_dec, a], axis=1)
            rhs_t = jnp.concatenate([states[g].astype(BF16), v_t], axis=1)
            res = _dot_nt(lhs, rhs_t)
            o = jnp.where(low, res[:c_len], res[c_len:])
            sq = o * o
            msq_lo = jnp.sum(jnp.where(low, sq, 0.0), axis=-1, keepdims=True) * (1.0 / HG_DV)
            msq_hi = jnp.sum(jnp.where(low, 0.0, sq), axis=-1, keepdims=True) * (1.0 / HG_DV)
            inv_rms = jnp.where(low, lax.rsqrt(msq_lo + EPS), lax.rsqrt(msq_hi + EPS))
            o_ref[0, rows, cols] = (o * inv_rms * gain[:, cols]).astype(BF16)
            new_states.append(states[g] * decay[:, cols] + jnp.where(same_head, upd, 0.0))
        return new_states

    return gates, scores, output


def _proj_body(x_ref, gpre_ref, wcat_ref, wkpe_ref, gq_ref, wuq_ref, gkv_ref, wukv_ref, ra_ref, rs_ref,
               bg_ref, lb_ref, ghg_ref, q_ref, k_ref, v_ref, ga_ref, o_ref, gb_ref, gt_ref, st_ref):
    tm = x_ref.shape[1]
    groups = range(HG_HEADS // 2)

    @pl.when(pl.program_id(1) == 0)
    def _():
        st_ref[...] = jnp.zeros_like(st_ref)

    h = (_rms(x_ref[0]) * gpre_ref[...]).astype(BF16)

    def proj(lo, width):
        if lo == COL_KPE:
            return _dot(h, wkpe_ref[...])
        if lo > COL_KPE:
            lo -= HEAD_PAD
        return _dot(h, wcat_ref[:, lo:lo + width])

    ra = ra_ref[...]
    rs = rs_ref[...]
    lane = lax.broadcasted_iota(jnp.int32, ra.shape, 1)
    takes_upper = lane < MLA_NOPE + MLA_ROPE // 2

    def rope(t):
        partner = jnp.where(takes_upper, pltpu.roll(t, LANES - MLA_ROPE // 2, 1), pltpu.roll(t, MLA_ROPE // 2, 1))
        return t * ra + partner * rs

    held = {}

    def q_down():
        held["cqn"] = (_rms(proj(COL_CQ, Q_LORA)) * gq_ref[...]).astype(BF16)

    def q_up():
        scale = math.log2(math.e) / math.sqrt(MLA_QK)
        qu = _dot(held["cqn"], wuq_ref[...])
        for hh in range(MLA_HEADS):
            q_ref[0, hh] = (rope(qu[:, hh * HEAD_PAD:(hh + 1) * HEAD_PAD]) * scale).T.astype(BF16)

    def kv_down():
        held["ckvn"] = (_rms(proj(COL_CKV, KV_LORA)) * gkv_ref[...]).astype(BF16)
        held["kpe"] = rope(proj(COL_KPE, HEAD_PAD))

    def kv_up():
        kvu = _dot(held["ckvn"], wukv_ref[...])
        for hh in range(MLA_HEADS):
            k_ref[0, hh] = (kvu[:, hh * HEAD_PAD:(hh + 1) * HEAD_PAD] + held["kpe"]).astype(BF16)
        v_ref[0] = kvu[:, MLA_HEADS * HEAD_PAD:].T.astype(BF16)

    def silu_gate(lo, out_ref):
        z = proj(lo, out_ref.shape[-1])
        out_ref[0] = (z * _sigmoid(z)).astype(BF16)

    def merge_gates(c, width=2 * LANES):
        z = proj(COL_MG + c * width, width) + bg_ref[:, c * width:(c + 1) * width]
        gt_ref[0, :, c * width:(c + 1) * width] = _sigmoid(z).astype(BF16)

    pieces = [functools.partial(merge_gates, c) for c in range(D_MODEL // LANES)]
    pieces[1:1] = [q_down]
    pieces[3:3] = [kv_down]
    pieces[5:5] = [q_up]
    pieces[7:7] = [kv_up]
    pieces += [functools.partial(silu_gate, COL_GA, ga_ref), functools.partial(silu_gate, COL_GB, gb_ref)]

    def next_piece():
        if pieces:
            pieces.pop(0)()

    rec = {}
    gates, scores, output = _recurrence_stages(rec, lb_ref[...], ghg_ref[...], o_ref)
    n_chunks = tm // C_HGRN
    rec["hf"] = proj(COL_HF, HG_WIDTH)
    gated = {0: gates(0)}
    rec["hq"] = proj(COL_HQ, HG_WIDTH)
    if n_chunks > 1:
        gated[1] = gates(1)
    rec["hi"] = proj(COL_HI, HG_WIDTH)
    scored = {0: scores(0, gated.pop(0))}
    states = [st_ref[g] for g in groups]
    for c in range(n_chunks):
        next_piece()
        if c + 2 < n_chunks:
            gated[c + 2] = gates(c + 2)
        next_piece()
        if c + 1 < n_chunks:
            scored[c + 1] = scores(c + 1, gated.pop(c + 1))
        states = output(c, scored.pop(c), states)
    for g in groups:
        st_ref[g] = states[g]
    while pieces:
        next_piece()


def _proj_call(x, gpre, win, gq, wuq, gkv, wukv, ra, rs, bg, lb, ghg):
    b, s, _ = x.shape
    tm = min(TM_PROJ, s)
    const = lambda bi, si: (0, 0)
    tok = lambda bi, si: (bi, si, 0)

    def full(a):
        return pl.BlockSpec(a.shape, const, pipeline_mode=pl.Buffered(1))

    def tok_out(width, dtype):
        return jax.ShapeDtypeStruct((b, s, width), dtype), pl.BlockSpec((1, tm, width), tok)

    head_shape = jax.ShapeDtypeStruct((b, MLA_HEADS, s, HEAD_PAD), BF16)
    head_spec = pl.BlockSpec((1, MLA_HEADS, tm, HEAD_PAD), lambda bi, si: (bi, 0, si, 0))
    outs = [(jax.ShapeDtypeStruct((b, MLA_HEADS, HEAD_PAD, s), BF16),
             pl.BlockSpec((1, MLA_HEADS, HEAD_PAD, tm), lambda bi, si: (bi, 0, 0, si))),
            (head_shape, head_spec),
            (jax.ShapeDtypeStruct((b, MLA_WIDTH, s), BF16),
             pl.BlockSpec((1, MLA_WIDTH, tm), lambda bi, si: (bi, 0, si))),
            tok_out(MLA_WIDTH, BF16), tok_out(HG_WIDTH, BF16), tok_out(HG_WIDTH, BF16),
            tok_out(2 * D_MODEL, BF16)]
    rope_spec = pl.BlockSpec((tm, HEAD_PAD), lambda bi, si: (si, 0))
    return pl.pallas_call(
        _proj_body,
        grid=(b, s // tm),
        in_specs=[pl.BlockSpec((1, tm, D_MODEL), tok), full(gpre), *(full(w) for w in win), full(gq), full(wuq),
                  full(gkv), full(wukv), rope_spec, rope_spec, full(bg), full(lb), full(ghg)],
        out_specs=[o[1] for o in outs],
        out_shape=[o[0] for o in outs],
        scratch_shapes=[pltpu.VMEM((HG_HEADS // 2, LANES, LANES), F32)],
        compiler_params=pltpu.CompilerParams(dimension_semantics=("parallel", "arbitrary"),
                                             vmem_limit_bytes=VMEM_LIMIT),
        name="proj_hgrn",
    )(x, gpre, *win, gq, wuq, gkv, wukv, ra, rs, bg, lb, ghg)


def _attn_body(qtab_ref, ktab_ref, q_ref, k_ref, vt_ref, o_ref, s_ref, m_ref, acc_ref):
    tq = s_ref.shape[-1]
    n_q = k_ref.shape[2] // tq
    n_off = n_q * (n_q - 1) // 2
    heads = range(HEADS_PER_GROUP)
    key_chunk = lax.broadcasted_iota(jnp.int32, (tq, tq), 0) // CHUNK
    query_chunk = lax.broadcasted_iota(jnp.int32, (tq, tq), 1) // CHUNK
    visible = key_chunk <= query_chunk
    ones = jnp.ones((SUM_ROWS, tq), BF16)

    m_ref[...] = jnp.full(m_ref.shape, -jnp.inf, F32)
    acc_ref[...] = jnp.zeros(acc_ref.shape, F32)

    half = tq // 2
    assert half % CHUNK == 0 and half % LANES == 0
    early, late, everything = slice(0, half), slice(half, tq), slice(0, tq)

    def column_parts(on_diagonal):
        return [(early, early), (late, everything)] if on_diagonal else [(everything, everything)]

    def score(slot, pos, hh, on_diagonal):
        q_off = qtab_ref[pos] * tq
        k_off = ktab_ref[pos] * tq
        part_max = []
        for queries, keys in column_parts(on_diagonal):
            k_rows = pl.ds(pl.multiple_of(k_off + keys.start, half), keys.stop - keys.start)
            q_cols = pl.ds(pl.multiple_of(q_off + queries.start, half), queries.stop - queries.start)
            sc = _dot(k_ref[0, hh, k_rows, :], q_ref[0, hh, :, q_cols])
            if on_diagonal:
                sc = jnp.where(visible[keys, queries], sc, -jnp.inf)
            s_ref[slot, hh, keys, queries] = sc
            part_max.append(jnp.max(sc, axis=0, keepdims=True))
        return jnp.concatenate(part_max, axis=1)

    def absorb(slot, pos, hh, block_max, on_diagonal):
        qi = qtab_ref[pos]
        k_off = ktab_ref[pos] * tq
        m_old = m_ref[qi, hh]
        m_new = jnp.maximum(m_old, block_max)
        m_ref[qi, hh] = m_new
        alpha = jnp.exp2(m_old - m_new)
        for queries, keys in column_parts(on_diagonal):
            p = jnp.exp2(s_ref[slot, hh, keys, queries] - m_new[:, queries]).astype(BF16)
            k_cols = pl.ds(pl.multiple_of(k_off + keys.start, half), keys.stop - keys.start)
            vt = vt_ref[0, hh * MLA_V:(hh + 1) * MLA_V, k_cols]
            vt_ones = jnp.concatenate([vt, ones[:, keys]], axis=0)
            acc_ref[qi, hh, :, queries] = (alpha[:, queries] * acc_ref[qi, hh, :, queries]
                                           + _dot(vt_ones, p))

    def pipeline(first, count, on_diagonal, unroll):
        if count == 0:
            return
        assert unroll % 2 == 0
        n_loop = (count - 1) // unroll

        def several(pos, n, block_max, score_last):
            for u in range(n):
                nxt = []
                for hh in heads:
                    if u + 1 < n or score_last:
                        nxt.append(score((u + 1) % 2, pos + u + 1, hh, on_diagonal))
                    absorb(u % 2, pos + u, hh, block_max[hh], on_diagonal)
                block_max = tuple(nxt)
            return block_max

        first_max = tuple(score(0, first, hh, on_diagonal) for hh in heads)
        block_max = lax.fori_loop(0, n_loop, lambda t, bm: several(first + unroll * t, unroll, bm, True), first_max)
        several(first + unroll * n_loop, count - unroll * n_loop, block_max, False)

    pipeline(0, n_q, True, UNROLL_DIAGONAL)
    pipeline(n_q, n_off, False, UNROLL_VISIBLE)

    def finish(qi, carry):
        out_t = jnp.concatenate([acc_ref[qi, hh, :MLA_V, :] / acc_ref[qi, hh, MLA_V:MLA_V + 1, :] for hh in heads],
                                axis=0)
        o_ref[0, pl.ds(pl.multiple_of(qi * tq, tq), tq), :] = out_t.T.astype(BF16)
        return carry

    lax.fori_loop(0, n_q, finish, 0)


def _attn_call(q, k, vt):
    b, _, s, _ = k.shape
    tq = min(TQ_ATTN, s)
    n_q = s // tq
    off = [(qi, kj) for kj in range(n_q) for qi in range(kj + 1, n_q)]
    order = [(qi, qi) for qi in range(n_q)] + off
    qtab = jnp.asarray([p[0] for p in order], jnp.int32)
    ktab = jnp.asarray([p[1] for p in order], jnp.int32)
    grid_spec = pltpu.PrefetchScalarGridSpec(
        num_scalar_prefetch=2,
        grid=(b, N_GROUPS),
        in_specs=[pl.BlockSpec((1, HEADS_PER_GROUP, HEAD_PAD, s), lambda bi, g, qt, kt: (bi, g, 0, 0)),
                  pl.BlockSpec((1, HEADS_PER_GROUP, s, HEAD_PAD), lambda bi, g, qt, kt: (bi, g, 0, 0)),
                  pl.BlockSpec((1, LANES, s), lambda bi, g, qt, kt: (bi, g, 0))],
        out_specs=pl.BlockSpec((1, s, LANES), lambda bi, g, qt, kt: (bi, 0, g)),
        scratch_shapes=[pltpu.VMEM((2, HEADS_PER_GROUP, tq, tq), F32),
                        pltpu.VMEM((n_q, HEADS_PER_GROUP, 1, tq), F32),
                        pltpu.VMEM((n_q, HEADS_PER_GROUP, MLA_V + SUM_ROWS, tq), F32)])
    return pl.pallas_call(
        _attn_body,
        grid_spec=grid_spec,
        out_shape=jax.ShapeDtypeStruct((b, s, MLA_WIDTH), BF16),
        compiler_params=pltpu.CompilerParams(dimension_semantics=("parallel", "parallel"),
                                             vmem_limit_bytes=VMEM_LIMIT),
        name="attn",
    )(qtab, ktab, q, k, vt)


def _out_body(x_ref, attn_ref, ga_ref, o_ref, gb_ref, gt_ref, wa_ref, wb_ref, wo_ref, gpost_ref, out_ref):
    tm = x_ref.shape[1]
    parts = [slice(i * tm // OUT_PARTS, (i + 1) * tm // OUT_PARTS) for i in range(OUT_PARTS)]

    def merged(r):
        ya = _dot(attn_ref[0, r] * ga_ref[0, r], wa_ref[...])
        yb = _dot(o_ref[0, r] * gb_ref[0, r], wb_ref[...])
        m = gt_ref[0, r, :D_MODEL].astype(F32) * ya + gt_ref[0, r, D_MODEL:].astype(F32) * yb
        return m.astype(BF16)

    ms = [merged(r) for r in parts]
    ys = [_dot(m, wo_ref[...]) for m in ms]
    for r, y in zip(parts, ys):
        out_ref[0, r] = x_ref[0, r] + _rms(y) * gpost_ref[...]


def _out_call(x, attn, ga, o, gb, gt, wa, wb, wo, gpost):
    b, s, _ = x.shape
    tm = min(TM_OUT, s)
    tok = lambda bi, si: (bi, si, 0)
    const = lambda bi, si: (0, 0)

    def tspec(a):
        return pl.BlockSpec((1, tm, a.shape[-1]), tok)

    def full(a):
        return pl.BlockSpec(a.shape, const)

    return pl.pallas_call(
        _out_body,
        grid=(b, s // tm),
        in_specs=[tspec(x), tspec(attn), tspec(ga), tspec(o), tspec(gb), tspec(gt),
                  full(wa), full(wb), full(wo), full(gpost)],
        out_specs=tspec(x),
        out_shape=jax.ShapeDtypeStruct(x.shape, x.dtype),
        compiler_params=pltpu.CompilerParams(dimension_semantics=("parallel", "parallel"),
                                             vmem_limit_bytes=VMEM_LIMIT),
        name="merge_out",
    )(x, attn, ga, o, gb, gt, wa, wb, wo, gpost)


def _rope_tables(s):
    inv = ROPE_THETA ** (-np.arange(0, MLA_ROPE, 2, dtype=np.float64) / MLA_ROPE)
    ang = np.arange(s, dtype=np.float64)[:, None] * inv[None, :]
    cos, sin = np.cos(ang), np.sin(ang)
    pad = np.zeros((s, HEAD_PAD - MLA_QK))
    ra = np.concatenate([np.ones((s, MLA_NOPE)), cos, cos, pad], axis=1)
    rs = np.concatenate([np.zeros((s, MLA_NOPE)), -sin, sin, pad], axis=1)
    return jnp.asarray(ra, F32), jnp.asarray(rs, F32)


def _relayout_body(wt_ref, kpe_src_ref, out_ref, kpe_ref):
    out_ref[...] = wt_ref[...].T.astype(BF16)

    @pl.when(pl.program_id(0) == 0)
    def _():
        t = pltpu.roll(kpe_src_ref[...].T, MLA_NOPE, 1)
        lane = lax.broadcasted_iota(jnp.int32, t.shape, 1)
        kpe_ref[...] = jnp.where((lane >= MLA_NOPE) & (lane < MLA_QK), t, 0.0).astype(BF16)


def _relayout_w_in(w_in):
    d, n = w_in.shape
    kpe_lo = Q_LORA + KV_LORA
    wide_lo = kpe_lo + MLA_ROPE
    cols = W_RELAYOUT_COLS
    assert kpe_lo % cols == 0 and (n - wide_lo) % cols == 0
    n_lat = kpe_lo // cols

    def source_row(i):
        return pl.multiple_of(i * cols + jnp.minimum(i // n_lat, 1) * MLA_ROPE, MLA_ROPE)

    return pl.pallas_call(
        _relayout_body,
        grid=((n - MLA_ROPE) // cols,),
        in_specs=[pl.BlockSpec((pl.Element(cols), pl.Element(d)), lambda i: (source_row(i), 0)),
                  pl.BlockSpec((pl.Element(HEAD_PAD), pl.Element(d)), lambda i: (kpe_lo, 0))],
        out_specs=[pl.BlockSpec((d, cols), lambda i: (0, i)), pl.BlockSpec((d, HEAD_PAD), lambda i: (0, 0))],
        out_shape=[jax.ShapeDtypeStruct((d, n - MLA_ROPE), BF16), jax.ShapeDtypeStruct((d, HEAD_PAD), BF16)],
        compiler_params=pltpu.CompilerParams(dimension_semantics=("arbitrary",), vmem_limit_bytes=VMEM_LIMIT),
        name="relayout_w_in",
    )(w_in.T, w_in.T)


def _pad_weights(w_in, w_uq, w_ukv):
    w_cat, w_kpe = _relayout_w_in(w_in)
    wuq = jnp.pad(w_uq.reshape(Q_LORA, MLA_HEADS, MLA_QK), ((0, 0), (0, 0), (0, HEAD_PAD - MLA_QK)))
    wkv = w_ukv.reshape(KV_LORA, MLA_HEADS, MLA_NOPE + MLA_V)
    wk = jnp.pad(wkv[..., :MLA_NOPE], ((0, 0), (0, 0), (0, HEAD_PAD - MLA_NOPE)))
    wukv = jnp.concatenate([wk.reshape(KV_LORA, MLA_HEADS * HEAD_PAD),
                            wkv[..., MLA_NOPE:].reshape(KV_LORA, MLA_WIDTH)], axis=1)
    return (w_cat, w_kpe), wuq.reshape(Q_LORA, MLA_HEADS * HEAD_PAD).astype(BF16), wukv.astype(BF16)


def kernel(x, g_pre, w_in, b_gate, g_q, w_uq, g_kv, w_ukv, lb_logits, g_hgrn, w_branch_a, w_branch_b, w_out,
           g_post):
    assert g_pre.shape[0] == 1, "single-layer block"
    s = x.shape[1]
    win, wuq, wukv = _pad_weights(w_in[0], w_uq[0], w_ukv[0])
    ra, rs = _rope_tables(s)
    lower_bound = jax.nn.softmax(lb_logits.astype(F32), axis=0)[0:1]
    ghg = jnp.tile(g_hgrn[0], HG_HEADS)[None, :]

    q, k, vt, ga, o, gb, gt = _proj_call(x, g_pre, win, g_q, wuq, g_kv, wukv, ra, rs, b_gate, lower_bound, ghg)
    attn = _attn_call(q, k, vt)
    return _out_call(x, attn, ga, o, gb, gt, w_branch_a[0].astype(BF16), w_branch_b[0].astype(BF16),
                     w_out[0].astype(BF16), g_post)
```

```python
import functools
import math

import jax
import jax.numpy as jnp
import numpy as np
from jax import lax
from jax.experimental import pallas as pl
from jax.experimental.pallas import tpu as pltpu

F32 = jnp.float32
BF16 = jnp.bfloat16

D_MODEL = 1024
CHUNK = 64
EPS = 1e-6

MLA_HEADS = 8
MLA_NOPE = 64
MLA_ROPE = 32
MLA_V = 64
MLA_QK = MLA_NOPE + MLA_ROPE
Q_LORA = 768
KV_LORA = 256
ROPE_THETA = 10000.0
MLA_WIDTH = MLA_HEADS * MLA_V

HG_HEADS = 8
HG_DK = 64
HG_DV = 64
HG_WIDTH = HG_HEADS * HG_DV

LANES = 128
HEAD_PAD = LANES
HEADS_PER_GROUP = LANES // MLA_V
N_GROUPS = MLA_HEADS // HEADS_PER_GROUP

COL_CQ = 0
COL_CKV = COL_CQ + Q_LORA
COL_KPE = COL_CKV + KV_LORA
COL_GA = COL_KPE + HEAD_PAD
COL_HQ = COL_GA + MLA_WIDTH
COL_HF = COL_HQ + HG_WIDTH
COL_HI = COL_HF + HG_WIDTH
COL_GB = COL_HI + HG_WIDTH
COL_MG = COL_GB + HG_WIDTH
D_IN_PAD = COL_MG + 2 * D_MODEL

VMEM_LIMIT = 56 * 1024 * 1024

TM_PROJ = 512
TQ_ATTN = 512
SUM_ROWS = 16
UNROLL_DIAGONAL = 4
UNROLL_VISIBLE = 8
C_HGRN = 64
TM_OUT = 1024
OUT_PARTS = 2
W_RELAYOUT_COLS = 512


def _sigmoid(z):
    return 1.0 / (1.0 + jnp.exp(-z))


def _rms(t):
    return t * lax.rsqrt(jnp.mean(t * t, axis=-1, keepdims=True) + EPS)


def _dot(a, b):
    return jnp.dot(a, b, preferred_element_type=F32)


def _dot_nt(a, b):
    return lax.dot_general(a, b, (((1,), (1,)), ((), ())), preferred_element_type=F32)


def _cumsum_rows(t):
    rows = t.shape[0]
    row = lax.broadcasted_iota(jnp.int32, t.shape, 0)
    step = 1
    while step < rows:
        t = t + jnp.where(row >= step, pltpu.roll(t, step, 0), 0.0)
        step *= 2
    return t


def _recurrence_stages(proj, lb, gain, o_ref):
    c_len = C_HGRN
    groups = range(HG_HEADS // 2)
    low = lax.broadcasted_iota(jnp.int32, (c_len, LANES), 1) < HG_DK
    causal = (lax.broadcasted_iota(jnp.int32, (2 * c_len, c_len), 0) % c_len
              >= lax.broadcasted_iota(jnp.int32, (2 * c_len, c_len), 1))
    same_head = ((lax.broadcasted_iota(jnp.int32, (LANES, LANES), 0) < HG_DV)
                 == (lax.broadcasted_iota(jnp.int32, (LANES, LANES), 1) < HG_DK))

    def stack_heads(t):
        zero = jnp.zeros_like(t)
        return jnp.concatenate([jnp.where(low, t, zero), jnp.where(low, zero, t)], axis=0)

    def gates(c):
        f = lb + (1.0 - lb) * _sigmoid(proj["hf"][c * c_len:(c + 1) * c_len])
        return 1.0 - f, _cumsum_rows(jnp.log2(f))

    def scores(c, gated):
        k_in, cum = gated
        rows = slice(c * c_len, (c + 1) * c_len)
        mid = cum[c_len // 2 - 1:c_len // 2, :]
        tot = cum[c_len - 1:c_len, :]
        q_mid = proj["hq"][rows] * jnp.exp2(cum - mid)
        k_mid = k_in * jnp.exp2(mid - cum)
        q_dec = (q_mid * jnp.exp2(mid)).astype(BF16)
        k_end = (k_mid * jnp.exp2(tot - mid)).astype(BF16)
        q_mid = q_mid.astype(BF16)
        k_mid = k_mid.astype(BF16)
        v = proj["hi"][rows]
        per_group = []
        for g in groups:
            cols = slice(g * LANES, (g + 1) * LANES)
            a = _dot_nt(stack_heads(q_mid[:, cols]), k_mid[:, cols])
            v_t = v[:, cols].T.astype(BF16)
            upd = _dot(v_t, k_end[:, cols])
            per_group.append((a, v_t, upd, stack_heads(q_dec[:, cols])))
        return per_group, jnp.exp2(tot)

    def output(c, scored, states):
        per_group, decay = scored
        rows = slice(c * c_len, (c + 1) * c_len)
        new_states = []
        for g in groups:
            cols = slice(g * LANES, (g + 1) * LANES)
            a, v_t, upd, q_dec = per_group[g]
            a = jnp.where(causal, a, 0.0).astype(BF16)
            lhs = jnp.concatenate([q_dec, a], axis=1)
            rhs_t = jnp.concatenate([states[g].astype(BF16), v_t], axis=1)
            res = _dot_nt(lhs, rhs_t)
            o = jnp.where(low, res[:c_len], res[c_len:])
            sq = o * o
            msq_lo = jnp.sum(jnp.where(low, sq, 0.0), axis=-1, keepdims=True) * (1.0 / HG_DV)
            msq_hi = jnp.sum(jnp.where(low, 0.0, sq), axis=-1, keepdims=True) * (1.0 / HG_DV)
            inv_rms = jnp.where(low, lax.rsqrt(msq_lo + EPS), lax.rsqrt(msq_hi + EPS))
            o_ref[0, rows, cols] = (o * inv_rms * gain[:, cols]).astype(BF16)
            new_states.append(states[g] * decay[:, cols] + jnp.where(same_head, upd, 0.0))
        return new_states

    return gates, scores, output


def _proj_body(x_ref, gpre_ref, wcat_ref, wkpe_ref, gq_ref, wuq_ref, gkv_ref, wukv_ref, ra_ref, rs_ref,
               bg_ref, lb_ref, ghg_ref, q_ref, k_ref, v_ref, ga_ref, o_ref, gb_ref, gt_ref, st_ref):
    tm = x_ref.shape[1]
    groups = range(HG_HEADS // 2)

    @pl.when(pl.program_id(1) == 0)
    def _():
        st_ref[...] = jnp.zeros_like(st_ref)

    h = (_rms(x_ref[0]) * gpre_ref[...]).astype(BF16)

    def proj(lo, width):
        if lo == COL_KPE:
            return _dot(h, wkpe_ref[...])
        if lo > COL_KPE:
            lo -= HEAD_PAD
        return _dot(h, wcat_ref[:, lo:lo + width])

    ra = ra_ref[...]
    rs = rs_ref[...]
    lane = lax.broadcasted_iota(jnp.int32, ra.shape, 1)
    takes_upper = lane < MLA_NOPE + MLA_ROPE // 2

    def rope(t):
        partner = jnp.where(takes_upper, pltpu.roll(t, LANES - MLA_ROPE // 2, 1), pltpu.roll(t, MLA_ROPE // 2, 1))
        return t * ra + partner * rs

    held = {}

    def q_down():
        held["cqn"] = (_rms(proj(COL_CQ, Q_LORA)) * gq_ref[...]).astype(BF16)

    def q_up(first_head, n_heads):
        scale = math.log2(math.e) / math.sqrt(MLA_QK)
        qu = _dot(held["cqn"], wuq_ref[:, first_head * HEAD_PAD:(first_head + n_heads) * HEAD_PAD])
        for i in range(n_heads):
            q_ref[0, first_head + i] = (rope(qu[:, i * HEAD_PAD:(i + 1) * HEAD_PAD]) * scale).T.astype(BF16)

    def kv_down():
        held["ckvn"] = (_rms(proj(COL_CKV, KV_LORA)) * gkv_ref[...]).astype(BF16)
        held["kpe"] = rope(proj(COL_KPE, HEAD_PAD))

    def kv_up():
        kvu = _dot(held["ckvn"], wukv_ref[...])
        for hh in range(MLA_HEADS):
            k_ref[0, hh] = (kvu[:, hh * HEAD_PAD:(hh + 1) * HEAD_PAD] + held["kpe"]).astype(BF16)
        v_ref[0] = kvu[:, MLA_HEADS * HEAD_PAD:].T.astype(BF16)

    def silu_gate(lo, out_ref):
        z = proj(lo, out_ref.shape[-1])
        out_ref[0] = (z * _sigmoid(z)).astype(BF16)

    def merge_gates(c, width=2 * LANES):
        z = proj(COL_MG + c * width, width) + bg_ref[:, c * width:(c + 1) * width]
        gt_ref[0, :, c * width:(c + 1) * width] = _sigmoid(z).astype(BF16)

    half_heads = MLA_HEADS // 2
    others = [q_down, kv_down, functools.partial(q_up, 0, half_heads), kv_up,
              functools.partial(q_up, half_heads, half_heads),
              functools.partial(silu_gate, COL_GA, ga_ref), functools.partial(silu_gate, COL_GB, gb_ref)]
    pieces = []
    for c in range(D_MODEL // LANES):
        pieces += [functools.partial(merge_gates, c)] + others[c:c + 1]

    def next_piece():
        if pieces:
            pieces.pop(0)()

    rec = {}
    gates, scores, output = _recurrence_stages(rec, lb_ref[...], ghg_ref[...], o_ref)
    n_chunks = tm // C_HGRN
    x = x_ref[0]
    inv_rms = lax.rsqrt(jnp.mean(x * x, axis=-1, keepdims=True) + EPS)
    rec["hf"] = _dot((x * gpre_ref[...]).astype(BF16), wcat_ref[:, COL_HF - HEAD_PAD:COL_HF - HEAD_PAD + HG_WIDTH]) * inv_rms
    gated = {0: gates(0)}
    rec["hq"] = proj(COL_HQ, HG_WIDTH)
    if n_chunks > 1:
        gated[1] = gates(1)
    rec["hi"] = proj(COL_HI, HG_WIDTH)
    scored = {0: scores(0, gated.pop(0))}
    states = [st_ref[g] for g in groups]
    for c in range(n_chunks):
        if c + 2 < n_chunks:
            gated[c + 2] = gates(c + 2)
        next_piece()
        if c + 1 < n_chunks:
            scored[c + 1] = scores(c + 1, gated.pop(c + 1))
        next_piece()
        states = output(c, scored.pop(c), states)
    for g in groups:
        st_ref[g] = states[g]
    while pieces:
        next_piece()


def _proj_call(x, gpre, win, gq, wuq, gkv, wukv, ra, rs, bg, lb, ghg):
    b, s, _ = x.shape
    tm = min(TM_PROJ, s)
    const = lambda bi, si: (0, 0)
    tok = lambda bi, si: (bi, si, 0)

    def full(a):
        return pl.BlockSpec(a.shape, const, pipeline_mode=pl.Buffered(1))

    def tok_out(width, dtype):
        return jax.ShapeDtypeStruct((b, s, width), dtype), pl.BlockSpec((1, tm, width), tok)

    head_shape = jax.ShapeDtypeStruct((b, MLA_HEADS, s, HEAD_PAD), BF16)
    head_spec = pl.BlockSpec((1, MLA_HEADS, tm, HEAD_PAD), lambda bi, si: (bi, 0, si, 0))
    outs = [(jax.ShapeDtypeStruct((b, MLA_HEADS, HEAD_PAD, s), BF16),
             pl.BlockSpec((1, MLA_HEADS, HEAD_PAD, tm), lambda bi, si: (bi, 0, 0, si))),
            (head_shape, head_spec),
            (jax.ShapeDtypeStruct((b, MLA_WIDTH, s), BF16),
             pl.BlockSpec((1, MLA_WIDTH, tm), lambda bi, si: (bi, 0, si))),
            tok_out(MLA_WIDTH, BF16), tok_out(HG_WIDTH, BF16), tok_out(HG_WIDTH, BF16),
            tok_out(2 * D_MODEL, BF16)]
    rope_spec = pl.BlockSpec((tm, HEAD_PAD), lambda bi, si: (si, 0))
    return pl.pallas_call(
        _proj_body,
        grid=(b, s // tm),
        in_specs=[pl.BlockSpec((1, tm, D_MODEL), tok), full(gpre), *(full(w) for w in win), full(gq), full(wuq),
                  full(gkv), full(wukv), rope_spec, rope_spec, full(bg), full(lb), full(ghg)],
        out_specs=[o[1] for o in outs],
        out_shape=[o[0] for o in outs],
        scratch_shapes=[pltpu.VMEM((HG_HEADS // 2, LANES, LANES), F32)],
        compiler_params=pltpu.CompilerParams(dimension_semantics=("parallel", "arbitrary"),
                                             vmem_limit_bytes=VMEM_LIMIT),
        name="proj_hgrn",
    )(x, gpre, *win, gq, wuq, gkv, wukv, ra, rs, bg, lb, ghg)


def _attn_body(qtab_ref, ktab_ref, q_ref, k_ref, vt_ref, o_ref, s_ref, m_ref, acc_ref):
    tq = s_ref.shape[-1]
    n_q = k_ref.shape[2] // tq
    n_off = n_q * (n_q - 1) // 2
    heads = range(HEADS_PER_GROUP)
    key_chunk = lax.broadcasted_iota(jnp.int32, (tq, tq), 0) // CHUNK
    query_chunk = lax.broadcasted_iota(jnp.int32, (tq, tq), 1) // CHUNK
    visible = key_chunk <= query_chunk
    ones = jnp.ones((SUM_ROWS, tq), BF16)

    m_ref[...] = jnp.full(m_ref.shape, -jnp.inf, F32)
    acc_ref[...] = jnp.zeros(acc_ref.shape, F32)

    half = tq // 2
    assert half % CHUNK == 0 and half % LANES == 0
    early, late, everything = slice(0, half), slice(half, tq), slice(0, tq)

    def column_parts(on_diagonal):
        return [(early, early), (late, everything)] if on_diagonal else [(everything, everything)]

    def score(slot, pos, hh, on_diagonal):
        q_off = qtab_ref[pos] * tq
        k_off = ktab_ref[pos] * tq
        part_max = []
        for queries, keys in column_parts(on_diagonal):
            k_rows = pl.ds(pl.multiple_of(k_off + keys.start, half), keys.stop - keys.start)
            q_cols = pl.ds(pl.multiple_of(q_off + queries.start, half), queries.stop - queries.start)
            sc = _dot(k_ref[0, hh, k_rows, :], q_ref[0, hh, :, q_cols])
            if on_diagonal:
                sc = jnp.where(visible[keys, queries], sc, -jnp.inf)
            s_ref[slot, hh, keys, queries] = sc
            part_max.append(jnp.max(sc, axis=0, keepdims=True))
        return jnp.concatenate(part_max, axis=1)

    def absorb(slot, pos, hh, block_max, on_diagonal):
        qi = qtab_ref[pos]
        k_off = ktab_ref[pos] * tq
        m_old = m_ref[qi, hh]
        m_new = jnp.maximum(m_old, block_max)
        m_ref[qi, hh] = m_new
        alpha = jnp.exp2(m_old - m_new)
        for queries, keys in column_parts(on_diagonal):
            p = jnp.exp2(s_ref[slot, hh, keys, queries] - m_new[:, queries]).astype(BF16)
            k_cols = pl.ds(pl.multiple_of(k_off + keys.start, half), keys.stop - keys.start)
            vt = vt_ref[0, hh * MLA_V:(hh + 1) * MLA_V, k_cols]
            vt_ones = jnp.concatenate([vt, ones[:, keys]], axis=0)
            acc_ref[qi, hh, :, queries] = (alpha[:, queries] * acc_ref[qi, hh, :, queries]
                                           + _dot(vt_ones, p))

    def pipeline(first, count, on_diagonal, unroll):
        if count == 0:
            return
        assert unroll % 2 == 0
        n_loop = (count - 1) // unroll

        def several(pos, n, block_max, score_last):
            for u in range(n):
                nxt = []
                for hh in heads:
                    if u + 1 < n or score_last:
                        nxt.append(score((u + 1) % 2, pos + u + 1, hh, on_diagonal))
                    absorb(u % 2, pos + u, hh, block_max[hh], on_diagonal)
                block_max = tuple(nxt)
            return block_max

        first_max = tuple(score(0, first, hh, on_diagonal) for hh in heads)
        block_max = lax.fori_loop(0, n_loop, lambda t, bm: several(first + unroll * t, unroll, bm, True), first_max)
        several(first + unroll * n_loop, count - unroll * n_loop, block_max, False)

    pipeline(0, n_q, True, UNROLL_DIAGONAL)
    pipeline(n_q, n_off, False, UNROLL_VISIBLE)

    def finish(qi, carry):
        out_t = jnp.concatenate([acc_ref[qi, hh, :MLA_V, :] / acc_ref[qi, hh, MLA_V:MLA_V + 1, :] for hh in heads],
                                axis=0)
        o_ref[0, pl.ds(pl.multiple_of(qi * tq, tq), tq), :] = out_t.T.astype(BF16)
        return carry

    lax.fori_loop(0, n_q, finish, 0)


def _attn_call(q, k, vt):
    b, _, s, _ = k.shape
    tq = min(TQ_ATTN, s)
    n_q = s // tq
    off = [(qi, kj) for kj in range(n_q) for qi in range(kj + 1, n_q)]
    order = [(qi, qi) for qi in range(n_q)] + off
    qtab = jnp.asarray([p[0] for p in order], jnp.int32)
    ktab = jnp.asarray([p[1] for p in order], jnp.int32)
    grid_spec = pltpu.PrefetchScalarGridSpec(
        num_scalar_prefetch=2,
        grid=(b, N_GROUPS),
        in_specs=[pl.BlockSpec((1, HEADS_PER_GROUP, HEAD_PAD, s), lambda bi, g, qt, kt: (bi, g, 0, 0)),
                  pl.BlockSpec((1, HEADS_PER_GROUP, s, HEAD_PAD), lambda bi, g, qt, kt: (bi, g, 0, 0)),
                  pl.BlockSpec((1, LANES, s), lambda bi, g, qt, kt: (bi, g, 0))],
        out_specs=pl.BlockSpec((1, s, LANES), lambda bi, g, qt, kt: (bi, 0, g)),
        scratch_shapes=[pltpu.VMEM((2, HEADS_PER_GROUP, tq, tq), F32),
                        pltpu.VMEM((n_q, HEADS_PER_GROUP, 1, tq), F32),
                        pltpu.VMEM((n_q, HEADS_PER_GROUP, MLA_V + SUM_ROWS, tq), F32)])
    return pl.pallas_call(
        _attn_body,
        grid_spec=grid_spec,
        out_shape=jax.ShapeDtypeStruct((b, s, MLA_WIDTH), BF16),
        compiler_params=pltpu.CompilerParams(dimension_semantics=("parallel", "parallel"),
                                             vmem_limit_bytes=VMEM_LIMIT),
        name="attn",
    )(qtab, ktab, q, k, vt)


def _out_body(x_ref, attn_ref, ga_ref, o_ref, gb_ref, gt_ref, wa_ref, wb_ref, wo_ref, gpost_ref, out_ref):
    tm = x_ref.shape[1]
    parts = [slice(i * tm // OUT_PARTS, (i + 1) * tm // OUT_PARTS) for i in range(OUT_PARTS)]

    def merged(r):
        ya = _dot(attn_ref[0, r] * ga_ref[0, r], wa_ref[...])
        yb = _dot(o_ref[0, r] * gb_ref[0, r], wb_ref[...])
        m = gt_ref[0, r, :D_MODEL].astype(F32) * ya + gt_ref[0, r, D_MODEL:].astype(F32) * yb
        return m.astype(BF16)

    ms = [merged(r) for r in parts]
    ys = [_dot(m, wo_ref[...]) for m in ms]
    for r, y in zip(parts, ys):
        out_ref[0, r] = x_ref[0, r] + _rms(y) * gpost_ref[...]


def _out_call(x, attn, ga, o, gb, gt, wa, wb, wo, gpost):
    b, s, _ = x.shape
    tm = min(TM_OUT, s)
    tok = lambda bi, si: (bi, si, 0)
    const = lambda bi, si: (0, 0)

    def tspec(a):
        return pl.BlockSpec((1, tm, a.shape[-1]), tok)

    def full(a):
        return pl.BlockSpec(a.shape, const)

    return pl.pallas_call(
        _out_body,
        grid=(b, s // tm),
        in_specs=[tspec(x), tspec(attn), tspec(ga), tspec(o), tspec(gb), tspec(gt),
                  full(wa), full(wb), full(wo), full(gpost)],
        out_specs=tspec(x),
        out_shape=jax.ShapeDtypeStruct(x.shape, x.dtype),
        compiler_params=pltpu.CompilerParams(dimension_semantics=("parallel", "parallel"),
                                             vmem_limit_bytes=VMEM_LIMIT),
        name="merge_out",
    )(x, attn, ga, o, gb, gt, wa, wb, wo, gpost)


def _rope_tables(s):
    inv = ROPE_THETA ** (-np.arange(0, MLA_ROPE, 2, dtype=np.float64) / MLA_ROPE)
    ang = np.arange(s, dtype=np.float64)[:, None] * inv[None, :]
    cos, sin = np.cos(ang), np.sin(ang)
    pad = np.zeros((s, HEAD_PAD - MLA_QK))
    ra = np.concatenate([np.ones((s, MLA_NOPE)), cos, cos, pad], axis=1)
    rs = np.concatenate([np.zeros((s, MLA_NOPE)), -sin, sin, pad], axis=1)
    return jnp.asarray(ra, F32), jnp.asarray(rs, F32)


def _relayout_body(wt_ref, kpe_src_ref, out_ref, kpe_ref):
    out_ref[...] = wt_ref[...].T.astype(BF16)

    @pl.when(pl.program_id(0) == 0)
    def _():
        t = pltpu.roll(kpe_src_ref[...].T, MLA_NOPE, 1)
        lane = lax.broadcasted_iota(jnp.int32, t.shape, 1)
        kpe_ref[...] = jnp.where((lane >= MLA_NOPE) & (lane < MLA_QK), t, 0.0).astype(BF16)


def _relayout_w_in(w_in):
    d, n = w_in.shape
    kpe_lo = Q_LORA + KV_LORA
    wide_lo = kpe_lo + MLA_ROPE
    cols = W_RELAYOUT_COLS
    assert kpe_lo % cols == 0 and (n - wide_lo) % cols == 0
    n_lat = kpe_lo // cols

    def source_row(i):
        return pl.multiple_of(i * cols + jnp.minimum(i // n_lat, 1) * MLA_ROPE, MLA_ROPE)

    return pl.pallas_call(
        _relayout_body,
        grid=((n - MLA_ROPE) // cols,),
        in_specs=[pl.BlockSpec((pl.Element(cols), pl.Element(d)), lambda i: (source_row(i), 0)),
                  pl.BlockSpec((pl.Element(HEAD_PAD), pl.Element(d)), lambda i: (kpe_lo, 0))],
        out_specs=[pl.BlockSpec((d, cols), lambda i: (0, i)), pl.BlockSpec((d, HEAD_PAD), lambda i: (0, 0))],
        out_shape=[jax.ShapeDtypeStruct((d, n - MLA_ROPE), BF16), jax.ShapeDtypeStruct((d, HEAD_PAD), BF16)],
        compiler_params=pltpu.CompilerParams(dimension_semantics=("arbitrary",), vmem_limit_bytes=VMEM_LIMIT),
        name="relayout_w_in",
    )(w_in.T, w_in.T)


def _pad_weights(w_in, w_uq, w_ukv):
    w_cat, w_kpe = _relayout_w_in(w_in)
    wuq = jnp.pad(w_uq.reshape(Q_LORA, MLA_HEADS, MLA_QK), ((0, 0), (0, 0), (0, HEAD_PAD - MLA_QK)))
    wkv = w_ukv.reshape(KV_LORA, MLA_HEADS, MLA_NOPE + MLA_V)
    wk = jnp.pad(wkv[..., :MLA_NOPE], ((0, 0), (0, 0), (0, HEAD_PAD - MLA_NOPE)))
    wukv = jnp.concatenate([wk.reshape(KV_LORA, MLA_HEADS * HEAD_PAD),
                            wkv[..., MLA_NOPE:].reshape(KV_LORA, MLA_WIDTH)], axis=1)
    return (w_cat, w_kpe), wuq.reshape(Q_LORA, MLA_HEADS * HEAD_PAD).astype(BF16), wukv.astype(BF16)


def kernel(x, g_pre, w_in, b_gate, g_q, w_uq, g_kv, w_ukv, lb_logits, g_hgrn, w_branch_a, w_branch_b, w_out,
           g_post):
    assert g_pre.shape[0] == 1, "single-layer block"
    s = x.shape[1]
    win, wuq, wukv = _pad_weights(w_in[0], w_uq[0], w_ukv[0])
    ra, rs = _rope_tables(s)
    lower_bound = jax.nn.softmax(lb_logits.astype(F32), axis=0)[0:1]
    ghg = jnp.tile(g_hgrn[0], HG_HEADS)[None, :]

    q, k, vt, ga, o, gb, gt = _proj_call(x, g_pre, win, g_q, wuq, g_kv, wukv, ra, rs, b_gate, lower_bound, ghg)
    attn = _attn_call(q, k, vt)
    return _out_call(x, attn, ga, o, gb, gt, w_branch_a[0].astype(BF16), w_branch_b[0].astype(BF16),
                     w_out[0].astype(BF16), g_post)
```

```python
import functools
import math

import jax
import jax.numpy as jnp
import numpy as np
from jax import lax
from jax.experimental import pallas as pl
from jax.experimental.pallas import tpu as pltpu

F32 = jnp.float32
BF16 = jnp.bfloat16

D_MODEL = 1024
CHUNK = 64
EPS = 1e-6

MLA_HEADS = 8
MLA_NOPE = 64
MLA_ROPE = 32
MLA_V = 64
MLA_QK = MLA_NOPE + MLA_ROPE
Q_LORA = 768
KV_LORA = 256
ROPE_THETA = 10000.0
MLA_WIDTH = MLA_HEADS * MLA_V

HG_HEADS = 8
HG_DK = 64
HG_DV = 64
HG_WIDTH = HG_HEADS * HG_DV

LANES = 128
HEAD_PAD = LANES
HEADS_PER_GROUP = LANES // MLA_V
N_GROUPS = MLA_HEADS // HEADS_PER_GROUP

COL_CQ = 0
COL_CKV = COL_CQ + Q_LORA
COL_KPE = COL_CKV + KV_LORA
COL_GA = COL_KPE + HEAD_PAD
COL_HQ = COL_GA + MLA_WIDTH
COL_HF = COL_HQ + HG_WIDTH
COL_HI = COL_HF + HG_WIDTH
COL_GB = COL_HI + HG_WIDTH
COL_MG = COL_GB + HG_WIDTH
D_IN_PAD = COL_MG + 2 * D_MODEL

VMEM_LIMIT = 56 * 1024 * 1024

TM_PROJ = 512
TQ_ATTN = 512
SUM_ROWS = 16
UNROLL_DIAGONAL = 4
UNROLL_VISIBLE = 8
C_HGRN = 64
TM_OUT = 1024
OUT_PARTS = 2
W_RELAYOUT_COLS = 512


def _sigmoid(z):
    return 1.0 / (1.0 + jnp.exp(-z))


def _rms(t):
    return t * lax.rsqrt(jnp.mean(t * t, axis=-1, keepdims=True) + EPS)


def _dot(a, b):
    return jnp.dot(a, b, preferred_element_type=F32)


def _dot_nt(a, b):
    return lax.dot_general(a, b, (((1,), (1,)), ((), ())), preferred_element_type=F32)


def _cumsum_rows(t):
    rows = t.shape[0]
    row = lax.broadcasted_iota(jnp.int32, t.shape, 0)
    step = 1
    while step < rows:
        t = t + jnp.where(row >= step, pltpu.roll(t, step, 0), 0.0)
        step *= 2
    return t


def _recurrence_stages(proj, lb, gain, o_ref):
    c_len = C_HGRN
    groups = range(HG_HEADS // 2)
    low = lax.broadcasted_iota(jnp.int32, (c_len, LANES), 1) < HG_DK
    causal = (lax.broadcasted_iota(jnp.int32, (c_len, 2 * c_len), 0)
              >= lax.broadcasted_iota(jnp.int32, (c_len, 2 * c_len), 1) % c_len)
    own_scores = ((lax.broadcasted_iota(jnp.int32, (LANES, 2 * c_len), 0) < HG_DV)
                  == (lax.broadcasted_iota(jnp.int32, (LANES, 2 * c_len), 1) < c_len))
    same_head = ((lax.broadcasted_iota(jnp.int32, (LANES, LANES), 0) < HG_DV)
                 == (lax.broadcasted_iota(jnp.int32, (LANES, LANES), 1) < HG_DK))

    def stack_heads(t):
        zero = jnp.zeros_like(t)
        return jnp.concatenate([jnp.where(low, t, zero), jnp.where(low, zero, t)], axis=0)

    def gates(c):
        f = lb + (1.0 - lb) * _sigmoid(proj["hf"][c * c_len:(c + 1) * c_len])
        return 1.0 - f, _cumsum_rows(jnp.log2(f))

    def scores(c, gated):
        k_in, cum = gated
        rows = slice(c * c_len, (c + 1) * c_len)
        mid = cum[c_len // 2 - 1:c_len // 2, :]
        tot = cum[c_len - 1:c_len, :]
        q_mid = proj["hq"][rows] * jnp.exp2(cum - mid)
        k_mid = k_in * jnp.exp2(mid - cum)
        q_dec = (q_mid * jnp.exp2(mid)).astype(BF16)
        k_end = (k_mid * jnp.exp2(tot - mid)).astype(BF16)
        q_mid = q_mid.astype(BF16)
        k_mid = k_mid.astype(BF16)
        v = proj["hi"][rows]
        per_group = []
        for g in groups:
            cols = slice(g * LANES, (g + 1) * LANES)
            a = _dot_nt(q_mid[:, cols], stack_heads(k_mid[:, cols]))
            vv = v[:, cols]
            v_tt = jnp.concatenate([vv, vv], axis=0).T
            upd = _dot(v_tt[:, :c_len].astype(BF16), k_end[:, cols])
            v_heads = jnp.where(own_scores, v_tt, 0.0).astype(BF16)
            per_group.append((a, v_heads, upd, q_dec[:, cols]))
        return per_group, jnp.exp2(tot)

    def output(c, scored, states):
        per_group, decay = scored
        rows = slice(c * c_len, (c + 1) * c_len)
        new_states = []
        for g in groups:
            cols = slice(g * LANES, (g + 1) * LANES)
            a, v_heads, upd, q_dec = per_group[g]
            a = jnp.where(causal, a, 0.0).astype(BF16)
            lhs = jnp.concatenate([q_dec, a], axis=1)
            rhs_t = jnp.concatenate([states[g].astype(BF16), v_heads], axis=1)
            o = _dot_nt(lhs, rhs_t)
            sq = o * o
            msq_lo = jnp.sum(jnp.where(low, sq, 0.0), axis=-1, keepdims=True) * (1.0 / HG_DV)
            msq_hi = jnp.sum(jnp.where(low, 0.0, sq), axis=-1, keepdims=True) * (1.0 / HG_DV)
            inv_rms = jnp.where(low, lax.rsqrt(msq_lo + EPS), lax.rsqrt(msq_hi + EPS))
            o_ref[0, rows, cols] = (o * inv_rms * gain[:, cols]).astype(BF16)
            new_states.append(states[g] * decay[:, cols] + jnp.where(same_head, upd, 0.0))
        return new_states

    return gates, scores, output


def _proj_body(x_ref, gpre_ref, wcat_ref, wkpe_ref, gq_ref, wuq_ref, gkv_ref, wukv_ref, ra_ref, rs_ref,
               bg_ref, lb_ref, ghg_ref, q_ref, k_ref, v_ref, ga_ref, o_ref, gb_ref, gt_ref, st_ref):
    tm = x_ref.shape[1]
    groups = range(HG_HEADS // 2)

    @pl.when(pl.program_id(1) == 0)
    def _():
        st_ref[...] = jnp.zeros_like(st_ref)

    h = (_rms(x_ref[0]) * gpre_ref[...]).astype(BF16)

    def proj(lo, width):
        if lo == COL_KPE:
            return _dot(h, wkpe_ref[...])
        if lo > COL_KPE:
            lo -= HEAD_PAD
        return _dot(h, wcat_ref[:, lo:lo + width])

    ra = ra_ref[...]
    rs = rs_ref[...]
    lane = lax.broadcasted_iota(jnp.int32, ra.shape, 1)
    takes_upper = lane < MLA_NOPE + MLA_ROPE // 2

    def rope(t):
        partner = jnp.where(takes_upper, pltpu.roll(t, LANES - MLA_ROPE // 2, 1), pltpu.roll(t, MLA_ROPE // 2, 1))
        return t * ra + partner * rs

    held = {}

    def q_down():
        held["cqn"] = (_rms(proj(COL_CQ, Q_LORA)) * gq_ref[...]).astype(BF16)

    def q_up(first_head, n_heads):
        scale = math.log2(math.e) / math.sqrt(MLA_QK)
        qu = _dot(held["cqn"], wuq_ref[:, first_head * HEAD_PAD:(first_head + n_heads) * HEAD_PAD])
        for i in range(n_heads):
            q_ref[0, first_head + i] = (rope(qu[:, i * HEAD_PAD:(i + 1) * HEAD_PAD]) * scale).T.astype(BF16)

    def kv_down():
        held["ckvn"] = (_rms(proj(COL_CKV, KV_LORA)) * gkv_ref[...]).astype(BF16)
        held["kpe"] = rope(proj(COL_KPE, HEAD_PAD))

    def kv_up():
        kvu = _dot(held["ckvn"], wukv_ref[...])
        for hh in range(MLA_HEADS):
            k_ref[0, hh] = (kvu[:, hh * HEAD_PAD:(hh + 1) * HEAD_PAD] + held["kpe"]).astype(BF16)
        v_ref[0] = kvu[:, MLA_HEADS * HEAD_PAD:].T.astype(BF16)

    def silu_gate(lo, out_ref):
        z = proj(lo, out_ref.shape[-1])
        out_ref[0] = (z * _sigmoid(z)).astype(BF16)

    def merge_gates(c, width=2 * LANES):
        z = proj(COL_MG + c * width, width) + bg_ref[:, c * width:(c + 1) * width]
        gt_ref[0, :, c * width:(c + 1) * width] = _sigmoid(z).astype(BF16)

    half_heads = MLA_HEADS // 2
    others = [q_down, kv_down, functools.partial(q_up, 0, half_heads), kv_up,
              functools.partial(q_up, half_heads, half_heads),
              functools.partial(silu_gate, COL_GA, ga_ref), functools.partial(silu_gate, COL_GB, gb_ref)]
    pieces = []
    for c in range(D_MODEL // LANES):
        pieces += [functools.partial(merge_gates, c)] + others[c:c + 1]

    def next_piece():
        if pieces:
            pieces.pop(0)()

    rec = {}
    gates, scores, output = _recurrence_stages(rec, lb_ref[...], ghg_ref[...], o_ref)
    n_chunks = tm // C_HGRN
    x = x_ref[0]
    inv_rms = lax.rsqrt(jnp.mean(x * x, axis=-1, keepdims=True) + EPS)
    rec["hf"] = _dot((x * gpre_ref[...]).astype(BF16), wcat_ref[:, COL_HF - HEAD_PAD:COL_HF - HEAD_PAD + HG_WIDTH]) * inv_rms
    gated = {0: gates(0)}
    rec["hq"] = proj(COL_HQ, HG_WIDTH)
    if n_chunks > 1:
        gated[1] = gates(1)
    rec["hi"] = proj(COL_HI, HG_WIDTH)
    scored = {0: scores(0, gated.pop(0))}
    states = [st_ref[g] for g in groups]
    for c in range(n_chunks):
        if c + 2 < n_chunks:
            gated[c + 2] = gates(c + 2)
        next_piece()
        if c + 1 < n_chunks:
            scored[c + 1] = scores(c + 1, gated.pop(c + 1))
        next_piece()
        states = output(c, scored.pop(c), states)
    for g in groups:
        st_ref[g] = states[g]
    while pieces:
        next_piece()


def _proj_call(x, gpre, win, gq, wuq, gkv, wukv, ra, rs, bg, lb, ghg):
    b, s, _ = x.shape
    tm = min(TM_PROJ, s)
    const = lambda bi, si: (0, 0)
    tok = lambda bi, si: (bi, si, 0)

    def full(a):
        return pl.BlockSpec(a.shape, const, pipeline_mode=pl.Buffered(1))

    def tok_out(width, dtype):
        return jax.ShapeDtypeStruct((b, s, width), dtype), pl.BlockSpec((1, tm, width), tok)

    head_shape = jax.ShapeDtypeStruct((b, MLA_HEADS, s, HEAD_PAD), BF16)
    head_spec = pl.BlockSpec((1, MLA_HEADS, tm, HEAD_PAD), lambda bi, si: (bi, 0, si, 0))
    outs = [(jax.ShapeDtypeStruct((b, MLA_HEADS, HEAD_PAD, s), BF16),
             pl.BlockSpec((1, MLA_HEADS, HEAD_PAD, tm), lambda bi, si: (bi, 0, 0, si))),
            (head_shape, head_spec),
            (jax.ShapeDtypeStruct((b, MLA_WIDTH, s), BF16),
             pl.BlockSpec((1, MLA_WIDTH, tm), lambda bi, si: (bi, 0, si))),
            tok_out(MLA_WIDTH, BF16), tok_out(HG_WIDTH, BF16), tok_out(HG_WIDTH, BF16),
            tok_out(2 * D_MODEL, BF16)]
    rope_spec = pl.BlockSpec((tm, HEAD_PAD), lambda bi, si: (si, 0))
    return pl.pallas_call(
        _proj_body,
        grid=(b, s // tm),
        in_specs=[pl.BlockSpec((1, tm, D_MODEL), tok), full(gpre), *(full(w) for w in win), full(gq), full(wuq),
                  full(gkv), full(wukv), rope_spec, rope_spec, full(bg), full(lb), full(ghg)],
        out_specs=[o[1] for o in outs],
        out_shape=[o[0] for o in outs],
        scratch_shapes=[pltpu.VMEM((HG_HEADS // 2, LANES, LANES), F32)],
        compiler_params=pltpu.CompilerParams(dimension_semantics=("parallel", "arbitrary"),
                                             vmem_limit_bytes=VMEM_LIMIT),
        name="proj_hgrn",
    )(x, gpre, *win, gq, wuq, gkv, wukv, ra, rs, bg, lb, ghg)


def _attn_body(qtab_ref, ktab_ref, q_ref, k_ref, vt_ref, o_ref, s_ref, m_ref, acc_ref):
    tq = s_ref.shape[-1]
    n_q = k_ref.shape[2] // tq
    n_off = n_q * (n_q - 1) // 2
    heads = range(HEADS_PER_GROUP)
    key_chunk = lax.broadcasted_iota(jnp.int32, (tq, tq), 0) // CHUNK
    query_chunk = lax.broadcasted_iota(jnp.int32, (tq, tq), 1) // CHUNK
    visible = key_chunk <= query_chunk
    ones = jnp.ones((SUM_ROWS, tq), BF16)

    m_ref[...] = jnp.full(m_ref.shape, -jnp.inf, F32)
    acc_ref[...] = jnp.zeros(acc_ref.shape, F32)

    half = tq // 2
    assert half % CHUNK == 0 and half % LANES == 0
    early, late, everything = slice(0, half), slice(half, tq), slice(0, tq)

    def column_parts(on_diagonal):
        return [(early, early), (late, everything)] if on_diagonal else [(everything, everything)]

    def score(slot, pos, hh, on_diagonal):
        q_off = qtab_ref[pos] * tq
        k_off = ktab_ref[pos] * tq
        part_max = []
        for queries, keys in column_parts(on_diagonal):
            k_rows = pl.ds(pl.multiple_of(k_off + keys.start, half), keys.stop - keys.start)
            q_cols = pl.ds(pl.multiple_of(q_off + queries.start, half), queries.stop - queries.start)
            sc = _dot(k_ref[0, hh, k_rows, :], q_ref[0, hh, :, q_cols])
            if on_diagonal:
                sc = jnp.where(visible[keys, queries], sc, -jnp.inf)
            s_ref[slot, hh, keys, queries] = sc
            part_max.append(jnp.max(sc, axis=0, keepdims=True))
        return jnp.concatenate(part_max, axis=1)

    def absorb(slot, pos, hh, block_max, on_diagonal):
        qi = qtab_ref[pos]
        k_off = ktab_ref[pos] * tq
        m_old = m_ref[qi, hh]
        m_new = jnp.maximum(m_old, block_max)
        m_ref[qi, hh] = m_new
        alpha = jnp.exp2(m_old - m_new)
        for queries, keys in column_parts(on_diagonal):
            p = jnp.exp2(s_ref[slot, hh, keys, queries] - m_new[:, queries]).astype(BF16)
            k_cols = pl.ds(pl.multiple_of(k_off + keys.start, half), keys.stop - keys.start)
            vt = vt_ref[0, hh * MLA_V:(hh + 1) * MLA_V, k_cols]
            vt_ones = jnp.concatenate([vt, ones[:, keys]], axis=0)
            acc_ref[qi, hh, :, queries] = (alpha[:, queries] * acc_ref[qi, hh, :, queries]
                                           + _dot(vt_ones, p))

    def pipeline(first, count, on_diagonal, unroll):
        if count == 0:
            return
        assert unroll % 2 == 0
        n_loop = (count - 1) // unroll

        def several(pos, n, block_max, score_last):
            for u in range(n):
                nxt = []
                for hh in heads:
                    if u + 1 < n or score_last:
                        nxt.append(score((u + 1) % 2, pos + u + 1, hh, on_diagonal))
                    absorb(u % 2, pos + u, hh, block_max[hh], on_diagonal)
                block_max = tuple(nxt)
            return block_max

        first_max = tuple(score(0, first, hh, on_diagonal) for hh in heads)
        block_max = lax.fori_loop(0, n_loop, lambda t, bm: several(first + unroll * t, unroll, bm, True), first_max)
        several(first + unroll * n_loop, count - unroll * n_loop, block_max, False)

    pipeline(0, n_q, True, UNROLL_DIAGONAL)
    pipeline(n_q, n_off, False, UNROLL_VISIBLE)

    def finish(qi, carry):
        out_t = jnp.concatenate([acc_ref[qi, hh, :MLA_V, :] / acc_ref[qi, hh, MLA_V:MLA_V + 1, :] for hh in heads],
                                axis=0)
        o_ref[0, pl.ds(pl.multiple_of(qi * tq, tq), tq), :] = out_t.T.astype(BF16)
        return carry

    lax.fori_loop(0, n_q, finish, 0)


def _attn_call(q, k, vt):
    b, _, s, _ = k.shape
    tq = min(TQ_ATTN, s)
    n_q = s // tq
    off = [(qi, kj) for kj in range(n_q) for qi in range(kj + 1, n_q)]
    order = [(qi, qi) for qi in range(n_q)] + off
    qtab = jnp.asarray([p[0] for p in order], jnp.int32)
    ktab = jnp.asarray([p[1] for p in order], jnp.int32)
    grid_spec = pltpu.PrefetchScalarGridSpec(
        num_scalar_prefetch=2,
        grid=(b, N_GROUPS),
        in_specs=[pl.BlockSpec((1, HEADS_PER_GROUP, HEAD_PAD, s), lambda bi, g, qt, kt: (bi, g, 0, 0)),
                  pl.BlockSpec((1, HEADS_PER_GROUP, s, HEAD_PAD), lambda bi, g, qt, kt: (bi, g, 0, 0)),
                  pl.BlockSpec((1, LANES, s), lambda bi, g, qt, kt: (bi, g, 0))],
        out_specs=pl.BlockSpec((1, s, LANES), lambda bi, g, qt, kt: (bi, 0, g)),
        scratch_shapes=[pltpu.VMEM((2, HEADS_PER_GROUP, tq, tq), F32),
                        pltpu.VMEM((n_q, HEADS_PER_GROUP, 1, tq), F32),
                        pltpu.VMEM((n_q, HEADS_PER_GROUP, MLA_V + SUM_ROWS, tq), F32)])
    return pl.pallas_call(
        _attn_body,
        grid_spec=grid_spec,
        out_shape=jax.ShapeDtypeStruct((b, s, MLA_WIDTH), BF16),
        compiler_params=pltpu.CompilerParams(dimension_semantics=("parallel", "parallel"),
                                             vmem_limit_bytes=VMEM_LIMIT),
        name="attn",
    )(qtab, ktab, q, k, vt)


def _out_body(x_ref, attn_ref, ga_ref, o_ref, gb_ref, gt_ref, wa_ref, wb_ref, wo_ref, gpost_ref, out_ref):
    tm = x_ref.shape[1]
    parts = [slice(i * tm // OUT_PARTS, (i + 1) * tm // OUT_PARTS) for i in range(OUT_PARTS)]

    def merged(r):
        ya = _dot(attn_ref[0, r] * ga_ref[0, r], wa_ref[...])
        yb = _dot(o_ref[0, r] * gb_ref[0, r], wb_ref[...])
        m = gt_ref[0, r, :D_MODEL].astype(F32) * ya + gt_ref[0, r, D_MODEL:].astype(F32) * yb
        return m.astype(BF16)

    ms = [merged(r) for r in parts]
    ys = [_dot(m, wo_ref[...]) for m in ms]
    for r, y in zip(parts, ys):
        out_ref[0, r] = x_ref[0, r] + _rms(y) * gpost_ref[...]


def _out_call(x, attn, ga, o, gb, gt, wa, wb, wo, gpost):
    b, s, _ = x.shape
    tm = min(TM_OUT, s)
    tok = lambda bi, si: (bi, si, 0)
    const = lambda bi, si: (0, 0)

    def tspec(a):
        return pl.BlockSpec((1, tm, a.shape[-1]), tok)

    def full(a):
        return pl.BlockSpec(a.shape, const)

    return pl.pallas_call(
        _out_body,
        grid=(b, s // tm),
        in_specs=[tspec(x), tspec(attn), tspec(ga), tspec(o), tspec(gb), tspec(gt),
                  full(wa), full(wb), full(wo), full(gpost)],
        out_specs=tspec(x),
        out_shape=jax.ShapeDtypeStruct(x.shape, x.dtype),
        compiler_params=pltpu.CompilerParams(dimension_semantics=("parallel", "parallel"),
                                             vmem_limit_bytes=VMEM_LIMIT),
        name="merge_out",
    )(x, attn, ga, o, gb, gt, wa, wb, wo, gpost)


def _rope_tables(s):
    inv = ROPE_THETA ** (-np.arange(0, MLA_ROPE, 2, dtype=np.float64) / MLA_ROPE)
    ang = np.arange(s, dtype=np.float64)[:, None] * inv[None, :]
    cos, sin = np.cos(ang), np.sin(ang)
    pad = np.zeros((s, HEAD_PAD - MLA_QK))
    ra = np.concatenate([np.ones((s, MLA_NOPE)), cos, cos, pad], axis=1)
    rs = np.concatenate([np.zeros((s, MLA_NOPE)), -sin, sin, pad], axis=1)
    return jnp.asarray(ra, F32), jnp.asarray(rs, F32)


def _relayout_body(wt_ref, kpe_src_ref, out_ref, kpe_ref):
    out_ref[...] = wt_ref[...].T.astype(BF16)

    @pl.when(pl.program_id(0) == 0)
    def _():
        t = pltpu.roll(kpe_src_ref[...].T, MLA_NOPE, 1)
        lane = lax.broadcasted_iota(jnp.int32, t.shape, 1)
        kpe_ref[...] = jnp.where((lane >= MLA_NOPE) & (lane < MLA_QK), t, 0.0).astype(BF16)


def _relayout_w_in(w_in):
    d, n = w_in.shape
    kpe_lo = Q_LORA + KV_LORA
    wide_lo = kpe_lo + MLA_ROPE
    cols = W_RELAYOUT_COLS
    assert kpe_lo % cols == 0 and (n - wide_lo) % cols == 0
    n_lat = kpe_lo // cols

    def source_row(i):
        return pl.multiple_of(i * cols + jnp.minimum(i // n_lat, 1) * MLA_ROPE, MLA_ROPE)

    return pl.pallas_call(
        _relayout_body,
        grid=((n - MLA_ROPE) // cols,),
        in_specs=[pl.BlockSpec((pl.Element(cols), pl.Element(d)), lambda i: (source_row(i), 0)),
                  pl.BlockSpec((pl.Element(HEAD_PAD), pl.Element(d)), lambda i: (kpe_lo, 0))],
        out_specs=[pl.BlockSpec((d, cols), lambda i: (0, i)), pl.BlockSpec((d, HEAD_PAD), lambda i: (0, 0))],
        out_shape=[jax.ShapeDtypeStruct((d, n - MLA_ROPE), BF16), jax.ShapeDtypeStruct((d, HEAD_PAD), BF16)],
        compiler_params=pltpu.CompilerParams(dimension_semantics=("arbitrary",), vmem_limit_bytes=VMEM_LIMIT),
        name="relayout_w_in",
    )(w_in.T, w_in.T)


def _pad_weights(w_in, w_uq, w_ukv):
    w_cat, w_kpe = _relayout_w_in(w_in)
    wuq = jnp.pad(w_uq.reshape(Q_LORA, MLA_HEADS, MLA_QK), ((0, 0), (0, 0), (0, HEAD_PAD - MLA_QK)))
    wkv = w_ukv.reshape(KV_LORA, MLA_HEADS, MLA_NOPE + MLA_V)
    wk = jnp.pad(wkv[..., :MLA_NOPE], ((0, 0), (0, 0), (0, HEAD_PAD - MLA_NOPE)))
    wukv = jnp.concatenate([wk.reshape(KV_LORA, MLA_HEADS * HEAD_PAD),
                            wkv[..., MLA_NOPE:].reshape(KV_LORA, MLA_WIDTH)], axis=1)
    return (w_cat, w_kpe), wuq.reshape(Q_LORA, MLA_HEADS * HEAD_PAD).astype(BF16), wukv.astype(BF16)


def kernel(x, g_pre, w_in, b_gate, g_q, w_uq, g_kv, w_ukv, lb_logits, g_hgrn, w_branch_a, w_branch_b, w_out,
           g_post):
    assert g_pre.shape[0] == 1, "single-layer block"
    s = x.shape[1]
    win, wuq, wukv = _pad_weights(w_in[0], w_uq[0], w_ukv[0])
    ra, rs = _rope_tables(s)
    lower_bound = jax.nn.softmax(lb_logits.astype(F32), axis=0)[0:1]
    ghg = jnp.tile(g_hgrn[0], HG_HEADS)[None, :]

    q, k, vt, ga, o, gb, gt = _proj_call(x, g_pre, win, g_q, wuq, g_kv, wukv, ra, rs, b_gate, lower_bound, ghg)
    attn = _attn_call(q, k, vt)
    return _out_call(x, attn, ga, o, gb, gt, w_branch_a[0].astype(BF16), w_branch_b[0].astype(BF16),
                     w_out[0].astype(BF16), g_post)
```

```python
import functools
import math

import jax
import jax.numpy as jnp
import numpy as np
from jax import lax
from jax.experimental import pallas as pl
from jax.experimental.pallas import tpu as pltpu

F32 = jnp.float32
BF16 = jnp.bfloat16

D_MODEL = 1024
CHUNK = 64
EPS = 1e-6

MLA_HEADS = 8
MLA_NOPE = 64
MLA_ROPE = 32
MLA_V = 64
MLA_QK = MLA_NOPE + MLA_ROPE
Q_LORA = 768
KV_LORA = 256
ROPE_THETA = 10000.0
MLA_WIDTH = MLA_HEADS * MLA_V

HG_HEADS = 8
HG_DK = 64
HG_DV = 64
HG_WIDTH = HG_HEADS * HG_DV

LANES = 128
HEAD_PAD = LANES
HEADS_PER_GROUP = LANES // MLA_V
N_GROUPS = MLA_HEADS // HEADS_PER_GROUP

COL_CQ = 0
COL_CKV = COL_CQ + Q_LORA
COL_KPE = COL_CKV + KV_LORA
COL_GA = COL_KPE + HEAD_PAD
COL_HQ = COL_GA + MLA_WIDTH
COL_HF = COL_HQ + HG_WIDTH
COL_HI = COL_HF + HG_WIDTH
COL_GB = COL_HI + HG_WIDTH
COL_MG = COL_GB + HG_WIDTH
D_IN_PAD = COL_MG + 2 * D_MODEL

VMEM_LIMIT = 56 * 1024 * 1024

TM_PROJ = 512
TQ_ATTN = 512
SUM_ROWS = 16
UNROLL_DIAGONAL = 4
UNROLL_VISIBLE = 8
C_HGRN = 64
TM_OUT = 1024
OUT_PARTS = 2
W_RELAYOUT_COLS = 512


def _sigmoid(z):
    return 1.0 / (1.0 + jnp.exp(-z))


def _rms(t):
    return t * lax.rsqrt(jnp.mean(t * t, axis=-1, keepdims=True) + EPS)


def _dot(a, b):
    return jnp.dot(a, b, preferred_element_type=F32)


def _dot_nt(a, b):
    return lax.dot_general(a, b, (((1,), (1,)), ((), ())), preferred_element_type=F32)


def _cumsum_rows(t):
    rows = t.shape[0]
    row = lax.broadcasted_iota(jnp.int32, t.shape, 0)
    step = 1
    while step < rows:
        t = t + jnp.where(row >= step, pltpu.roll(t, step, 0), 0.0)
        step *= 2
    return t


def _recurrence_stages(proj, lb, gain, o_ref):
    c_len = C_HGRN
    groups = range(HG_HEADS // 2)
    low = lax.broadcasted_iota(jnp.int32, (c_len, LANES), 1) < HG_DK
    causal = (lax.broadcasted_iota(jnp.int32, (2 * c_len, c_len), 0) % c_len
              >= lax.broadcasted_iota(jnp.int32, (2 * c_len, c_len), 1))
    same_head = ((lax.broadcasted_iota(jnp.int32, (LANES, LANES), 0) < HG_DV)
                 == (lax.broadcasted_iota(jnp.int32, (LANES, LANES), 1) < HG_DK))

    def stack_heads(t):
        zero = jnp.zeros_like(t)
        return jnp.concatenate([jnp.where(low, t, zero), jnp.where(low, zero, t)], axis=0)

    def gates(c):
        f = lb + (1.0 - lb) * _sigmoid(proj["hf"][c * c_len:(c + 1) * c_len])
        return 1.0 - f, _cumsum_rows(jnp.log2(f))

    def scores(c, gated):
        k_in, cum = gated
        rows = slice(c * c_len, (c + 1) * c_len)
        mid = cum[c_len // 2 - 1:c_len // 2, :]
        tot = cum[c_len - 1:c_len, :]
        q_mid = proj["hq"][rows] * jnp.exp2(cum - mid)
        k_mid = k_in * jnp.exp2(mid - cum)
        q_dec = (q_mid * jnp.exp2(mid)).astype(BF16)
        k_end = (k_mid * jnp.exp2(tot - mid)).astype(BF16)
        q_mid = q_mid.astype(BF16)
        k_mid = k_mid.astype(BF16)
        v = proj["hi"][rows]
        per_group = []
        for g in groups:
            cols = slice(g * LANES, (g + 1) * LANES)
            a = _dot_nt(stack_heads(q_mid[:, cols]), k_mid[:, cols])
            v_t = v[:, cols].T.astype(BF16)
            upd = _dot(v_t, k_end[:, cols])
            per_group.append((a, v_t, upd, stack_heads(q_dec[:, cols])))
        return per_group, jnp.exp2(tot)

    def output(c, scored, states):
        per_group, decay = scored
        rows = slice(c * c_len, (c + 1) * c_len)
        new_states = []
        for g in groups:
            cols = slice(g * LANES, (g + 1) * LANES)
            a, v_t, upd, q_dec = per_group[g]
            a = jnp.where(causal, a, 0.0).astype(BF16)
            lhs = jnp.concatenate([q_dec, a], axis=1)
            rhs_t = jnp.concatenate([states[g].astype(BF16), v_t], axis=1)
            res = _dot_nt(lhs, rhs_t)
            o = jnp.where(low, res[:c_len], res[c_len:])
            sq = o * o
            msq_lo = jnp.sum(jnp.where(low, sq, 0.0), axis=-1, keepdims=True) * (1.0 / HG_DV)
            msq_hi = jnp.sum(jnp.where(low, 0.0, sq), axis=-1, keepdims=True) * (1.0 / HG_DV)
            inv_rms = jnp.where(low, lax.rsqrt(msq_lo + EPS), lax.rsqrt(msq_hi + EPS))
            o_ref[0, rows, cols] = (o * inv_rms * gain[:, cols]).astype(BF16)
            new_states.append(states[g] * decay[:, cols] + jnp.where(same_head, upd, 0.0))
        return new_states

    return gates, scores, output


def _proj_body(x_ref, gpre_ref, wcat_ref, wkpe_ref, gq_ref, wuq_ref, gkv_ref, wukv_ref, ra_ref, rs_ref,
               bg_ref, lb_ref, ghg_ref, q_ref, k_ref, v_ref, ga_ref, o_ref, gb_ref, gt_ref, st_ref):
    tm = x_ref.shape[1]
    groups = range(HG_HEADS // 2)

    @pl.when(pl.program_id(1) == 0)
    def _():
        st_ref[...] = jnp.zeros_like(st_ref)

    h = (_rms(x_ref[0]) * gpre_ref[...]).astype(BF16)

    def proj(lo, width):
        if lo == COL_KPE:
            return _dot(h, wkpe_ref[...])
        if lo > COL_KPE:
            lo -= HEAD_PAD
        return _dot(h, wcat_ref[:, lo:lo + width])

    ra = ra_ref[...]
    rs = rs_ref[...]
    lane = lax.broadcasted_iota(jnp.int32, ra.shape, 1)
    takes_upper = lane < MLA_NOPE + MLA_ROPE // 2

    def rope(t):
        partner = jnp.where(takes_upper, pltpu.roll(t, LANES - MLA_ROPE // 2, 1), pltpu.roll(t, MLA_ROPE // 2, 1))
        return t * ra + partner * rs

    held = {}

    def q_down():
        held["cqn"] = (_rms(proj(COL_CQ, Q_LORA)) * gq_ref[...]).astype(BF16)

    def q_up(first_head, n_heads):
        scale = math.log2(math.e) / math.sqrt(MLA_QK)
        qu = _dot(held["cqn"], wuq_ref[:, first_head * HEAD_PAD:(first_head + n_heads) * HEAD_PAD])
        for i in range(n_heads):
            q_ref[0, first_head + i] = (rope(qu[:, i * HEAD_PAD:(i + 1) * HEAD_PAD]) * scale).T.astype(BF16)

    def kv_down():
        held["ckvn"] = (_rms(proj(COL_CKV, KV_LORA)) * gkv_ref[...]).astype(BF16)
        held["kpe"] = rope(proj(COL_KPE, HEAD_PAD))

    def kv_up():
        kvu = _dot(held["ckvn"], wukv_ref[...])
        for hh in range(MLA_HEADS):
            k_ref[0, hh] = (kvu[:, hh * HEAD_PAD:(hh + 1) * HEAD_PAD] + held["kpe"]).astype(BF16)
        v_ref[0] = kvu[:, MLA_HEADS * HEAD_PAD:].T.astype(BF16)

    def silu_gate(lo, out_ref):
        z = proj(lo, out_ref.shape[-1])
        out_ref[0] = (z * _sigmoid(z)).astype(BF16)

    def merge_gates(c, width=2 * LANES):
        z = proj(COL_MG + c * width, width) + bg_ref[:, c * width:(c + 1) * width]
        gt_ref[0, :, c * width:(c + 1) * width] = _sigmoid(z).astype(BF16)

    half_heads = MLA_HEADS // 2
    others = [q_down, kv_down, functools.partial(q_up, 0, half_heads), kv_up,
              functools.partial(q_up, half_heads, half_heads),
              functools.partial(silu_gate, COL_GA, ga_ref), functools.partial(silu_gate, COL_GB, gb_ref)]
    pieces = []
    for c in range(D_MODEL // LANES):
        pieces += [functools.partial(merge_gates, c)] + others[c:c + 1]

    def next_piece():
        if pieces:
            pieces.pop(0)()

    rec = {}
    gates, scores, output = _recurrence_stages(rec, lb_ref[...], ghg_ref[...], o_ref)
    n_chunks = tm // C_HGRN
    x = x_ref[0]
    inv_rms = lax.rsqrt(jnp.mean(x * x, axis=-1, keepdims=True) + EPS)
    rec["hf"] = _dot((x * gpre_ref[...]).astype(BF16), wcat_ref[:, COL_HF - HEAD_PAD:COL_HF - HEAD_PAD + HG_WIDTH]) * inv_rms
    gated = {0: gates(0)}
    rec["hq"] = proj(COL_HQ, HG_WIDTH)
    if n_chunks > 1:
        gated[1] = gates(1)
    rec["hi"] = proj(COL_HI, HG_WIDTH)
    scored = {0: scores(0, gated.pop(0))}
    states = [st_ref[g] for g in groups]
    for c in range(n_chunks):
        if c + 2 < n_chunks:
            gated[c + 2] = gates(c + 2)
        next_piece()
        if c + 1 < n_chunks:
            scored[c + 1] = scores(c + 1, gated.pop(c + 1))
        next_piece()
        states = output(c, scored.pop(c), states)
    for g in groups:
        st_ref[g] = states[g]
    while pieces:
        next_piece()


def _proj_call(x, gpre, win, gq, wuq, gkv, wukv, ra, rs, bg, lb, ghg):
    b, s, _ = x.shape
    tm = min(TM_PROJ, s)
    const = lambda bi, si: (0, 0)
    tok = lambda bi, si: (bi, si, 0)

    def full(a):
        return pl.BlockSpec(a.shape, const, pipeline_mode=pl.Buffered(1))

    def tok_out(width, dtype):
        return jax.ShapeDtypeStruct((b, s, width), dtype), pl.BlockSpec((1, tm, width), tok)

    head_shape = jax.ShapeDtypeStruct((b, MLA_HEADS, s, HEAD_PAD), BF16)
    head_spec = pl.BlockSpec((1, MLA_HEADS, tm, HEAD_PAD), lambda bi, si: (bi, 0, si, 0))
    outs = [(jax.ShapeDtypeStruct((b, MLA_HEADS, HEAD_PAD, s), BF16),
             pl.BlockSpec((1, MLA_HEADS, HEAD_PAD, tm), lambda bi, si: (bi, 0, 0, si))),
            (head_shape, head_spec),
            (jax.ShapeDtypeStruct((b, MLA_WIDTH, s), BF16),
             pl.BlockSpec((1, MLA_WIDTH, tm), lambda bi, si: (bi, 0, si))),
            tok_out(MLA_WIDTH, BF16), tok_out(HG_WIDTH, BF16), tok_out(HG_WIDTH, BF16),
            tok_out(2 * D_MODEL, BF16)]
    rope_spec = pl.BlockSpec((tm, HEAD_PAD), lambda bi, si: (si, 0))
    return pl.pallas_call(
        _proj_body,
        grid=(b, s // tm),
        in_specs=[pl.BlockSpec((1, tm, D_MODEL), tok), full(gpre), *(full(w) for w in win), full(gq), full(wuq),
                  full(gkv), full(wukv), rope_spec, rope_spec, full(bg), full(lb), full(ghg)],
        out_specs=[o[1] for o in outs],
        out_shape=[o[0] for o in outs],
        scratch_shapes=[pltpu.VMEM((HG_HEADS // 2, LANES, LANES), F32)],
        compiler_params=pltpu.CompilerParams(dimension_semantics=("parallel", "arbitrary"),
                                             vmem_limit_bytes=VMEM_LIMIT),
        name="proj_hgrn",
    )(x, gpre, *win, gq, wuq, gkv, wukv, ra, rs, bg, lb, ghg)


def _attn_body(qtab_ref, ktab_ref, q_ref, k_ref, vt_ref, o_ref, s_ref, m_ref, acc_ref):
    tq = s_ref.shape[-1]
    n_q = k_ref.shape[2] // tq
    n_off = n_q * (n_q - 1) // 2
    heads = range(HEADS_PER_GROUP)
    key_chunk = lax.broadcasted_iota(jnp.int32, (tq, tq), 0) // CHUNK
    query_chunk = lax.broadcasted_iota(jnp.int32, (tq, tq), 1) // CHUNK
    visible = key_chunk <= query_chunk
    ones = jnp.ones((SUM_ROWS, tq), BF16)

    m_ref[...] = jnp.full(m_ref.shape, -jnp.inf, F32)
    acc_ref[...] = jnp.zeros(acc_ref.shape, F32)

    half = tq // 2
    assert half % CHUNK == 0 and half % LANES == 0
    early, late, everything = slice(0, half), slice(half, tq), slice(0, tq)

    def column_parts(on_diagonal):
        return [(early, early), (late, everything)] if on_diagonal else [(everything, everything)]

    def score(slot, pos, hh, on_diagonal):
        q_off = qtab_ref[pos] * tq
        k_off = ktab_ref[pos] * tq
        part_max = []
        for queries, keys in column_parts(on_diagonal):
            k_rows = pl.ds(pl.multiple_of(k_off + keys.start, half), keys.stop - keys.start)
            q_cols = pl.ds(pl.multiple_of(q_off + queries.start, half), queries.stop - queries.start)
            sc = _dot(k_ref[0, hh, k_rows, :], q_ref[0, hh, :, q_cols])
            if on_diagonal:
                sc = jnp.where(visible[keys, queries], sc, -jnp.inf)
            s_ref[slot, hh, keys, queries] = sc
            part_max.append(jnp.max(sc, axis=0, keepdims=True))
        return jnp.concatenate(part_max, axis=1)

    def absorb(slot, pos, hh, block_max, on_diagonal):
        qi = qtab_ref[pos]
        k_off = ktab_ref[pos] * tq
        m_old = m_ref[qi, hh]
        m_new = jnp.maximum(m_old, block_max)
        m_ref[qi, hh] = m_new
        alpha = jnp.exp2(m_old - m_new)
        for queries, keys in column_parts(on_diagonal):
            p = jnp.exp2(s_ref[slot, hh, keys, queries] - m_new[:, queries]).astype(BF16)
            k_cols = pl.ds(pl.multiple_of(k_off + keys.start, half), keys.stop - keys.start)
            vt = vt_ref[0, hh * MLA_V:(hh + 1) * MLA_V, k_cols]
            vt_ones = jnp.concatenate([vt, ones[:, keys]], axis=0)
            acc_ref[qi, hh, :, queries] = (alpha[:, queries] * acc_ref[qi, hh, :, queries]
                                           + _dot(vt_ones, p))

    def pipeline(first, count, on_diagonal, unroll):
        if count == 0:
            return
        assert unroll % 2 == 0
        n_loop = (count - 1) // unroll

        def several(pos, n, block_max, score_last):
            for u in range(n):
                nxt = []
                for hh in heads:
                    if u + 1 < n or score_last:
                        nxt.append(score((u + 1) % 2, pos + u + 1, hh, on_diagonal))
                    absorb(u % 2, pos + u, hh, block_max[hh], on_diagonal)
                block_max = tuple(nxt)
            return block_max

        first_max = tuple(score(0, first, hh, on_diagonal) for hh in heads)
        block_max = lax.fori_loop(0, n_loop, lambda t, bm: several(first + unroll * t, unroll, bm, True), first_max)
        several(first + unroll * n_loop, count - unroll * n_loop, block_max, False)

    pipeline(0, n_q, True, UNROLL_DIAGONAL)
    pipeline(n_q, n_off, False, UNROLL_VISIBLE)

    for qi in range(n_q):
        out_t = jnp.concatenate([acc_ref[qi, hh, :MLA_V, :] / acc_ref[qi, hh, MLA_V:MLA_V + 1, :] for hh in heads],
                                axis=0)
        o_ref[0, qi * tq:(qi + 1) * tq, :] = out_t.T.astype(BF16)


def _attn_call(q, k, vt):
    b, _, s, _ = k.shape
    tq = min(TQ_ATTN, s)
    n_q = s // tq
    off = [(qi, kj) for kj in range(n_q) for qi in range(kj + 1, n_q)]
    order = [(qi, qi) for qi in range(n_q)] + off
    qtab = jnp.asarray([p[0] for p in order], jnp.int32)
    ktab = jnp.asarray([p[1] for p in order], jnp.int32)
    grid_spec = pltpu.PrefetchScalarGridSpec(
        num_scalar_prefetch=2,
        grid=(b, N_GROUPS),
        in_specs=[pl.BlockSpec((1, HEADS_PER_GROUP, HEAD_PAD, s), lambda bi, g, qt, kt: (bi, g, 0, 0)),
                  pl.BlockSpec((1, HEADS_PER_GROUP, s, HEAD_PAD), lambda bi, g, qt, kt: (bi, g, 0, 0)),
                  pl.BlockSpec((1, LANES, s), lambda bi, g, qt, kt: (bi, g, 0))],
        out_specs=pl.BlockSpec((1, s, LANES), lambda bi, g, qt, kt: (bi, 0, g)),
        scratch_shapes=[pltpu.VMEM((2, HEADS_PER_GROUP, tq, tq), F32),
                        pltpu.VMEM((n_q, HEADS_PER_GROUP, 1, tq), F32),
                        pltpu.VMEM((n_q, HEADS_PER_GROUP, MLA_V + SUM_ROWS, tq), F32)])
    return pl.pallas_call(
        _attn_body,
        grid_spec=grid_spec,
        out_shape=jax.ShapeDtypeStruct((b, s, MLA_WIDTH), BF16),
        compiler_params=pltpu.CompilerParams(dimension_semantics=("parallel", "parallel"),
                                             vmem_limit_bytes=VMEM_LIMIT),
        name="attn",
    )(qtab, ktab, q, k, vt)


def _out_body(x_ref, attn_ref, ga_ref, o_ref, gb_ref, gt_ref, wa_ref, wb_ref, wo_ref, gpost_ref, out_ref):
    tm = x_ref.shape[1]
    parts = [slice(i * tm // OUT_PARTS, (i + 1) * tm // OUT_PARTS) for i in range(OUT_PARTS)]

    def merged(r):
        ya = _dot(attn_ref[0, r] * ga_ref[0, r], wa_ref[...])
        yb = _dot(o_ref[0, r] * gb_ref[0, r], wb_ref[...])
        m = gt_ref[0, r, :D_MODEL].astype(F32) * ya + gt_ref[0, r, D_MODEL:].astype(F32) * yb
        return m.astype(BF16)

    ms = [merged(r) for r in parts]
    ys = [_dot(m, wo_ref[...]) for m in ms]
    for r, y in zip(parts, ys):
        out_ref[0, r] = x_ref[0, r] + _rms(y) * gpost_ref[...]


def _out_call(x, attn, ga, o, gb, gt, wa, wb, wo, gpost):
    b, s, _ = x.shape
    tm = min(TM_OUT, s)
    tok = lambda bi, si: (bi, si, 0)
    const = lambda bi, si: (0, 0)

    def tspec(a):
        return pl.BlockSpec((1, tm, a.shape[-1]), tok)

    def full(a):
        return pl.BlockSpec(a.shape, const)

    return pl.pallas_call(
        _out_body,
        grid=(b, s // tm),
        in_specs=[tspec(x), tspec(attn), tspec(ga), tspec(o), tspec(gb), tspec(gt),
                  full(wa), full(wb), full(wo), full(gpost)],
        out_specs=tspec(x),
        out_shape=jax.ShapeDtypeStruct(x.shape, x.dtype),
        compiler_params=pltpu.CompilerParams(dimension_semantics=("parallel", "parallel"),
                                             vmem_limit_bytes=VMEM_LIMIT),
        name="merge_out",
    )(x, attn, ga, o, gb, gt, wa, wb, wo, gpost)


def _rope_tables(s):
    inv = ROPE_THETA ** (-np.arange(0, MLA_ROPE, 2, dtype=np.float64) / MLA_ROPE)
    ang = np.arange(s, dtype=np.float64)[:, None] * inv[None, :]
    cos, sin = np.cos(ang), np.sin(ang)
    pad = np.zeros((s, HEAD_PAD - MLA_QK))
    ra = np.concatenate([np.ones((s, MLA_NOPE)), cos, cos, pad], axis=1)
    rs = np.concatenate([np.zeros((s, MLA_NOPE)), -sin, sin, pad], axis=1)
    return jnp.asarray(ra, F32), jnp.asarray(rs, F32)


def _relayout_body(wt_ref, kpe_src_ref, out_ref, kpe_ref):
    out_ref[...] = wt_ref[...].T.astype(BF16)

    @pl.when(pl.program_id(0) == 0)
    def _():
        t = pltpu.roll(kpe_src_ref[...].T, MLA_NOPE, 1)
        lane = lax.broadcasted_iota(jnp.int32, t.shape, 1)
        kpe_ref[...] = jnp.where((lane >= MLA_NOPE) & (lane < MLA_QK), t, 0.0).astype(BF16)


def _relayout_w_in(w_in):
    d, n = w_in.shape
    kpe_lo = Q_LORA + KV_LORA
    wide_lo = kpe_lo + MLA_ROPE
    cols = W_RELAYOUT_COLS
    assert kpe_lo % cols == 0 and (n - wide_lo) % cols == 0
    n_lat = kpe_lo // cols

    def source_row(i):
        return pl.multiple_of(i * cols + jnp.minimum(i // n_lat, 1) * MLA_ROPE, MLA_ROPE)

    return pl.pallas_call(
        _relayout_body,
        grid=((n - MLA_ROPE) // cols,),
        in_specs=[pl.BlockSpec((pl.Element(cols), pl.Element(d)), lambda i: (source_row(i), 0)),
                  pl.BlockSpec((pl.Element(HEAD_PAD), pl.Element(d)), lambda i: (kpe_lo, 0))],
        out_specs=[pl.BlockSpec((d, cols), lambda i: (0, i)), pl.BlockSpec((d, HEAD_PAD), lambda i: (0, 0))],
        out_shape=[jax.ShapeDtypeStruct((d, n - MLA_ROPE), BF16), jax.ShapeDtypeStruct((d, HEAD_PAD), BF16)],
        compiler_params=pltpu.CompilerParams(dimension_semantics=("arbitrary",), vmem_limit_bytes=VMEM_LIMIT),
        name="relayout_w_in",
    )(w_in.T, w_in.T)


def _pad_weights(w_in, w_uq, w_ukv):
    w_cat, w_kpe = _relayout_w_in(w_in)
    wuq = jnp.pad(w_uq.reshape(Q_LORA, MLA_HEADS, MLA_QK), ((0, 0), (0, 0), (0, HEAD_PAD - MLA_QK)))
    wkv = w_ukv.reshape(KV_LORA, MLA_HEADS, MLA_NOPE + MLA_V)
    wk = jnp.pad(wkv[..., :MLA_NOPE], ((0, 0), (0, 0), (0, HEAD_PAD - MLA_NOPE)))
    wukv = jnp.concatenate([wk.reshape(KV_LORA, MLA_HEADS * HEAD_PAD),
                            wkv[..., MLA_NOPE:].reshape(KV_LORA, MLA_WIDTH)], axis=1)
    return (w_cat, w_kpe), wuq.reshape(Q_LORA, MLA_HEADS * HEAD_PAD).astype(BF16), wukv.astype(BF16)


def kernel(x, g_pre, w_in, b_gate, g_q, w_uq, g_kv, w_ukv, lb_logits, g_hgrn, w_branch_a, w_branch_b, w_out,
           g_post):
    assert g_pre.shape[0] == 1, "single-layer block"
    s = x.shape[1]
    win, wuq, wukv = _pad_weights(w_in[0], w_uq[0], w_ukv[0])
    ra, rs = _rope_tables(s)
    lower_bound = jax.nn.softmax(lb_logits.astype(F32), axis=0)[0:1]
    ghg = jnp.tile(g_hgrn[0], HG_HEADS)[None, :]

    q, k, vt, ga, o, gb, gt = _proj_call(x, g_pre, win, g_q, wuq, g_kv, wukv, ra, rs, b_gate, lower_bound, ghg)
    attn = _attn_call(q, k, vt)
    return _out_call(x, attn, ga, o, gb, gt, w_branch_a[0].astype(BF16), w_branch_b[0].astype(BF16),
                     w_out[0].astype(BF16), g_post)
```

```python
import functools
import math

import jax
import jax.numpy as jnp
import numpy as np
from jax import lax
from jax.experimental import pallas as pl
from jax.experimental.pallas import tpu as pltpu

F32 = jnp.float32
BF16 = jnp.bfloat16

D_MODEL = 1024
CHUNK = 64
EPS = 1e-6

MLA_HEADS = 8
MLA_NOPE = 64
MLA_ROPE = 32
MLA_V = 64
MLA_QK = MLA_NOPE + MLA_ROPE
Q_LORA = 768
KV_LORA = 256
ROPE_THETA = 10000.0
MLA_WIDTH = MLA_HEADS * MLA_V

HG_HEADS = 8
HG_DK = 64
HG_DV = 64
HG_WIDTH = HG_HEADS * HG_DV

LANES = 128
HEAD_PAD = LANES
HEADS_PER_GROUP = LANES // MLA_V
N_GROUPS = MLA_HEADS // HEADS_PER_GROUP

COL_CQ = 0
COL_CKV = COL_CQ + Q_LORA
COL_KPE = COL_CKV + KV_LORA
COL_GA = COL_KPE + HEAD_PAD
COL_HQ = COL_GA + MLA_WIDTH
COL_HF = COL_HQ + HG_WIDTH
COL_HI = COL_HF + HG_WIDTH
COL_GB = COL_HI + HG_WIDTH
COL_MG = COL_GB + HG_WIDTH
D_IN_PAD = COL_MG + 2 * D_MODEL

VMEM_LIMIT = 56 * 1024 * 1024

TM_PROJ = 512
TQ_ATTN = 512
SUM_ROWS = 16
UNROLL_DIAGONAL = 4
UNROLL_VISIBLE = 8
C_HGRN = 64
TM_OUT = 1024
OUT_PARTS = 2
W_RELAYOUT_COLS = 512


def _sigmoid(z):
    return 1.0 / (1.0 + jnp.exp(-z))


def _rms(t):
    return t * lax.rsqrt(jnp.mean(t * t, axis=-1, keepdims=True) + EPS)


def _dot(a, b):
    return jnp.dot(a, b, preferred_element_type=F32)


def _dot_nt(a, b):
    return lax.dot_general(a, b, (((1,), (1,)), ((), ())), preferred_element_type=F32)


def _cumsum_rows(t):
    rows = t.shape[0]
    row = lax.broadcasted_iota(jnp.int32, t.shape, 0)
    step = 1
    while step < rows:
        t = t + jnp.where(row >= step, pltpu.roll(t, step, 0), 0.0)
        step *= 2
    return t


def _recurrence_stages(proj, lb, gain, o_ref):
    c_len = C_HGRN
    groups = range(HG_HEADS // 2)
    low = lax.broadcasted_iota(jnp.int32, (c_len, LANES), 1) < HG_DK
    causal = (lax.broadcasted_iota(jnp.int32, (2 * c_len, c_len), 0) % c_len
              >= lax.broadcasted_iota(jnp.int32, (2 * c_len, c_len), 1))
    same_head = ((lax.broadcasted_iota(jnp.int32, (LANES, LANES), 0) < HG_DV)
                 == (lax.broadcasted_iota(jnp.int32, (LANES, LANES), 1) < HG_DK))

    def stack_heads(t):
        zero = jnp.zeros_like(t)
        return jnp.concatenate([jnp.where(low, t, zero), jnp.where(low, zero, t)], axis=0)

    def gates(c):
        f = lb + (1.0 - lb) * _sigmoid(proj["hf"][c * c_len:(c + 1) * c_len])
        return 1.0 - f, _cumsum_rows(jnp.log2(f))

    def scores(c, gated):
        k_in, cum = gated
        rows = slice(c * c_len, (c + 1) * c_len)
        mid = cum[c_len // 2 - 1:c_len // 2, :]
        tot = cum[c_len - 1:c_len, :]
        q_mid = proj["hq"][rows] * jnp.exp2(cum - mid)
        k_mid = k_in * jnp.exp2(mid - cum)
        q_dec = (q_mid * jnp.exp2(mid)).astype(BF16)
        k_end = (k_mid * jnp.exp2(tot - mid)).astype(BF16)
        q_mid = q_mid.astype(BF16)
        k_mid = k_mid.astype(BF16)
        v = proj["hi"][rows]
        per_group = []
        for g in groups:
            cols = slice(g * LANES, (g + 1) * LANES)
            a = _dot_nt(stack_heads(q_mid[:, cols]), k_mid[:, cols])
            v_t = v[:, cols].T.astype(BF16)
            upd = _dot(v_t, k_end[:, cols])
            per_group.append((a, v_t, upd, stack_heads(q_dec[:, cols])))
        return per_group, jnp.exp2(tot)

    def output(c, scored, states):
        per_group, decay = scored
        rows = slice(c * c_len, (c + 1) * c_len)
        new_states = []
        for g in groups:
            cols = slice(g * LANES, (g + 1) * LANES)
            a, v_t, upd, q_dec = per_group[g]
            a = jnp.where(causal, a, 0.0).astype(BF16)
            lhs = jnp.concatenate([q_dec, a], axis=1)
            rhs_t = jnp.concatenate([states[g].astype(BF16), v_t], axis=1)
            res = _dot_nt(lhs, rhs_t)
            o = jnp.where(low, res[:c_len], res[c_len:])
            sq = o * o
            msq_lo = jnp.sum(jnp.where(low, sq, 0.0), axis=-1, keepdims=True) * (1.0 / HG_DV)
            msq_hi = jnp.sum(jnp.where(low, 0.0, sq), axis=-1, keepdims=True) * (1.0 / HG_DV)
            inv_rms = jnp.where(low, lax.rsqrt(msq_lo + EPS), lax.rsqrt(msq_hi + EPS))
            o_ref[0, rows, cols] = (o * inv_rms * gain[:, cols]).astype(BF16)
            new_states.append(states[g] * decay[:, cols] + jnp.where(same_head, upd, 0.0))
        return new_states

    return gates, scores, output


def _proj_body(x_ref, gpre_ref, wcat_ref, wkpe_ref, gq_ref, wuq_ref, gkv_ref, wukv_ref, ra_ref, rs_ref,
               bg_ref, lb_ref, ghg_ref, q_ref, k_ref, v_ref, ga_ref, o_ref, gb_ref, gt_ref, st_ref):
    tm = x_ref.shape[1]
    groups = range(HG_HEADS // 2)

    @pl.when(pl.program_id(1) == 0)
    def _():
        st_ref[...] = jnp.zeros_like(st_ref)

    h = (_rms(x_ref[0]) * gpre_ref[...]).astype(BF16)

    def proj(lo, width):
        if lo == COL_KPE:
            return _dot(h, wkpe_ref[...])
        if lo > COL_KPE:
            lo -= HEAD_PAD
        return _dot(h, wcat_ref[:, lo:lo + width])

    ra = ra_ref[...]
    rs = rs_ref[...]
    lane = lax.broadcasted_iota(jnp.int32, ra.shape, 1)
    takes_upper = lane < MLA_NOPE + MLA_ROPE // 2

    def rope(t):
        partner = jnp.where(takes_upper, pltpu.roll(t, LANES - MLA_ROPE // 2, 1), pltpu.roll(t, MLA_ROPE // 2, 1))
        return t * ra + partner * rs

    held = {}

    def q_down():
        held["cqn"] = (_rms(proj(COL_CQ, Q_LORA)) * gq_ref[...]).astype(BF16)

    def q_up(first_head, n_heads):
        scale = math.log2(math.e) / math.sqrt(MLA_QK)
        qu = _dot(held["cqn"], wuq_ref[:, first_head * HEAD_PAD:(first_head + n_heads) * HEAD_PAD])
        for i in range(n_heads):
            q_ref[0, first_head + i] = (rope(qu[:, i * HEAD_PAD:(i + 1) * HEAD_PAD]) * scale).T.astype(BF16)

    def kv_down():
        held["ckvn"] = (_rms(proj(COL_CKV, KV_LORA)) * gkv_ref[...]).astype(BF16)
        held["kpe"] = rope(proj(COL_KPE, HEAD_PAD))

    def kv_up():
        kvu = _dot(held["ckvn"], wukv_ref[...])
        for hh in range(MLA_HEADS):
            k_ref[0, hh] = (kvu[:, hh * HEAD_PAD:(hh + 1) * HEAD_PAD] + held["kpe"]).astype(BF16)
        v_ref[0] = kvu[:, MLA_HEADS * HEAD_PAD:].T.astype(BF16)

    def silu_gate(lo, out_ref):
        z = proj(lo, out_ref.shape[-1])
        out_ref[0] = (z * _sigmoid(z)).astype(BF16)

    def merge_gates(c, width=2 * LANES):
        z = proj(COL_MG + c * width, width) + bg_ref[:, c * width:(c + 1) * width]
        gt_ref[0, :, c * width:(c + 1) * width] = _sigmoid(z).astype(BF16)

    half_heads = MLA_HEADS // 2
    others = [q_down, kv_down, functools.partial(q_up, 0, half_heads), kv_up,
              functools.partial(q_up, half_heads, half_heads),
              functools.partial(silu_gate, COL_GA, ga_ref), functools.partial(silu_gate, COL_GB, gb_ref)]
    pieces = []
    for c in range(D_MODEL // LANES):
        pieces += [functools.partial(merge_gates, c)] + others[c:c + 1]

    def next_piece():
        if pieces:
            pieces.pop(0)()

    rec = {}
    gates, scores, output = _recurrence_stages(rec, lb_ref[...], ghg_ref[...], o_ref)
    n_chunks = tm // C_HGRN
    x = x_ref[0]
    inv_rms = lax.rsqrt(jnp.mean(x * x, axis=-1, keepdims=True) + EPS)
    rec["hf"] = _dot((x * gpre_ref[...]).astype(BF16), wcat_ref[:, COL_HF - HEAD_PAD:COL_HF - HEAD_PAD + HG_WIDTH]) * inv_rms
    gated = {0: gates(0)}
    rec["hq"] = proj(COL_HQ, HG_WIDTH)
    if n_chunks > 1:
        gated[1] = gates(1)
    rec["hi"] = proj(COL_HI, HG_WIDTH)
    scored = {0: scores(0, gated.pop(0))}
    states = [st_ref[g] for g in groups]
    for c in range(n_chunks):
        if c + 2 < n_chunks:
            gated[c + 2] = gates(c + 2)
        next_piece()
        if c + 1 < n_chunks:
            scored[c + 1] = scores(c + 1, gated.pop(c + 1))
        next_piece()
        states = output(c, scored.pop(c), states)
    for g in groups:
        st_ref[g] = states[g]
    while pieces:
        next_piece()


def _proj_call(x, gpre, win, gq, wuq, gkv, wukv, ra, rs, bg, lb, ghg):
    b, s, _ = x.shape
    tm = min(TM_PROJ, s)
    const = lambda bi, si: (0, 0)
    tok = lambda bi, si: (bi, si, 0)

    def full(a):
        return pl.BlockSpec(a.shape, const, pipeline_mode=pl.Buffered(1))

    def tok_out(width, dtype):
        return jax.ShapeDtypeStruct((b, s, width), dtype), pl.BlockSpec((1, tm, width), tok)

    head_shape = jax.ShapeDtypeStruct((b, MLA_HEADS, s, HEAD_PAD), BF16)
    head_spec = pl.BlockSpec((1, MLA_HEADS, tm, HEAD_PAD), lambda bi, si: (bi, 0, si, 0))
    outs = [(jax.ShapeDtypeStruct((b, MLA_HEADS, HEAD_PAD, s), BF16),
             pl.BlockSpec((1, MLA_HEADS, HEAD_PAD, tm), lambda bi, si: (bi, 0, 0, si))),
            (head_shape, head_spec),
            (jax.ShapeDtypeStruct((b, MLA_WIDTH, s), BF16),
             pl.BlockSpec((1, MLA_WIDTH, tm), lambda bi, si: (bi, 0, si))),
            tok_out(MLA_WIDTH, BF16), tok_out(HG_WIDTH, BF16), tok_out(HG_WIDTH, BF16),
            tok_out(2 * D_MODEL, BF16)]
    rope_spec = pl.BlockSpec((tm, HEAD_PAD), lambda bi, si: (si, 0))
    return pl.pallas_call(
        _proj_body,
        grid=(b, s // tm),
        in_specs=[pl.BlockSpec((1, tm, D_MODEL), tok), full(gpre), *(full(w) for w in win), full(gq), full(wuq),
                  full(gkv), full(wukv), rope_spec, rope_spec, full(bg), full(lb), full(ghg)],
        out_specs=[o[1] for o in outs],
        out_shape=[o[0] for o in outs],
        scratch_shapes=[pltpu.VMEM((HG_HEADS // 2, LANES, LANES), F32)],
        compiler_params=pltpu.CompilerParams(dimension_semantics=("parallel", "arbitrary"),
                                             vmem_limit_bytes=VMEM_LIMIT),
        name="proj_hgrn",
    )(x, gpre, *win, gq, wuq, gkv, wukv, ra, rs, bg, lb, ghg)


def _attn_body(qtab_ref, ktab_ref, q_ref, k_ref, vt_ref, o_ref, s_ref, m_ref, acc_ref):
    tq = s_ref.shape[-1]
    n_q = k_ref.shape[2] // tq
    n_off = n_q * (n_q - 1) // 2
    heads = range(HEADS_PER_GROUP)
    key_chunk = lax.broadcasted_iota(jnp.int32, (tq, tq), 0) // CHUNK
    query_chunk = lax.broadcasted_iota(jnp.int32, (tq, tq), 1) // CHUNK
    visible = key_chunk <= query_chunk
    ones = jnp.ones((SUM_ROWS, tq), BF16)

    half = tq // 2
    assert half % CHUNK == 0 and half % LANES == 0
    early, late, everything = slice(0, half), slice(half, tq), slice(0, tq)

    def column_parts(on_diagonal):
        return [(early, early), (late, everything)] if on_diagonal else [(everything, everything)]

    def score(slot, pos, hh, on_diagonal):
        q_off = qtab_ref[pos] * tq
        k_off = ktab_ref[pos] * tq
        part_max = []
        for queries, keys in column_parts(on_diagonal):
            k_rows = pl.ds(pl.multiple_of(k_off + keys.start, half), keys.stop - keys.start)
            q_cols = pl.ds(pl.multiple_of(q_off + queries.start, half), queries.stop - queries.start)
            sc = _dot(k_ref[0, hh, k_rows, :], q_ref[0, hh, :, q_cols])
            if on_diagonal:
                sc = jnp.where(visible[keys, queries], sc, -jnp.inf)
            s_ref[slot, hh, keys, queries] = sc
            part_max.append(jnp.max(sc, axis=0, keepdims=True))
        return jnp.concatenate(part_max, axis=1)

    def absorb(slot, pos, hh, block_max, on_diagonal):
        qi = qtab_ref[pos]
        k_off = ktab_ref[pos] * tq
        if on_diagonal:
            m_new, alpha = block_max, None
        else:
            m_old = m_ref[qi, hh]
            m_new = jnp.maximum(m_old, block_max)
            alpha = jnp.exp2(m_old - m_new)
        m_ref[qi, hh] = m_new
        for queries, keys in column_parts(on_diagonal):
            p = jnp.exp2(s_ref[slot, hh, keys, queries] - m_new[:, queries]).astype(BF16)
            k_cols = pl.ds(pl.multiple_of(k_off + keys.start, half), keys.stop - keys.start)
            vt = vt_ref[0, hh * MLA_V:(hh + 1) * MLA_V, k_cols]
            pv = _dot(jnp.concatenate([vt, ones[:, keys]], axis=0), p)
            if not on_diagonal:
                pv += alpha[:, queries] * acc_ref[qi, hh, :, queries]
            acc_ref[qi, hh, :, queries] = pv

    def pipeline(first, count, on_diagonal, unroll):
        if count == 0:
            return
        assert unroll % 2 == 0
        n_loop = (count - 1) // unroll

        def several(pos, n, block_max, score_last):
            for u in range(n):
                nxt = []
                for hh in heads:
                    if u + 1 < n or score_last:
                        nxt.append(score((u + 1) % 2, pos + u + 1, hh, on_diagonal))
                    absorb(u % 2, pos + u, hh, block_max[hh], on_diagonal)
                block_max = tuple(nxt)
            return block_max

        first_max = tuple(score(0, first, hh, on_diagonal) for hh in heads)
        block_max = lax.fori_loop(0, n_loop, lambda t, bm: several(first + unroll * t, unroll, bm, True), first_max)
        several(first + unroll * n_loop, count - unroll * n_loop, block_max, False)

    pipeline(0, n_q, True, UNROLL_DIAGONAL)
    pipeline(n_q, n_off, False, UNROLL_VISIBLE)

    for qi in range(n_q):
        out_t = jnp.concatenate([acc_ref[qi, hh, :MLA_V, :] / acc_ref[qi, hh, MLA_V:MLA_V + 1, :] for hh in heads],
                                axis=0)
        o_ref[0, qi * tq:(qi + 1) * tq, :] = out_t.T.astype(BF16)


def _attn_call(q, k, vt):
    b, _, s, _ = k.shape
    tq = min(TQ_ATTN, s)
    n_q = s // tq
    off = [(qi, kj) for kj in range(n_q) for qi in range(kj + 1, n_q)]
    order = [(qi, qi) for qi in range(n_q)] + off
    qtab = jnp.asarray([p[0] for p in order], jnp.int32)
    ktab = jnp.asarray([p[1] for p in order], jnp.int32)
    grid_spec = pltpu.PrefetchScalarGridSpec(
        num_scalar_prefetch=2,
        grid=(b, N_GROUPS),
        in_specs=[pl.BlockSpec((1, HEADS_PER_GROUP, HEAD_PAD, s), lambda bi, g, qt, kt: (bi, g, 0, 0)),
                  pl.BlockSpec((1, HEADS_PER_GROUP, s, HEAD_PAD), lambda bi, g, qt, kt: (bi, g, 0, 0)),
                  pl.BlockSpec((1, LANES, s), lambda bi, g, qt, kt: (bi, g, 0))],
        out_specs=pl.BlockSpec((1, s, LANES), lambda bi, g, qt, kt: (bi, 0, g)),
        scratch_shapes=[pltpu.VMEM((2, HEADS_PER_GROUP, tq, tq), F32),
                        pltpu.VMEM((n_q, HEADS_PER_GROUP, 1, tq), F32),
                        pltpu.VMEM((n_q, HEADS_PER_GROUP, MLA_V + SUM_ROWS, tq), F32)])
    return pl.pallas_call(
        _attn_body,
        grid_spec=grid_spec,
        out_shape=jax.ShapeDtypeStruct((b, s, MLA_WIDTH), BF16),
        compiler_params=pltpu.CompilerParams(dimension_semantics=("parallel", "parallel"),
                                             vmem_limit_bytes=VMEM_LIMIT),
        name="attn",
    )(qtab, ktab, q, k, vt)


def _out_body(x_ref, attn_ref, ga_ref, o_ref, gb_ref, gt_ref, wa_ref, wb_ref, wo_ref, gpost_ref, out_ref):
    tm = x_ref.shape[1]
    parts = [slice(i * tm // OUT_PARTS, (i + 1) * tm // OUT_PARTS) for i in range(OUT_PARTS)]

    def merged(r):
        ya = _dot(attn_ref[0, r] * ga_ref[0, r], wa_ref[...])
        yb = _dot(o_ref[0, r] * gb_ref[0, r], wb_ref[...])
        m = gt_ref[0, r, :D_MODEL].astype(F32) * ya + gt_ref[0, r, D_MODEL:].astype(F32) * yb
        return m.astype(BF16)

    ms = [merged(r) for r in parts]
    ys = [_dot(m, wo_ref[...]) for m in ms]
    for r, y in zip(parts, ys):
        out_ref[0, r] = x_ref[0, r] + _rms(y) * gpost_ref[...]


def _out_call(x, attn, ga, o, gb, gt, wa, wb, wo, gpost):
    b, s, _ = x.shape
    tm = min(TM_OUT, s)
    tok = lambda bi, si: (bi, si, 0)
    const = lambda bi, si: (0, 0)

    def tspec(a):
        return pl.BlockSpec((1, tm, a.shape[-1]), tok)

    def full(a):
        return pl.BlockSpec(a.shape, const)

    return pl.pallas_call(
        _out_body,
        grid=(b, s // tm),
        in_specs=[tspec(x), tspec(attn), tspec(ga), tspec(o), tspec(gb), tspec(gt),
                  full(wa), full(wb), full(wo), full(gpost)],
        out_specs=tspec(x),
        out_shape=jax.ShapeDtypeStruct(x.shape, x.dtype),
        compiler_params=pltpu.CompilerParams(dimension_semantics=("parallel", "parallel"),
                                             vmem_limit_bytes=VMEM_LIMIT),
        name="merge_out",
    )(x, attn, ga, o, gb, gt, wa, wb, wo, gpost)


def _rope_tables(s):
    inv = ROPE_THETA ** (-np.arange(0, MLA_ROPE, 2, dtype=np.float64) / MLA_ROPE)
    ang = np.arange(s, dtype=np.float64)[:, None] * inv[None, :]
    cos, sin = np.cos(ang), np.sin(ang)
    pad = np.zeros((s, HEAD_PAD - MLA_QK))
    ra = np.concatenate([np.ones((s, MLA_NOPE)), cos, cos, pad], axis=1)
    rs = np.concatenate([np.zeros((s, MLA_NOPE)), -sin, sin, pad], axis=1)
    return jnp.asarray(ra, F32), jnp.asarray(rs, F32)


def _relayout_body(wt_ref, kpe_src_ref, out_ref, kpe_ref):
    out_ref[...] = wt_ref[...].T.astype(BF16)

    @pl.when(pl.program_id(0) == 0)
    def _():
        t = pltpu.roll(kpe_src_ref[...].T, MLA_NOPE, 1)
        lane = lax.broadcasted_iota(jnp.int32, t.shape, 1)
        kpe_ref[...] = jnp.where((lane >= MLA_NOPE) & (lane < MLA_QK), t, 0.0).astype(BF16)


def _relayout_w_in(w_in):
    d, n = w_in.shape
    kpe_lo = Q_LORA + KV_LORA
    wide_lo = kpe_lo + MLA_ROPE
    cols = W_RELAYOUT_COLS
    assert kpe_lo % cols == 0 and (n - wide_lo) % cols == 0
    n_lat = kpe_lo // cols

    def source_row(i):
        return pl.multiple_of(i * cols + jnp.minimum(i // n_lat, 1) * MLA_ROPE, MLA_ROPE)

    return pl.pallas_call(
        _relayout_body,
        grid=((n - MLA_ROPE) // cols,),
        in_specs=[pl.BlockSpec((pl.Element(cols), pl.Element(d)), lambda i: (source_row(i), 0)),
                  pl.BlockSpec((pl.Element(HEAD_PAD), pl.Element(d)), lambda i: (kpe_lo, 0))],
        out_specs=[pl.BlockSpec((d, cols), lambda i: (0, i)), pl.BlockSpec((d, HEAD_PAD), lambda i: (0, 0))],
        out_shape=[jax.ShapeDtypeStruct((d, n - MLA_ROPE), BF16), jax.ShapeDtypeStruct((d, HEAD_PAD), BF16)],
        compiler_params=pltpu.CompilerParams(dimension_semantics=("arbitrary",), vmem_limit_bytes=VMEM_LIMIT),
        name="relayout_w_in",
    )(w_in.T, w_in.T)


def _pad_weights(w_in, w_uq, w_ukv):
    w_cat, w_kpe = _relayout_w_in(w_in)
    wuq = jnp.pad(w_uq.reshape(Q_LORA, MLA_HEADS, MLA_QK), ((0, 0), (0, 0), (0, HEAD_PAD - MLA_QK)))
    wkv = w_ukv.reshape(KV_LORA, MLA_HEADS, MLA_NOPE + MLA_V)
    wk = jnp.pad(wkv[..., :MLA_NOPE], ((0, 0), (0, 0), (0, HEAD_PAD - MLA_NOPE)))
    wukv = jnp.concatenate([wk.reshape(KV_LORA, MLA_HEADS * HEAD_PAD),
                            wkv[..., MLA_NOPE:].reshape(KV_LORA, MLA_WIDTH)], axis=1)
    return (w_cat, w_kpe), wuq.reshape(Q_LORA, MLA_HEADS * HEAD_PAD).astype(BF16), wukv.astype(BF16)


def kernel(x, g_pre, w_in, b_gate, g_q, w_uq, g_kv, w_ukv, lb_logits, g_hgrn, w_branch_a, w_branch_b, w_out,
           g_post):
    assert g_pre.shape[0] == 1, "single-layer block"
    s = x.shape[1]
    win, wuq, wukv = _pad_weights(w_in[0], w_uq[0], w_ukv[0])
    ra, rs = _rope_tables(s)
    lower_bound = jax.nn.softmax(lb_logits.astype(F32), axis=0)[0:1]
    ghg = jnp.tile(g_hgrn[0], HG_HEADS)[None, :]

    q, k, vt, ga, o, gb, gt = _proj_call(x, g_pre, win, g_q, wuq, g_kv, wukv, ra, rs, b_gate, lower_bound, ghg)
    attn = _attn_call(q, k, vt)
    return _out_call(x, attn, ga, o, gb, gt, w_branch_a[0].astype(BF16), w_branch_b[0].astype(BF16),
                     w_out[0].astype(BF16), g_post)
```

```python
import functools
import math

import jax
import jax.numpy as jnp
import numpy as np
from jax import lax
from jax.experimental import pallas as pl
from jax.experimental.pallas import tpu as pltpu

F32 = jnp.float32
BF16 = jnp.bfloat16

D_MODEL = 1024
CHUNK = 64
EPS = 1e-6

MLA_HEADS = 8
MLA_NOPE = 64
MLA_ROPE = 32
MLA_V = 64
MLA_QK = MLA_NOPE + MLA_ROPE
Q_LORA = 768
KV_LORA = 256
ROPE_THETA = 10000.0
MLA_WIDTH = MLA_HEADS * MLA_V

HG_HEADS = 8
HG_DK = 64
HG_DV = 64
HG_WIDTH = HG_HEADS * HG_DV

LANES = 128
HEAD_PAD = LANES
HEADS_PER_GROUP = LANES // MLA_V
N_GROUPS = MLA_HEADS // HEADS_PER_GROUP

COL_CQ = 0
COL_CKV = COL_CQ + Q_LORA
COL_KPE = COL_CKV + KV_LORA
COL_GA = COL_KPE + HEAD_PAD
COL_HQ = COL_GA + MLA_WIDTH
COL_HF = COL_HQ + HG_WIDTH
COL_HI = COL_HF + HG_WIDTH
COL_GB = COL_HI + HG_WIDTH
COL_MG = COL_GB + HG_WIDTH
D_IN_PAD = COL_MG + 2 * D_MODEL

VMEM_LIMIT = 56 * 1024 * 1024

TM_PROJ = 512
TQ_ATTN = 512
SUM_ROWS = 16
UNROLL_DIAGONAL = 4
UNROLL_VISIBLE = 8
C_HGRN = 64
TM_OUT = 1024
OUT_PARTS = 2
W_RELAYOUT_COLS = 512


def _sigmoid(z):
    return 1.0 / (1.0 + jnp.exp(-z))


def _rms(t):
    return t * lax.rsqrt(jnp.mean(t * t, axis=-1, keepdims=True) + EPS)


def _dot(a, b):
    return jnp.dot(a, b, preferred_element_type=F32)


def _dot_nt(a, b):
    return lax.dot_general(a, b, (((1,), (1,)), ((), ())), preferred_element_type=F32)


def _cumsum_rows(t):
    rows = t.shape[0]
    row = lax.broadcasted_iota(jnp.int32, t.shape, 0)
    step = 1
    while step < rows:
        t = t + jnp.where(row >= step, pltpu.roll(t, step, 0), 0.0)
        step *= 2
    return t


def _recurrence_stages(proj, lb, gain, o_ref):
    c_len = C_HGRN
    groups = range(HG_HEADS // 2)
    low = lax.broadcasted_iota(jnp.int32, (c_len, LANES), 1) < HG_DK
    causal = (lax.broadcasted_iota(jnp.int32, (2 * c_len, c_len), 0) % c_len
              >= lax.broadcasted_iota(jnp.int32, (2 * c_len, c_len), 1))
    same_head = ((lax.broadcasted_iota(jnp.int32, (LANES, LANES), 0) < HG_DV)
                 == (lax.broadcasted_iota(jnp.int32, (LANES, LANES), 1) < HG_DK))

    def stack_heads(t):
        zero = jnp.zeros_like(t)
        return jnp.concatenate([jnp.where(low, t, zero), jnp.where(low, zero, t)], axis=0)

    def gates(c):
        f = lb + (1.0 - lb) * _sigmoid(proj["hf"][c * c_len:(c + 1) * c_len])
        return 1.0 - f, _cumsum_rows(jnp.log2(f))

    def scores(c, gated):
        k_in, cum = gated
        rows = slice(c * c_len, (c + 1) * c_len)
        mid = cum[c_len // 2 - 1:c_len // 2, :]
        tot = cum[c_len - 1:c_len, :]
        q_mid = proj["hq"][rows] * jnp.exp2(cum - mid)
        k_mid = k_in * jnp.exp2(mid - cum)
        q_dec = (q_mid * jnp.exp2(mid)).astype(BF16)
        k_end = (k_mid * jnp.exp2(tot - mid)).astype(BF16)
        q_mid = q_mid.astype(BF16)
        k_mid = k_mid.astype(BF16)
        v = proj["hi"][rows]
        per_group = []
        for g in groups:
            cols = slice(g * LANES, (g + 1) * LANES)
            a = _dot_nt(stack_heads(q_mid[:, cols]), k_mid[:, cols])
            v_t = v[:, cols].T.astype(BF16)
            upd = _dot(v_t, k_end[:, cols])
            per_group.append((a, v_t, upd, stack_heads(q_dec[:, cols])))
        return per_group, jnp.exp2(tot)

    def output(c, scored, states):
        per_group, decay = scored
        rows = slice(c * c_len, (c + 1) * c_len)
        new_states = []
        for g in groups:
            cols = slice(g * LANES, (g + 1) * LANES)
            a, v_t, upd, q_dec = per_group[g]
            a = jnp.where(causal, a, 0.0).astype(BF16)
            lhs = jnp.concatenate([q_dec, a], axis=1)
            rhs_t = jnp.concatenate([states[g].astype(BF16), v_t], axis=1)
            res = _dot_nt(lhs, rhs_t)
            o = jnp.where(low, res[:c_len], res[c_len:])
            sq = o * o
            msq_lo = jnp.sum(jnp.where(low, sq, 0.0), axis=-1, keepdims=True) * (1.0 / HG_DV)
            msq_hi = jnp.sum(jnp.where(low, 0.0, sq), axis=-1, keepdims=True) * (1.0 / HG_DV)
            inv_rms = jnp.where(low, lax.rsqrt(msq_lo + EPS), lax.rsqrt(msq_hi + EPS))
            o_ref[0, rows, cols] = (o * inv_rms * gain[:, cols] * proj["gb"][rows, cols]).astype(BF16)
            new_states.append(states[g] * decay[:, cols] + jnp.where(same_head, upd, 0.0))
        return new_states

    return gates, scores, output


def _proj_body(x_ref, gpre_ref, wcat_ref, wkpe_ref, gq_ref, wuq_ref, gkv_ref, wukv_ref, ra_ref, rs_ref,
               bg_ref, lb_ref, ghg_ref, q_ref, k_ref, v_ref, ga_ref, o_ref, gt_ref, st_ref):
    tm = x_ref.shape[1]
    groups = range(HG_HEADS // 2)

    @pl.when(pl.program_id(1) == 0)
    def _():
        st_ref[...] = jnp.zeros_like(st_ref)

    h = (_rms(x_ref[0]) * gpre_ref[...]).astype(BF16)

    def proj(lo, width):
        if lo == COL_KPE:
            return _dot(h, wkpe_ref[...])
        if lo > COL_KPE:
            lo -= HEAD_PAD
        return _dot(h, wcat_ref[:, lo:lo + width])

    ra = ra_ref[...]
    rs = rs_ref[...]
    lane = lax.broadcasted_iota(jnp.int32, ra.shape, 1)
    takes_upper = lane < MLA_NOPE + MLA_ROPE // 2

    def rope(t):
        partner = jnp.where(takes_upper, pltpu.roll(t, LANES - MLA_ROPE // 2, 1), pltpu.roll(t, MLA_ROPE // 2, 1))
        return t * ra + partner * rs

    held = {}

    def q_down():
        held["cqn"] = (_rms(proj(COL_CQ, Q_LORA)) * gq_ref[...]).astype(BF16)

    def q_up(first_head, n_heads):
        scale = math.log2(math.e) / math.sqrt(MLA_QK)
        qu = _dot(held["cqn"], wuq_ref[:, first_head * HEAD_PAD:(first_head + n_heads) * HEAD_PAD])
        for i in range(n_heads):
            q_ref[0, first_head + i] = (rope(qu[:, i * HEAD_PAD:(i + 1) * HEAD_PAD]) * scale).T.astype(BF16)

    def kv_down():
        held["ckvn"] = (_rms(proj(COL_CKV, KV_LORA)) * gkv_ref[...]).astype(BF16)
        held["kpe"] = rope(proj(COL_KPE, HEAD_PAD))

    def kv_up():
        kvu = _dot(held["ckvn"], wukv_ref[...])
        for hh in range(MLA_HEADS):
            k_ref[0, hh] = (kvu[:, hh * HEAD_PAD:(hh + 1) * HEAD_PAD] + held["kpe"]).astype(BF16)
        v_ref[0] = kvu[:, MLA_HEADS * HEAD_PAD:].T.astype(BF16)

    def silu_gate(lo, out_ref):
        z = proj(lo, out_ref.shape[-1])
        out_ref[0] = (z * _sigmoid(z)).astype(BF16)

    def merge_gates(c, width=2 * LANES):
        z = proj(COL_MG + c * width, width) + bg_ref[:, c * width:(c + 1) * width]
        gt_ref[0, :, c * width:(c + 1) * width] = _sigmoid(z).astype(BF16)

    half_heads = MLA_HEADS // 2
    others = [q_down, kv_down, functools.partial(q_up, 0, half_heads), kv_up,
              functools.partial(q_up, half_heads, half_heads), functools.partial(silu_gate, COL_GA, ga_ref)]
    pieces = []
    for c in range(D_MODEL // LANES):
        pieces += [functools.partial(merge_gates, c)] + others[c:c + 1]

    def next_piece():
        if pieces:
            pieces.pop(0)()

    rec = {}
    gates, scores, output = _recurrence_stages(rec, lb_ref[...], ghg_ref[...], o_ref)
    n_chunks = tm // C_HGRN
    x = x_ref[0]
    inv_rms = lax.rsqrt(jnp.mean(x * x, axis=-1, keepdims=True) + EPS)
    rec["hf"] = _dot((x * gpre_ref[...]).astype(BF16), wcat_ref[:, COL_HF - HEAD_PAD:COL_HF - HEAD_PAD + HG_WIDTH]) * inv_rms
    gated = {0: gates(0)}
    rec["hq"] = proj(COL_HQ, HG_WIDTH)
    if n_chunks > 1:
        gated[1] = gates(1)
    rec["hi"] = proj(COL_HI, HG_WIDTH)
    scored = {0: scores(0, gated.pop(0))}
    z = proj(COL_GB, HG_WIDTH)
    rec["gb"] = z * _sigmoid(z)
    states = [st_ref[g] for g in groups]
    for c in range(n_chunks):
        if c + 2 < n_chunks:
            gated[c + 2] = gates(c + 2)
        next_piece()
        if c + 1 < n_chunks:
            scored[c + 1] = scores(c + 1, gated.pop(c + 1))
        next_piece()
        states = output(c, scored.pop(c), states)
    for g in groups:
        st_ref[g] = states[g]
    while pieces:
        next_piece()


def _proj_call(x, gpre, win, gq, wuq, gkv, wukv, ra, rs, bg, lb, ghg):
    b, s, _ = x.shape
    tm = min(TM_PROJ, s)
    const = lambda bi, si: (0, 0)
    tok = lambda bi, si: (bi, si, 0)

    def full(a):
        return pl.BlockSpec(a.shape, const, pipeline_mode=pl.Buffered(1))

    def tok_out(width, dtype):
        return jax.ShapeDtypeStruct((b, s, width), dtype), pl.BlockSpec((1, tm, width), tok)

    head_shape = jax.ShapeDtypeStruct((b, MLA_HEADS, s, HEAD_PAD), BF16)
    head_spec = pl.BlockSpec((1, MLA_HEADS, tm, HEAD_PAD), lambda bi, si: (bi, 0, si, 0))
    outs = [(jax.ShapeDtypeStruct((b, MLA_HEADS, HEAD_PAD, s), BF16),
             pl.BlockSpec((1, MLA_HEADS, HEAD_PAD, tm), lambda bi, si: (bi, 0, 0, si))),
            (head_shape, head_spec),
            (jax.ShapeDtypeStruct((b, MLA_WIDTH, s), BF16),
             pl.BlockSpec((1, MLA_WIDTH, tm), lambda bi, si: (bi, 0, si))),
            tok_out(MLA_WIDTH, BF16), tok_out(HG_WIDTH, BF16),
            tok_out(2 * D_MODEL, BF16)]
    rope_spec = pl.BlockSpec((tm, HEAD_PAD), lambda bi, si: (si, 0))
    return pl.pallas_call(
        _proj_body,
        grid=(b, s // tm),
        in_specs=[pl.BlockSpec((1, tm, D_MODEL), tok), full(gpre), *(full(w) for w in win), full(gq), full(wuq),
                  full(gkv), full(wukv), rope_spec, rope_spec, full(bg), full(lb), full(ghg)],
        out_specs=[o[1] for o in outs],
        out_shape=[o[0] for o in outs],
        scratch_shapes=[pltpu.VMEM((HG_HEADS // 2, LANES, LANES), F32)],
        compiler_params=pltpu.CompilerParams(dimension_semantics=("parallel", "arbitrary"),
                                             vmem_limit_bytes=VMEM_LIMIT),
        name="proj_hgrn",
    )(x, gpre, *win, gq, wuq, gkv, wukv, ra, rs, bg, lb, ghg)


def _attn_body(qtab_ref, ktab_ref, q_ref, k_ref, vt_ref, ga_ref, o_ref, s_ref, m_ref, acc_ref):
    tq = s_ref.shape[-1]
    n_q = k_ref.shape[2] // tq
    n_off = n_q * (n_q - 1) // 2
    heads = range(HEADS_PER_GROUP)
    key_chunk = lax.broadcasted_iota(jnp.int32, (tq, tq), 0) // CHUNK
    query_chunk = lax.broadcasted_iota(jnp.int32, (tq, tq), 1) // CHUNK
    visible = key_chunk <= query_chunk
    ones = jnp.ones((SUM_ROWS, tq), BF16)

    half = tq // 2
    assert half % CHUNK == 0 and half % LANES == 0
    early, late, everything = slice(0, half), slice(half, tq), slice(0, tq)

    def column_parts(on_diagonal):
        return [(early, early), (late, everything)] if on_diagonal else [(everything, everything)]

    def score(slot, pos, hh, on_diagonal):
        q_off = qtab_ref[pos] * tq
        k_off = ktab_ref[pos] * tq
        part_max = []
        for queries, keys in column_parts(on_diagonal):
            k_rows = pl.ds(pl.multiple_of(k_off + keys.start, half), keys.stop - keys.start)
            q_cols = pl.ds(pl.multiple_of(q_off + queries.start, half), queries.stop - queries.start)
            sc = _dot(k_ref[0, hh, k_rows, :], q_ref[0, hh, :, q_cols])
            if on_diagonal:
                sc = jnp.where(visible[keys, queries], sc, -jnp.inf)
            s_ref[slot, hh, keys, queries] = sc
            part_max.append(jnp.max(sc, axis=0, keepdims=True))
        return jnp.concatenate(part_max, axis=1)

    def absorb(slot, pos, hh, block_max, on_diagonal):
        qi = qtab_ref[pos]
        k_off = ktab_ref[pos] * tq
        if on_diagonal:
            m_new, alpha = block_max, None
        else:
            m_old = m_ref[qi, hh]
            m_new = jnp.maximum(m_old, block_max)
            alpha = jnp.exp2(m_old - m_new)
        m_ref[qi, hh] = m_new
        for queries, keys in column_parts(on_diagonal):
            p = jnp.exp2(s_ref[slot, hh, keys, queries] - m_new[:, queries]).astype(BF16)
            k_cols = pl.ds(pl.multiple_of(k_off + keys.start, half), keys.stop - keys.start)
            vt = vt_ref[0, hh * MLA_V:(hh + 1) * MLA_V, k_cols]
            pv = _dot(jnp.concatenate([vt, ones[:, keys]], axis=0), p)
            if not on_diagonal:
                pv += alpha[:, queries] * acc_ref[qi, hh, :, queries]
            acc_ref[qi, hh, :, queries] = pv

    def pipeline(first, count, on_diagonal, unroll):
        if count == 0:
            return
        assert unroll % 2 == 0
        n_loop = (count - 1) // unroll

        def several(pos, n, block_max, score_last):
            for u in range(n):
                nxt = []
                for hh in heads:
                    if u + 1 < n or score_last:
                        nxt.append(score((u + 1) % 2, pos + u + 1, hh, on_diagonal))
                    absorb(u % 2, pos + u, hh, block_max[hh], on_diagonal)
                block_max = tuple(nxt)
            return block_max

        first_max = tuple(score(0, first, hh, on_diagonal) for hh in heads)
        block_max = lax.fori_loop(0, n_loop, lambda t, bm: several(first + unroll * t, unroll, bm, True), first_max)
        several(first + unroll * n_loop, count - unroll * n_loop, block_max, False)

    pipeline(0, n_q, True, UNROLL_DIAGONAL)
    pipeline(n_q, n_off, False, UNROLL_VISIBLE)

    for qi in range(n_q):
        out_t = jnp.concatenate([acc_ref[qi, hh, :MLA_V, :] / acc_ref[qi, hh, MLA_V:MLA_V + 1, :] for hh in heads],
                                axis=0)
        rows = slice(qi * tq, (qi + 1) * tq)
        o_ref[0, rows, :] = (out_t.T * ga_ref[0, rows, :].astype(F32)).astype(BF16)


def _attn_call(q, k, vt, ga):
    b, _, s, _ = k.shape
    tq = min(TQ_ATTN, s)
    n_q = s // tq
    off = [(qi, kj) for kj in range(n_q) for qi in range(kj + 1, n_q)]
    order = [(qi, qi) for qi in range(n_q)] + off
    qtab = jnp.asarray([p[0] for p in order], jnp.int32)
    ktab = jnp.asarray([p[1] for p in order], jnp.int32)
    grid_spec = pltpu.PrefetchScalarGridSpec(
        num_scalar_prefetch=2,
        grid=(b, N_GROUPS),
        in_specs=[pl.BlockSpec((1, HEADS_PER_GROUP, HEAD_PAD, s), lambda bi, g, qt, kt: (bi, g, 0, 0)),
                  pl.BlockSpec((1, HEADS_PER_GROUP, s, HEAD_PAD), lambda bi, g, qt, kt: (bi, g, 0, 0)),
                  pl.BlockSpec((1, LANES, s), lambda bi, g, qt, kt: (bi, g, 0)),
                  pl.BlockSpec((1, s, LANES), lambda bi, g, qt, kt: (bi, 0, g))],
        out_specs=pl.BlockSpec((1, s, LANES), lambda bi, g, qt, kt: (bi, 0, g)),
        scratch_shapes=[pltpu.VMEM((2, HEADS_PER_GROUP, tq, tq), F32),
                        pltpu.VMEM((n_q, HEADS_PER_GROUP, 1, tq), F32),
                        pltpu.VMEM((n_q, HEADS_PER_GROUP, MLA_V + SUM_ROWS, tq), F32)])
    return pl.pallas_call(
        _attn_body,
        grid_spec=grid_spec,
        out_shape=jax.ShapeDtypeStruct((b, s, MLA_WIDTH), BF16),
        compiler_params=pltpu.CompilerParams(dimension_semantics=("parallel", "parallel"),
                                             vmem_limit_bytes=VMEM_LIMIT),
        name="attn",
    )(qtab, ktab, q, k, vt, ga)


def _out_body(x_ref, attn_ref, o_ref, gt_ref, wa_ref, wb_ref, wo_ref, gpost_ref, out_ref):
    tm = x_ref.shape[1]
    parts = [slice(i * tm // OUT_PARTS, (i + 1) * tm // OUT_PARTS) for i in range(OUT_PARTS)]

    def merged(r):
        ya = _dot(attn_ref[0, r], wa_ref[...])
        yb = _dot(o_ref[0, r], wb_ref[...])
        m = gt_ref[0, r, :D_MODEL].astype(F32) * ya + gt_ref[0, r, D_MODEL:].astype(F32) * yb
        return m.astype(BF16)

    ms = [merged(r) for r in parts]
    ys = [_dot(m, wo_ref[...]) for m in ms]
    for r, y in zip(parts, ys):
        out_ref[0, r] = x_ref[0, r] + _rms(y) * gpost_ref[...]


def _out_call(x, attn, o, gt, wa, wb, wo, gpost):
    b, s, _ = x.shape
    tm = min(TM_OUT, s)
    tok = lambda bi, si: (bi, si, 0)
    const = lambda bi, si: (0, 0)

    def tspec(a):
        return pl.BlockSpec((1, tm, a.shape[-1]), tok)

    def full(a):
        return pl.BlockSpec(a.shape, const)

    return pl.pallas_call(
        _out_body,
        grid=(b, s // tm),
        in_specs=[tspec(x), tspec(attn), tspec(o), tspec(gt), full(wa), full(wb), full(wo), full(gpost)],
        out_specs=tspec(x),
        out_shape=jax.ShapeDtypeStruct(x.shape, x.dtype),
        compiler_params=pltpu.CompilerParams(dimension_semantics=("parallel", "parallel"),
                                             vmem_limit_bytes=VMEM_LIMIT),
        name="merge_out",
    )(x, attn, o, gt, wa, wb, wo, gpost)


def _rope_tables(s):
    inv = ROPE_THETA ** (-np.arange(0, MLA_ROPE, 2, dtype=np.float64) / MLA_ROPE)
    ang = np.arange(s, dtype=np.float64)[:, None] * inv[None, :]
    cos, sin = np.cos(ang), np.sin(ang)
    pad = np.zeros((s, HEAD_PAD - MLA_QK))
    ra = np.concatenate([np.ones((s, MLA_NOPE)), cos, cos, pad], axis=1)
    rs = np.concatenate([np.zeros((s, MLA_NOPE)), -sin, sin, pad], axis=1)
    return jnp.asarray(ra, F32), jnp.asarray(rs, F32)


def _relayout_body(wt_ref, kpe_src_ref, out_ref, kpe_ref):
    out_ref[...] = wt_ref[...].T.astype(BF16)

    @pl.when(pl.program_id(0) == 0)
    def _():
        t = pltpu.roll(kpe_src_ref[...].T, MLA_NOPE, 1)
        lane = lax.broadcasted_iota(jnp.int32, t.shape, 1)
        kpe_ref[...] = jnp.where((lane >= MLA_NOPE) & (lane < MLA_QK), t, 0.0).astype(BF16)


def _relayout_w_in(w_in):
    d, n = w_in.shape
    kpe_lo = Q_LORA + KV_LORA
    wide_lo = kpe_lo + MLA_ROPE
    cols = W_RELAYOUT_COLS
    assert kpe_lo % cols == 0 and (n - wide_lo) % cols == 0
    n_lat = kpe_lo // cols

    def source_row(i):
        return pl.multiple_of(i * cols + jnp.minimum(i // n_lat, 1) * MLA_ROPE, MLA_ROPE)

    return pl.pallas_call(
        _relayout_body,
        grid=((n - MLA_ROPE) // cols,),
        in_specs=[pl.BlockSpec((pl.Element(cols), pl.Element(d)), lambda i: (source_row(i), 0)),
                  pl.BlockSpec((pl.Element(HEAD_PAD), pl.Element(d)), lambda i: (kpe_lo, 0))],
        out_specs=[pl.BlockSpec((d, cols), lambda i: (0, i)), pl.BlockSpec((d, HEAD_PAD), lambda i: (0, 0))],
        out_shape=[jax.ShapeDtypeStruct((d, n - MLA_ROPE), BF16), jax.ShapeDtypeStruct((d, HEAD_PAD), BF16)],
        compiler_params=pltpu.CompilerParams(dimension_semantics=("arbitrary",), vmem_limit_bytes=VMEM_LIMIT),
        name="relayout_w_in",
    )(w_in.T, w_in.T)


def _pad_weights(w_in, w_uq, w_ukv):
    w_cat, w_kpe = _relayout_w_in(w_in)
    wuq = jnp.pad(w_uq.reshape(Q_LORA, MLA_HEADS, MLA_QK), ((0, 0), (0, 0), (0, HEAD_PAD - MLA_QK)))
    wkv = w_ukv.reshape(KV_LORA, MLA_HEADS, MLA_NOPE + MLA_V)
    wk = jnp.pad(wkv[..., :MLA_NOPE], ((0, 0), (0, 0), (0, HEAD_PAD - MLA_NOPE)))
    wukv = jnp.concatenate([wk.reshape(KV_LORA, MLA_HEADS * HEAD_PAD),
                            wkv[..., MLA_NOPE:].reshape(KV_LORA, MLA_WIDTH)], axis=1)
    return (w_cat, w_kpe), wuq.reshape(Q_LORA, MLA_HEADS * HEAD_PAD).astype(BF16), wukv.astype(BF16)


def kernel(x, g_pre, w_in, b_gate, g_q, w_uq, g_kv, w_ukv, lb_logits, g_hgrn, w_branch_a, w_branch_b, w_out,
           g_post):
    assert g_pre.shape[0] == 1, "single-layer block"
    s = x.shape[1]
    win, wuq, wukv = _pad_weights(w_in[0], w_uq[0], w_ukv[0])
    ra, rs = _rope_tables(s)
    lower_bound = jax.nn.softmax(lb_logits.astype(F32), axis=0)[0:1]
    ghg = jnp.tile(g_hgrn[0], HG_HEADS)[None, :]

    q, k, vt, ga, o, gt = _proj_call(x, g_pre, win, g_q, wuq, g_kv, wukv, ra, rs, b_gate, lower_bound, ghg)
    attn = _attn_call(q, k, vt, ga)
    return _out_call(x, attn, o, gt, w_branch_a[0].astype(BF16), w_branch_b[0].astype(BF16),
                     w_out[0].astype(BF16), g_post)
```

```python
import functools
import math

import jax
import jax.numpy as jnp
import numpy as np
from jax import lax
from jax.experimental import pallas as pl
from jax.experimental.pallas import tpu as pltpu

F32 = jnp.float32
BF16 = jnp.bfloat16

D_MODEL = 1024
CHUNK = 64
EPS = 1e-6

MLA_HEADS = 8
MLA_NOPE = 64
MLA_ROPE = 32
MLA_V = 64
MLA_QK = MLA_NOPE + MLA_ROPE
Q_LORA = 768
KV_LORA = 256
ROPE_THETA = 10000.0
MLA_WIDTH = MLA_HEADS * MLA_V

HG_HEADS = 8
HG_DK = 64
HG_DV = 64
HG_WIDTH = HG_HEADS * HG_DV

LANES = 128
HEAD_PAD = LANES
HEADS_PER_GROUP = LANES // MLA_V
N_GROUPS = MLA_HEADS // HEADS_PER_GROUP

COL_CQ = 0
COL_CKV = COL_CQ + Q_LORA
COL_KPE = COL_CKV + KV_LORA
COL_GA = COL_KPE + HEAD_PAD
COL_HQ = COL_GA + MLA_WIDTH
COL_HF = COL_HQ + HG_WIDTH
COL_HI = COL_HF + HG_WIDTH
COL_GB = COL_HI + HG_WIDTH
COL_MG = COL_GB + HG_WIDTH
D_IN_PAD = COL_MG + 2 * D_MODEL

VMEM_LIMIT = 56 * 1024 * 1024

TM_PROJ = 512
TQ_ATTN = 512
SUM_ROWS = 16
UNROLL_DIAGONAL = 4
UNROLL_VISIBLE = 12
C_HGRN = 64
TM_OUT = 1024
OUT_PARTS = 2
W_RELAYOUT_COLS = 512


def _sigmoid(z):
    return 1.0 / (1.0 + jnp.exp(-z))


def _rms(t):
    return t * lax.rsqrt(jnp.mean(t * t, axis=-1, keepdims=True) + EPS)


def _dot(a, b):
    return jnp.dot(a, b, preferred_element_type=F32)


def _dot_nt(a, b):
    return lax.dot_general(a, b, (((1,), (1,)), ((), ())), preferred_element_type=F32)


def _cumsum_rows(t):
    rows = t.shape[0]
    row = lax.broadcasted_iota(jnp.int32, t.shape, 0)
    step = 1
    while step < rows:
        t = t + jnp.where(row >= step, pltpu.roll(t, step, 0), 0.0)
        step *= 2
    return t


def _recurrence_stages(proj, lb, gain, o_ref):
    c_len = C_HGRN
    groups = range(HG_HEADS // 2)
    low = lax.broadcasted_iota(jnp.int32, (c_len, LANES), 1) < HG_DK
    causal = (lax.broadcasted_iota(jnp.int32, (2 * c_len, c_len), 0) % c_len
              >= lax.broadcasted_iota(jnp.int32, (2 * c_len, c_len), 1))
    same_head = ((lax.broadcasted_iota(jnp.int32, (LANES, LANES), 0) < HG_DV)
                 == (lax.broadcasted_iota(jnp.int32, (LANES, LANES), 1) < HG_DK))

    def stack_heads(t):
        zero = jnp.zeros_like(t)
        return jnp.concatenate([jnp.where(low, t, zero), jnp.where(low, zero, t)], axis=0)

    def gates(c):
        f = lb + (1.0 - lb) * _sigmoid(proj["hf"][c * c_len:(c + 1) * c_len])
        return 1.0 - f, _cumsum_rows(jnp.log2(f))

    def scores(c, gated):
        k_in, cum = gated
        rows = slice(c * c_len, (c + 1) * c_len)
        mid = cum[c_len // 2 - 1:c_len // 2, :]
        tot = cum[c_len - 1:c_len, :]
        q_mid = proj["hq"][rows] * jnp.exp2(cum - mid)
        k_mid = k_in * jnp.exp2(mid - cum)
        q_dec = (q_mid * jnp.exp2(mid)).astype(BF16)
        k_end = (k_mid * jnp.exp2(tot - mid)).astype(BF16)
        q_mid = q_mid.astype(BF16)
        k_mid = k_mid.astype(BF16)
        v = proj["hi"][rows]
        per_group = []
        for g in groups:
            cols = slice(g * LANES, (g + 1) * LANES)
            a = _dot_nt(stack_heads(q_mid[:, cols]), k_mid[:, cols])
            v_t = v[:, cols].T.astype(BF16)
            upd = _dot(v_t, k_end[:, cols])
            per_group.append((a, v_t, upd, stack_heads(q_dec[:, cols])))
        return per_group, jnp.exp2(tot)

    def output(c, scored, states):
        per_group, decay = scored
        rows = slice(c * c_len, (c + 1) * c_len)
        new_states = []
        for g in groups:
            cols = slice(g * LANES, (g + 1) * LANES)
            a, v_t, upd, q_dec = per_group[g]
            a = jnp.where(causal, a, 0.0).astype(BF16)
            lhs = jnp.concatenate([q_dec, a], axis=1)
            rhs_t = jnp.concatenate([states[g].astype(BF16), v_t], axis=1)
            res = _dot_nt(lhs, rhs_t)
            o = jnp.where(low, res[:c_len], res[c_len:])
            sq = o * o
            msq_lo = jnp.sum(jnp.where(low, sq, 0.0), axis=-1, keepdims=True) * (1.0 / HG_DV)
            msq_hi = jnp.sum(jnp.where(low, 0.0, sq), axis=-1, keepdims=True) * (1.0 / HG_DV)
            inv_rms = jnp.where(low, lax.rsqrt(msq_lo + EPS), lax.rsqrt(msq_hi + EPS))
            o_ref[0, rows, cols] = (o * inv_rms * gain[:, cols] * proj["gb"][rows, cols]).astype(BF16)
            new_states.append(states[g] * decay[:, cols] + jnp.where(same_head, upd, 0.0))
        return new_states

    return gates, scores, output


def _proj_body(x_ref, gpre_ref, wcat_ref, wkpe_ref, gq_ref, wuq_ref, gkv_ref, wukv_ref, ra_ref, rs_ref,
               bg_ref, lb_ref, ghg_ref, q_ref, k_ref, v_ref, ga_ref, o_ref, gt_ref, st_ref):
    tm = x_ref.shape[1]
    groups = range(HG_HEADS // 2)

    @pl.when(pl.program_id(1) == 0)
    def _():
        st_ref[...] = jnp.zeros_like(st_ref)

    h = (_rms(x_ref[0]) * gpre_ref[...]).astype(BF16)

    def proj(lo, width):
        if lo == COL_KPE:
            return _dot(h, wkpe_ref[...])
        if lo > COL_KPE:
            lo -= HEAD_PAD
        return _dot(h, wcat_ref[:, lo:lo + width])

    ra = ra_ref[...]
    rs = rs_ref[...]
    lane = lax.broadcasted_iota(jnp.int32, ra.shape, 1)
    takes_upper = lane < MLA_NOPE + MLA_ROPE // 2

    def rope(t):
        partner = jnp.where(takes_upper, pltpu.roll(t, LANES - MLA_ROPE // 2, 1), pltpu.roll(t, MLA_ROPE // 2, 1))
        return t * ra + partner * rs

    held = {}

    def q_down():
        held["cqn"] = (_rms(proj(COL_CQ, Q_LORA)) * gq_ref[...]).astype(BF16)

    def q_up(first_head, n_heads):
        scale = math.log2(math.e) / math.sqrt(MLA_QK)
        qu = _dot(held["cqn"], wuq_ref[:, first_head * HEAD_PAD:(first_head + n_heads) * HEAD_PAD])
        for i in range(n_heads):
            q_ref[0, first_head + i] = (rope(qu[:, i * HEAD_PAD:(i + 1) * HEAD_PAD]) * scale).T.astype(BF16)

    def kv_down():
        held["ckvn"] = (_rms(proj(COL_CKV, KV_LORA)) * gkv_ref[...]).astype(BF16)
        held["kpe"] = rope(proj(COL_KPE, HEAD_PAD))

    def kv_up():
        kvu = _dot(held["ckvn"], wukv_ref[...])
        for hh in range(MLA_HEADS):
            k_ref[0, hh] = (kvu[:, hh * HEAD_PAD:(hh + 1) * HEAD_PAD] + held["kpe"]).astype(BF16)
        v_ref[0] = kvu[:, MLA_HEADS * HEAD_PAD:].T.astype(BF16)

    def silu_gate(lo, out_ref):
        z = proj(lo, out_ref.shape[-1])
        out_ref[0] = (z * _sigmoid(z)).astype(BF16)

    def merge_gates(c, width=2 * LANES):
        z = proj(COL_MG + c * width, width) + bg_ref[:, c * width:(c + 1) * width]
        gt_ref[0, :, c * width:(c + 1) * width] = _sigmoid(z).astype(BF16)

    half_heads = MLA_HEADS // 2
    others = [q_down, kv_down, functools.partial(q_up, 0, half_heads), kv_up,
              functools.partial(q_up, half_heads, half_heads), functools.partial(silu_gate, COL_GA, ga_ref)]
    pieces = []
    for c in range(D_MODEL // LANES):
        pieces += [functools.partial(merge_gates, c)] + others[c:c + 1]

    def next_piece():
        if pieces:
            pieces.pop(0)()

    rec = {}
    gates, scores, output = _recurrence_stages(rec, lb_ref[...], ghg_ref[...], o_ref)
    n_chunks = tm // C_HGRN
    x = x_ref[0]
    inv_rms = lax.rsqrt(jnp.mean(x * x, axis=-1, keepdims=True) + EPS)
    rec["hf"] = _dot((x * gpre_ref[...]).astype(BF16), wcat_ref[:, COL_HF - HEAD_PAD:COL_HF - HEAD_PAD + HG_WIDTH]) * inv_rms
    gated = {0: gates(0)}
    rec["hq"] = proj(COL_HQ, HG_WIDTH)
    if n_chunks > 1:
        gated[1] = gates(1)
    rec["hi"] = proj(COL_HI, HG_WIDTH)
    scored = {0: scores(0, gated.pop(0))}
    z = proj(COL_GB, HG_WIDTH)
    rec["gb"] = z * _sigmoid(z)
    states = [st_ref[g] for g in groups]
    for c in range(n_chunks):
        if c + 2 < n_chunks:
            gated[c + 2] = gates(c + 2)
        next_piece()
        if c + 1 < n_chunks:
            scored[c + 1] = scores(c + 1, gated.pop(c + 1))
        next_piece()
        states = output(c, scored.pop(c), states)
    for g in groups:
        st_ref[g] = states[g]
    while pieces:
        next_piece()


def _proj_call(x, gpre, win, gq, wuq, gkv, wukv, ra, rs, bg, lb, ghg):
    b, s, _ = x.shape
    tm = min(TM_PROJ, s)
    const = lambda bi, si: (0, 0)
    tok = lambda bi, si: (bi, si, 0)

    def full(a):
        return pl.BlockSpec(a.shape, const, pipeline_mode=pl.Buffered(1))

    def tok_out(width, dtype):
        return jax.ShapeDtypeStruct((b, s, width), dtype), pl.BlockSpec((1, tm, width), tok)

    head_shape = jax.ShapeDtypeStruct((b, MLA_HEADS, s, HEAD_PAD), BF16)
    head_spec = pl.BlockSpec((1, MLA_HEADS, tm, HEAD_PAD), lambda bi, si: (bi, 0, si, 0))
    outs = [(jax.ShapeDtypeStruct((b, MLA_HEADS, HEAD_PAD, s), BF16),
             pl.BlockSpec((1, MLA_HEADS, HEAD_PAD, tm), lambda bi, si: (bi, 0, 0, si))),
            (head_shape, head_spec),
            (jax.ShapeDtypeStruct((b, MLA_WIDTH, s), BF16),
             pl.BlockSpec((1, MLA_WIDTH, tm), lambda bi, si: (bi, 0, si))),
            tok_out(MLA_WIDTH, BF16), tok_out(HG_WIDTH, BF16),
            tok_out(2 * D_MODEL, BF16)]
    rope_spec = pl.BlockSpec((tm, HEAD_PAD), lambda bi, si: (si, 0))
    return pl.pallas_call(
        _proj_body,
        grid=(b, s // tm),
        in_specs=[pl.BlockSpec((1, tm, D_MODEL), tok), full(gpre), *(full(w) for w in win), full(gq), full(wuq),
                  full(gkv), full(wukv), rope_spec, rope_spec, full(bg), full(lb), full(ghg)],
        out_specs=[o[1] for o in outs],
        out_shape=[o[0] for o in outs],
        scratch_shapes=[pltpu.VMEM((HG_HEADS // 2, LANES, LANES), F32)],
        compiler_params=pltpu.CompilerParams(dimension_semantics=("parallel", "arbitrary"),
                                             vmem_limit_bytes=VMEM_LIMIT),
        name="proj_hgrn",
    )(x, gpre, *win, gq, wuq, gkv, wukv, ra, rs, bg, lb, ghg)


def _attn_body(qtab_ref, ktab_ref, q_ref, k_ref, vt_ref, ga_ref, o_ref, s_ref, m_ref, acc_ref):
    tq = s_ref.shape[-1]
    n_q = k_ref.shape[2] // tq
    n_off = n_q * (n_q - 1) // 2
    heads = range(HEADS_PER_GROUP)
    key_chunk = lax.broadcasted_iota(jnp.int32, (tq, tq), 0) // CHUNK
    query_chunk = lax.broadcasted_iota(jnp.int32, (tq, tq), 1) // CHUNK
    visible = key_chunk <= query_chunk
    ones = jnp.ones((SUM_ROWS, tq), BF16)

    half = tq // 2
    assert half % CHUNK == 0 and half % LANES == 0
    early, late, everything = slice(0, half), slice(half, tq), slice(0, tq)

    def column_parts(on_diagonal):
        return [(early, early), (late, everything)] if on_diagonal else [(everything, everything)]

    def score(slot, pos, hh, on_diagonal):
        q_off = qtab_ref[pos] * tq
        k_off = ktab_ref[pos] * tq
        part_max = []
        for queries, keys in column_parts(on_diagonal):
            k_rows = pl.ds(pl.multiple_of(k_off + keys.start, half), keys.stop - keys.start)
            q_cols = pl.ds(pl.multiple_of(q_off + queries.start, half), queries.stop - queries.start)
            sc = _dot(k_ref[0, hh, k_rows, :], q_ref[0, hh, :, q_cols])
            if on_diagonal:
                sc = jnp.where(visible[keys, queries], sc, -jnp.inf)
            s_ref[slot, hh, keys, queries] = sc
            part_max.append(jnp.max(sc, axis=0, keepdims=True))
        return jnp.concatenate(part_max, axis=1)

    def absorb(slot, pos, hh, block_max, on_diagonal):
        qi = qtab_ref[pos]
        k_off = ktab_ref[pos] * tq
        if on_diagonal:
            m_new, alpha = block_max, None
        else:
            m_old = m_ref[qi, hh]
            m_new = jnp.maximum(m_old, block_max)
            alpha = jnp.exp2(m_old - m_new)
        m_ref[qi, hh] = m_new
        for queries, keys in column_parts(on_diagonal):
            p = jnp.exp2(s_ref[slot, hh, keys, queries] - m_new[:, queries]).astype(BF16)
            k_cols = pl.ds(pl.multiple_of(k_off + keys.start, half), keys.stop - keys.start)
            vt = vt_ref[0, hh * MLA_V:(hh + 1) * MLA_V, k_cols]
            pv = _dot(jnp.concatenate([vt, ones[:, keys]], axis=0), p)
            if not on_diagonal:
                pv += alpha[:, queries] * acc_ref[qi, hh, :, queries]
            acc_ref[qi, hh, :, queries] = pv

    def pipeline(first, count, on_diagonal, unroll):
        if count == 0:
            return
        assert unroll % 2 == 0
        n_loop = (count - 1) // unroll

        def several(pos, n, block_max, score_last):
            for u in range(n):
                nxt = []
                for hh in heads:
                    if u + 1 < n or score_last:
                        nxt.append(score((u + 1) % 2, pos + u + 1, hh, on_diagonal))
                    absorb(u % 2, pos + u, hh, block_max[hh], on_diagonal)
                block_max = tuple(nxt)
            return block_max

        first_max = tuple(score(0, first, hh, on_diagonal) for hh in heads)
        block_max = lax.fori_loop(0, n_loop, lambda t, bm: several(first + unroll * t, unroll, bm, True), first_max)
        several(first + unroll * n_loop, count - unroll * n_loop, block_max, False)

    pipeline(0, n_q, True, UNROLL_DIAGONAL)
    pipeline(n_q, n_off, False, UNROLL_VISIBLE)

    for qi in range(n_q):
        out_t = jnp.concatenate([acc_ref[qi, hh, :MLA_V, :] / acc_ref[qi, hh, MLA_V:MLA_V + 1, :] for hh in heads],
                                axis=0)
        rows = slice(qi * tq, (qi + 1) * tq)
        o_ref[0, rows, :] = (out_t.T * ga_ref[0, rows, :].astype(F32)).astype(BF16)


def _attn_call(q, k, vt, ga):
    b, _, s, _ = k.shape
    tq = min(TQ_ATTN, s)
    n_q = s // tq
    off = [(qi, kj) for kj in range(n_q) for qi in range(kj + 1, n_q)]
    order = [(qi, qi) for qi in range(n_q)] + off
    qtab = jnp.asarray([p[0] for p in order], jnp.int32)
    ktab = jnp.asarray([p[1] for p in order], jnp.int32)
    grid_spec = pltpu.PrefetchScalarGridSpec(
        num_scalar_prefetch=2,
        grid=(b, N_GROUPS),
        in_specs=[pl.BlockSpec((1, HEADS_PER_GROUP, HEAD_PAD, s), lambda bi, g, qt, kt: (bi, g, 0, 0)),
                  pl.BlockSpec((1, HEADS_PER_GROUP, s, HEAD_PAD), lambda bi, g, qt, kt: (bi, g, 0, 0)),
                  pl.BlockSpec((1, LANES, s), lambda bi, g, qt, kt: (bi, g, 0)),
                  pl.BlockSpec((1, s, LANES), lambda bi, g, qt, kt: (bi, 0, g))],
        out_specs=pl.BlockSpec((1, s, LANES), lambda bi, g, qt, kt: (bi, 0, g)),
        scratch_shapes=[pltpu.VMEM((2, HEADS_PER_GROUP, tq, tq), F32),
                        pltpu.VMEM((n_q, HEADS_PER_GROUP, 1, tq), F32),
                        pltpu.VMEM((n_q, HEADS_PER_GROUP, MLA_V + SUM_ROWS, tq), F32)])
    return pl.pallas_call(
        _attn_body,
        grid_spec=grid_spec,
        out_shape=jax.ShapeDtypeStruct((b, s, MLA_WIDTH), BF16),
        compiler_params=pltpu.CompilerParams(dimension_semantics=("parallel", "parallel"),
                                             vmem_limit_bytes=VMEM_LIMIT),
        name="attn",
    )(qtab, ktab, q, k, vt, ga)


def _out_body(x_ref, attn_ref, o_ref, gt_ref, wa_ref, wb_ref, wo_ref, gpost_ref, out_ref):
    tm = x_ref.shape[1]
    parts = [slice(i * tm // OUT_PARTS, (i + 1) * tm // OUT_PARTS) for i in range(OUT_PARTS)]

    def merged(r):
        ya = _dot(attn_ref[0, r], wa_ref[...])
        yb = _dot(o_ref[0, r], wb_ref[...])
        m = gt_ref[0, r, :D_MODEL].astype(F32) * ya + gt_ref[0, r, D_MODEL:].astype(F32) * yb
        return m.astype(BF16)

    ms = [merged(r) for r in parts]
    ys = [_dot(m, wo_ref[...]) for m in ms]
    for r, y in zip(parts, ys):
        out_ref[0, r] = x_ref[0, r] + _rms(y) * gpost_ref[...]


def _out_call(x, attn, o, gt, wa, wb, wo, gpost):
    b, s, _ = x.shape
    tm = min(TM_OUT, s)
    tok = lambda bi, si: (bi, si, 0)
    const = lambda bi, si: (0, 0)

    def tspec(a):
        return pl.BlockSpec((1, tm, a.shape[-1]), tok)

    def full(a):
        return pl.BlockSpec(a.shape, const)

    return pl.pallas_call(
        _out_body,
        grid=(b, s // tm),
        in_specs=[tspec(x), tspec(attn), tspec(o), tspec(gt), full(wa), full(wb), full(wo), full(gpost)],
        out_specs=tspec(x),
        out_shape=jax.ShapeDtypeStruct(x.shape, x.dtype),
        compiler_params=pltpu.CompilerParams(dimension_semantics=("parallel", "parallel"),
                                             vmem_limit_bytes=VMEM_LIMIT),
        name="merge_out",
    )(x, attn, o, gt, wa, wb, wo, gpost)


def _rope_tables(s):
    inv = ROPE_THETA ** (-np.arange(0, MLA_ROPE, 2, dtype=np.float64) / MLA_ROPE)
    ang = np.arange(s, dtype=np.float64)[:, None] * inv[None, :]
    cos, sin = np.cos(ang), np.sin(ang)
    pad = np.zeros((s, HEAD_PAD - MLA_QK))
    ra = np.concatenate([np.ones((s, MLA_NOPE)), cos, cos, pad], axis=1)
    rs = np.concatenate([np.zeros((s, MLA_NOPE)), -sin, sin, pad], axis=1)
    return jnp.asarray(ra, F32), jnp.asarray(rs, F32)


def _relayout_body(wt_ref, kpe_src_ref, out_ref, kpe_ref):
    out_ref[...] = wt_ref[...].T.astype(BF16)

    @pl.when(pl.program_id(0) == 0)
    def _():
        t = pltpu.roll(kpe_src_ref[...].T, MLA_NOPE, 1)
        lane = lax.broadcasted_iota(jnp.int32, t.shape, 1)
        kpe_ref[...] = jnp.where((lane >= MLA_NOPE) & (lane < MLA_QK), t, 0.0).astype(BF16)


def _relayout_w_in(w_in):
    d, n = w_in.shape
    kpe_lo = Q_LORA + KV_LORA
    wide_lo = kpe_lo + MLA_ROPE
    cols = W_RELAYOUT_COLS
    assert kpe_lo % cols == 0 and (n - wide_lo) % cols == 0
    n_lat = kpe_lo // cols

    def source_row(i):
        return pl.multiple_of(i * cols + jnp.minimum(i // n_lat, 1) * MLA_ROPE, MLA_ROPE)

    return pl.pallas_call(
        _relayout_body,
        grid=((n - MLA_ROPE) // cols,),
        in_specs=[pl.BlockSpec((pl.Element(cols), pl.Element(d)), lambda i: (source_row(i), 0)),
                  pl.BlockSpec((pl.Element(HEAD_PAD), pl.Element(d)), lambda i: (kpe_lo, 0))],
        out_specs=[pl.BlockSpec((d, cols), lambda i: (0, i)), pl.BlockSpec((d, HEAD_PAD), lambda i: (0, 0))],
        out_shape=[jax.ShapeDtypeStruct((d, n - MLA_ROPE), BF16), jax.ShapeDtypeStruct((d, HEAD_PAD), BF16)],
        compiler_params=pltpu.CompilerParams(dimension_semantics=("arbitrary",), vmem_limit_bytes=VMEM_LIMIT),
        name="relayout_w_in",
    )(w_in.T, w_in.T)


def _pad_weights(w_in, w_uq, w_ukv):
    w_cat, w_kpe = _relayout_w_in(w_in)
    wuq = jnp.pad(w_uq.reshape(Q_LORA, MLA_HEADS, MLA_QK), ((0, 0), (0, 0), (0, HEAD_PAD - MLA_QK)))
    wkv = w_ukv.reshape(KV_LORA, MLA_HEADS, MLA_NOPE + MLA_V)
    wk = jnp.pad(wkv[..., :MLA_NOPE], ((0, 0), (0, 0), (0, HEAD_PAD - MLA_NOPE)))
    wukv = jnp.concatenate([wk.reshape(KV_LORA, MLA_HEADS * HEAD_PAD),
                            wkv[..., MLA_NOPE:].reshape(KV_LORA, MLA_WIDTH)], axis=1)
    return (w_cat, w_kpe), wuq.reshape(Q_LORA, MLA_HEADS * HEAD_PAD).astype(BF16), wukv.astype(BF16)


def kernel(x, g_pre, w_in, b_gate, g_q, w_uq, g_kv, w_ukv, lb_logits, g_hgrn, w_branch_a, w_branch_b, w_out,
           g_post):
    assert g_pre.shape[0] == 1, "single-layer block"
    s = x.shape[1]
    win, wuq, wukv = _pad_weights(w_in[0], w_uq[0], w_ukv[0])
    ra, rs = _rope_tables(s)
    lower_bound = jax.nn.softmax(lb_logits.astype(F32), axis=0)[0:1]
    ghg = jnp.tile(g_hgrn[0], HG_HEADS)[None, :]

    q, k, vt, ga, o, gt = _proj_call(x, g_pre, win, g_q, wuq, g_kv, wukv, ra, rs, b_gate, lower_bound, ghg)
    attn = _attn_call(q, k, vt, ga)
    return _out_call(x, attn, o, gt, w_branch_a[0].astype(BF16), w_branch_b[0].astype(BF16),
                     w_out[0].astype(BF16), g_post)
```
